```python
import math
import jax, jax.numpy as jnp
from jax import lax
import numpy as np

D_MODEL = 1024
BATCH = 16
SEQ = 256
DEPTH = 4
DEC_BATCH = 2
DEC_SEQ = 1024
PAST_LEN = 256

GRID_W = 64
D_MIX = D_MODEL
D_RNN = D_MIX // 2
N_LRU_HEADS = 8
LRU_HEAD_DIM = D_RNN // N_LRU_HEADS
LRU_C = 8.0
CONV_WIDTH = 4
CONV_LEFT = CONV_WIDTH // 2
D_POOL = D_MIX // 4
POOL_WINDOWS = (2, 4, 8, 16)
N_POOL_GROUPS = len(POOL_WINDOWS)
POOL_GROUP_DIM = D_POOL // N_POOL_GROUPS
D_FOURIER = D_MIX // 4
N_FOURIER_HEADS = 4
FOURIER_HEAD_DIM = D_FOURIER // N_FOURIER_HEADS
D_IN = 2 * D_RNN + D_POOL + D_FOURIER
D_FF = 4 * D_MODEL
N_MOD = 6
EPS = 1e-6

kernel_name = "hybrid_rglru_pool_fourier_diffusion_step"


def rms_norm(x, g):
    xf = x.astype(jnp.float32)
    y = xf * lax.rsqrt(jnp.mean(xf * xf, axis=-1, keepdims=True) + EPS)
    return (y * g.astype(jnp.float32)).astype(x.dtype)


def centred_dwconv(x, w, b):
    L = x.shape[1]
    xp = jnp.pad(x, ((0, 0), (CONV_LEFT, CONV_WIDTH - 1 - CONV_LEFT), (0, 0)))
    y = b
    for k in range(CONV_WIDTH):
        y = y + xp[:, k:k + L] * w[k]
    return y


def _lin_combine(e1, e2):
    a1, b1 = e1
    a2, b2 = e2
    return a1 * a2, a2 * b1 + b2


def rg_lru(x, w_r, b_r, w_i, b_i, lam, h0, reverse):
    B, L, _ = x.shape
    xh = x.reshape(B, L, N_LRU_HEADS, LRU_HEAD_DIM)
    r = jax.nn.sigmoid((jnp.einsum('blhi,hij->blhj', xh, w_r).reshape(B, L, D_RNN) + b_r).astype(jnp.float32))
    i = jax.nn.sigmoid((jnp.einsum('blhi,hij->blhj', xh, w_i).reshape(B, L, D_RNN) + b_i).astype(jnp.float32))
    log_a = -LRU_C * r * jax.nn.softplus(-lam.astype(jnp.float32))
    a = jnp.exp(log_a)
    mult = jnp.sqrt(-jnp.expm1(2.0 * log_a))
    bterm = mult * (i * x.astype(jnp.float32))
    a_cum, b_cum = lax.associative_scan(_lin_combine, (a, bterm), axis=1, reverse=reverse)
    hs = a_cum * h0.astype(jnp.float32)[:, None, :] + b_cum
    final = hs[:, 0] if reverse else hs[:, -1]
    return hs, final


def _bounds(n, w):
    idx = np.arange(n)
    lo = np.clip(idx - w // 2, 0, n)
    hi = np.clip(idx - w // 2 + w, 0, n)
    return lo, hi, (hi - lo).astype(np.float32)


def pool_1d(x, w):
    L = x.shape[1]
    cs = jnp.pad(jnp.cumsum(x, axis=1), ((0, 0), (1, 0), (0, 0)))
    lo, hi, cnt = _bounds(L, w)
    return (cs[:, hi] - cs[:, lo]) / jnp.asarray(cnt)[None, :, None]


def pool_2d(x, w, rows):
    B, L, C = x.shape
    g = x.reshape(B, rows, GRID_W, C)
    cs = jnp.cumsum(jnp.cumsum(g, axis=1), axis=2)
    cs = jnp.pad(cs, ((0, 0), (1, 0), (1, 0), (0, 0)))
    rlo, rhi, rc = _bounds(rows, w)
    clo, chi, cc = _bounds(GRID_W, w)
    top_hi = cs[:, rhi]
    top_lo = cs[:, rlo]
    s = top_hi[:, :, chi] - top_lo[:, :, chi] - top_hi[:, :, clo] + top_lo[:, :, clo]
    cnt = jnp.asarray(np.outer(rc, cc).astype(np.float32))
    return (s / cnt[None, :, :, None]).reshape(B, L, C)


def mixer(h, p, l, h0, grid_rows):
    B, L, _ = h.shape
    u = h @ p['w_in'][l]
    xr = u[..., :D_RNN]
    xg = u[..., D_RNN:2 * D_RNN]
    xp = u[..., 2 * D_RNN:2 * D_RNN + D_POOL]
    xf = u[..., 2 * D_RNN + D_POOL:]

    xc = centred_dwconv(xr, p['conv_w'][l], p['conv_b'][l])
    hf, sf = rg_lru(xc, p['lru_wr'][l, 0], p['lru_br'][l, 0], p['lru_wi'][l, 0], p['lru_bi'][l, 0],
                    p['lru_lambda'][l, 0], h0[:, 0], reverse=False)
    hb, sb = rg_lru(xc, p['lru_wr'][l, 1], p['lru_br'][l, 1], p['lru_wi'][l, 1], p['lru_bi'][l, 1],
                    p['lru_lambda'][l, 1], h0[:, 1], reverse=True)
    out_a = ((hf + hb) * jax.nn.gelu(xg.astype(jnp.float32))).astype(h.dtype)
    final_state = jnp.stack([sf, sb], axis=1)

    xpf = xp.astype(jnp.float32)
    pooled = []
    for gi, w in enumerate(POOL_WINDOWS):
        xs = xpf[..., gi * POOL_GROUP_DIM:(gi + 1) * POOL_GROUP_DIM]
        m = pool_1d(xs, w) if grid_rows is None else pool_2d(xs, w, grid_rows)
        pooled.append(m - xs)
    pooled = jnp.stack(pooled, axis=2).astype(h.dtype)
    out_b = jnp.einsum('blgi,gij->blgj', pooled, p['pool_w'][l]).reshape(B, L, D_POOL) * p['pool_scale'][l]

    xfh = xf.astype(jnp.float32).reshape(B, L, N_FOURIER_HEADS, FOURIER_HEAD_DIM)
    four = jnp.real(jnp.fft.fft2(xfh, axes=(1, 3), norm='ortho')).reshape(B, L, D_FOURIER).astype(h.dtype)
    out_c = four @ p['fourier_w'][l]

    cat = jnp.concatenate([out_a, out_b, out_c], axis=-1)
    return cat @ p['w_out'][l], final_state


def run_layer(x, cond, p, l, h0, grid_rows):
    mods = jax.nn.silu(cond) @ p['w_mod'][l] + p['b_mod'][l]
    mods = mods.reshape(mods.shape[:-1] + (N_MOD, D_MODEL))
    if mods.ndim == 2:
        mods = mods[None]
    mods = mods[:, None]
    sh1, sc1, g1, sh2, sc2, g2 = [mods[:, :, k] for k in range(N_MOD)]
    h = rms_norm(x, p['norm1_g'][l]) * (1.0 + sc1) + sh1
    out, st = mixer(h, p, l, h0, grid_rows)
    x = x + g1 * out
    h2 = rms_norm(x, p['norm2_g'][l]) * (1.0 + sc2) + sh2
    ff = jnp.square(jax.nn.relu(h2 @ p['mlp_w1'][l])) @ p['mlp_w2'][l]
    x = x + g2 * ff
    return x, st


def setup_inputs(seed: int = 0) -> dict:
    key = jax.random.key(seed)
    ks = jax.random.split(key, 24)

    def nrm(k, shape, scale):
        return jax.random.normal(k, shape, jnp.float32) * scale

    u = jax.random.uniform(ks[16], (DEPTH, 2, D_RNN), jnp.float32, minval=0.9, maxval=0.999)
    s = u ** (1.0 / LRU_C)
    lam = jnp.log(s) - jnp.log1p(-s)
    return {
        'x_prompt': nrm(ks[0], (BATCH, SEQ, D_MODEL), 1.0),
        'x_sample': nrm(ks[1], (DEC_BATCH, DEC_SEQ, D_MODEL), 1.0),
        'state_rglru': nrm(ks[2], (DEC_BATCH, DEPTH, 2, D_RNN), 0.5),
        'c': nrm(ks[3], (DEC_BATCH, D_MODEL), 1.0),
        'c_ctx': nrm(ks[4], (D_MODEL,), 1.0),
        'norm1_g': 1.0 + nrm(ks[5], (DEPTH, D_MODEL), 0.1),
        'norm2_g': 1.0 + nrm(ks[6], (DEPTH, D_MODEL), 0.1),
        'final_g': 1.0 + nrm(ks[7], (D_MODEL,), 0.1),
        'w_mod': nrm(ks[8], (DEPTH, D_MODEL, N_MOD * D_MODEL), 0.5 * D_MODEL ** -0.5),
        'b_mod': nrm(ks[9], (DEPTH, N_MOD * D_MODEL), 0.02),
        'w_in': nrm(ks[10], (DEPTH, D_MODEL, D_IN), D_MODEL ** -0.5),
        'conv_w': nrm(ks[11], (DEPTH, CONV_WIDTH, D_RNN), CONV_WIDTH ** -0.5),
        'conv_b': nrm(ks[12], (DEPTH, D_RNN), 0.02),
        'lru_wr': nrm(ks[13], (DEPTH, 2, N_LRU_HEADS, LRU_HEAD_DIM, LRU_HEAD_DIM), LRU_HEAD_DIM ** -0.5),
        'lru_br': nrm(ks[14], (DEPTH, 2, D_RNN), 0.1),
        'lru_wi': nrm(ks[15], (DEPTH, 2, N_LRU_HEADS, LRU_HEAD_DIM, LRU_HEAD_DIM), LRU_HEAD_DIM ** -0.5),
        'lru_bi': nrm(ks[17], (DEPTH, 2, D_RNN), 0.1),
        'lru_lambda': lam,
        'pool_w': nrm(ks[18], (DEPTH, N_POOL_GROUPS, POOL_GROUP_DIM, POOL_GROUP_DIM), POOL_GROUP_DIM ** -0.5),
        'pool_scale': 1.0 + nrm(ks[19], (DEPTH, D_POOL), 0.1),
        'fourier_w': nrm(ks[20], (DEPTH, D_FOURIER, D_FOURIER), D_FOURIER ** -0.5),
        'w_out': nrm(ks[21], (DEPTH, D_MIX, D_MODEL), D_MIX ** -0.5),
        'mlp_w1': nrm(ks[22], (DEPTH, D_MODEL, D_FF), D_MODEL ** -0.5),
        'mlp_w2': nrm(ks[23], (DEPTH, D_FF, D_MODEL), D_FF ** -0.5),
    }


def reference(x_prompt, x_sample, state_rglru, c, c_ctx, norm1_g, norm2_g, final_g, w_mod, b_mod, w_in,
              conv_w, conv_b, lru_wr, lru_br, lru_wi, lru_bi, lru_lambda, pool_w, pool_scale, fourier_w,
              w_out, mlp_w1, mlp_w2):
    p = {'norm1_g': norm1_g, 'norm2_g': norm2_g, 'w_mod': w_mod, 'b_mod': b_mod, 'w_in': w_in,
         'conv_w': conv_w, 'conv_b': conv_b, 'lru_wr': lru_wr, 'lru_br': lru_br, 'lru_wi': lru_wi,
         'lru_bi': lru_bi, 'lru_lambda': lru_lambda, 'pool_w': pool_w, 'pool_scale': pool_scale,
         'fourier_w': fourier_w, 'w_out': w_out, 'mlp_w1': mlp_w1, 'mlp_w2': mlp_w2}

    xc = x_prompt
    h0_ctx = jnp.zeros((x_prompt.shape[0], 2, D_RNN), x_prompt.dtype)
    states = []
    for l in range(DEPTH):
        xc, st = run_layer(xc, c_ctx, p, l, h0_ctx, None)
        states.append(st.astype(x_prompt.dtype))
    y_prompt = rms_norm(xc, final_g)
    new_state_rglru = jnp.stack(states, axis=1)

    rows = x_sample.shape[1] // GRID_W
    xs = x_sample
    for l in range(DEPTH):
        xs, _ = run_layer(xs, c, p, l, state_rglru[:, l], rows)
    y_sample = rms_norm(xs, final_g)

    return (y_prompt, y_sample, new_state_rglru)
```

```python
import functools
import math

import numpy as np
import jax
import jax.numpy as jnp
from jax import lax
from jax.experimental import pallas as pl
from jax.experimental.pallas import tpu as pltpu

D_MODEL = 1024
DEPTH = 4
GRID_W = 64
D_RNN = 512
N_LRU_HEADS = 8
LRU_HEAD_DIM = 64
LRU_C = 8.0
CONV_WIDTH = 4
CONV_LEFT = 2
D_POOL = 256
D_FOURIER = 256
N_FOURIER_HEADS = 4
FOURIER_HEAD_DIM = 64
D_IN = 2 * D_RNN + D_POOL + D_FOURIER
D_FF = 4 * D_MODEL
N_MOD = 6
EPS = 1e-6

LANES = 128
SUBLANES = 8
TILE = 1024
CHUNK = TILE // SUBLANES
N_SLAB = D_RNN // LANES
FF_CHUNK = 1024
MODS_ROWS = 8
VMEM_LIMIT = 60 * 1024 * 1024

F32 = jnp.float32
BF16 = jnp.bfloat16


def _dot(a, b):
    return jnp.dot(a, b, preferred_element_type=F32)


def _row_iota(shape):
    return lax.broadcasted_iota(jnp.int32, shape, 0)


def _shift_rows(x, d):
    n = x.shape[0]
    return pltpu.roll(x, (-d) % n, axis=0)


def _mods_kernel(cond_ref, w_ref, b_ref, o_ref):
    s = cond_ref[...]
    s = s * jax.nn.sigmoid(s)
    o_ref[0] = _dot(s.astype(BF16), w_ref[0].astype(BF16)) + b_ref[0]


def _mods_call(cond, w_mod, b_mod):
    n_tiles = (N_MOD * D_MODEL) // D_MODEL
    return pl.pallas_call(
        _mods_kernel,
        grid=(DEPTH, n_tiles),
        in_specs=[
            pl.BlockSpec((MODS_ROWS, D_MODEL), lambda l, j: (0, 0)),
            pl.BlockSpec((1, D_MODEL, D_MODEL), lambda l, j: (l, 0, j)),
            pl.BlockSpec((1, 1, D_MODEL), lambda l, j: (l, 0, j)),
        ],
        out_specs=pl.BlockSpec((1, MODS_ROWS, D_MODEL), lambda l, j: (l, 0, j)),
        out_shape=jax.ShapeDtypeStruct((DEPTH, MODS_ROWS, N_MOD * D_MODEL), F32),
        compiler_params=pltpu.CompilerParams(dimension_semantics=("arbitrary", "arbitrary")),
        name="mods",
    )(cond, w_mod, b_mod.reshape(DEPTH, 1, N_MOD * D_MODEL))


def _window_sums(x, pos, length, unit):
    fwd = x
    bwd = jnp.where(pos >= 1, _shift_rows(x, -unit), 0.0)
    outs = [fwd + bwd]
    for k in (1, 2, 4):
        fwd = fwd + jnp.where(pos + k < length, _shift_rows(fwd, k * unit), 0.0)
        bwd = bwd + jnp.where(pos - k >= 0, _shift_rows(bwd, -k * unit), 0.0)
        outs.append(fwd + bwd)
    return outs


def _pool_axis(x, pos, length, unit):
    s1, s2, s4, s8 = _window_sums(x, pos, length, unit)
    lane = lax.broadcasted_iota(jnp.int32, x.shape, 1)
    sel = jnp.where(lane < 64, s1, jnp.where(lane < 128, s2, jnp.where(lane < 192, s4, s8)))
    half = jnp.where(lane < 64, 1, jnp.where(lane < 128, 2, jnp.where(lane < 192, 4, 8)))
    cnt = jnp.minimum(pos + half, length) - jnp.maximum(pos - half, 0)
    return sel, cnt.astype(F32)


def _mixer_kernel(x_ref, mods_ref, g1_ref, win_ref, cw_ref, cb_ref, wg_ref, bg_ref, lam_ref, h0_ref,
                  pw_ref, ps_ref, cdft_ref, fw_ref, dft_ref, wout_ref,
                  xo_ref, st_ref,
                  u_ref, xc_ref, gate_ref, scan_ref, cat_ref, *, seq_len):
    n_seq = TILE // seq_len
    chunks_per_seq = seq_len // CHUNK
    mods = mods_ref[0, 0]

    x = x_ref[...]
    ms = jnp.mean(x * x, axis=-1, keepdims=True)
    h = x * lax.rsqrt(ms + EPS) * g1_ref[0]
    h = h * (1.0 + mods[1:2]) + mods[0:1]
    u_ref[...] = _dot(h.astype(BF16), win_ref[0])

    xr = u_ref[:, 0:D_RNN]
    pos = _row_iota((TILE, D_RNN)) & (seq_len - 1)
    acc = cb_ref[0] + xr * cw_ref[0, CONV_LEFT:CONV_LEFT + 1, :]
    for k in range(CONV_WIDTH):
        d = k - CONV_LEFT
        if d == 0:
            continue
        valid = (pos + d >= 0) & (pos + d < seq_len)
        acc = acc + jnp.where(valid, _shift_rows(xr, d), 0.0) * cw_ref[0, k:k + 1, :]
    xc_ref[...] = acc

    xcb = xc_ref[...].astype(BF16)
    for d in range(2):
        for gate in range(2):
            for blk in range(2):
                col = gate * D_RNN + blk * 256
                gate_ref[:, col:col + 256] = _dot(xcb[:, blk * 256:(blk + 1) * 256],
                                                  wg_ref[0, d * 4 + gate * 2 + blk])
        lam = lam_ref[0, d:d + 1, :]
        softplus_neg_lam = jnp.maximum(-lam, 0.0) + jnp.log1p(jnp.exp(-jnp.abs(lam)))
        b_r = bg_ref[0, 2 * d:2 * d + 1, :]
        b_i = bg_ref[0, 2 * d + 1:2 * d + 2, :]

        def gate_chunk(c, carry, d=d, softplus_neg_lam=softplus_neg_lam, b_r=b_r, b_i=b_i):
            r0 = pl.multiple_of(c * CHUNK, CHUNK)
            r = jax.nn.sigmoid(gate_ref[pl.ds(r0, CHUNK), 0:D_RNN] + b_r)
            i = jax.nn.sigmoid(gate_ref[pl.ds(r0, CHUNK), D_RNN:2 * D_RNN] + b_i)
            log_a = -LRU_C * r * softplus_neg_lam
            a = jnp.exp(log_a)
            th = jnp.tanh(log_a)
            mult = jnp.sqrt(-2.0 * th / (1.0 - th))
            b = mult * (i * xc_ref[pl.ds(r0, CHUNK), :])
            for s in range(N_SLAB):
                scan_ref[2 * d, s, pl.ds(c, CHUNK, stride=SUBLANES), :] = a[:, s * LANES:(s + 1) * LANES]
                scan_ref[2 * d + 1, s, pl.ds(c, CHUNK, stride=SUBLANES), :] = b[:, s * LANES:(s + 1) * LANES]
            return carry

        lax.fori_loop(0, SUBLANES, gate_chunk, 0)

    def load(arr, s, t):
        return scan_ref[arr, s, pl.ds(pl.multiple_of(t * SUBLANES, SUBLANES), SUBLANES), :]

    def pass1(t, carry):
        hf, pf, hb, pb = carry
        tb = CHUNK - 1 - t
        nhf, npf, nhb, npb = [], [], [], []
        for s in range(N_SLAB):
            af = load(0, s, t)
            ab = load(2, s, tb)
            nhf.append(af * hf[s] + load(1, s, t))
            npf.append(af * pf[s])
            nhb.append(ab * hb[s] + load(3, s, tb))
            npb.append(ab * pb[s])
        return tuple(nhf), tuple(npf), tuple(nhb), tuple(npb)

    zeros = tuple(jnp.zeros((SUBLANES, LANES), F32) for _ in range(N_SLAB))
    ones = tuple(jnp.ones((SUBLANES, LANES), F32) for _ in range(N_SLAB))
    ef, pf, eb, pb = lax.fori_loop(0, CHUNK, pass1, (zeros, ones, zeros, ones), unroll=4)

    row = _row_iota((SUBLANES, LANES))
    seq_first = (row & (chunks_per_seq - 1)) == 0
    seq_last = (row & (chunks_per_seq - 1)) == chunks_per_seq - 1
    init_f, init_b = [], []
    for s in range(N_SLAB):
        h0f = jnp.broadcast_to(h0_ref[0, 0, 0:1, s * LANES:(s + 1) * LANES], (SUBLANES, LANES))
        h0b = jnp.broadcast_to(h0_ref[0, 0, 1:2, s * LANES:(s + 1) * LANES], (SUBLANES, LANES))
        cf, cb = h0f, h0b
        for _ in range(chunks_per_seq - 1):
            cf = jnp.where(seq_first, h0f, pltpu.roll(ef[s] + pf[s] * cf, 1, axis=0))
            cb = jnp.where(seq_last, h0b, pltpu.roll(eb[s] + pb[s] * cb, SUBLANES - 1, axis=0))
        init_f.append(cf)
        init_b.append(cb)

    def pass2(t, carry):
        hf, hb = carry
        tb = CHUNK - 1 - t
        nhf, nhb = [], []
        for s in range(N_SLAB):
            vf = load(0, s, t) * hf[s] + load(1, s, t)
            vb = load(2, s, tb) * hb[s] + load(3, s, tb)
            scan_ref[1, s, pl.ds(pl.multiple_of(t * SUBLANES, SUBLANES), SUBLANES), :] = vf
            scan_ref[3, s, pl.ds(pl.multiple_of(tb * SUBLANES, SUBLANES), SUBLANES), :] = vb
            nhf.append(vf)
            nhb.append(vb)
        return tuple(nhf), tuple(nhb)

    lax.fori_loop(0, CHUNK, pass2, (tuple(init_f), tuple(init_b)), unroll=4)

    if st_ref is not None:
        even = (_row_iota((SUBLANES, LANES)) & 1) == 0
        for s in range(N_SLAB):
            last_f = scan_ref[1, s, pl.ds((CHUNK - 1) * SUBLANES, SUBLANES), :]
            first_b = scan_ref[3, s, pl.ds(0, SUBLANES), :]
            st_ref[0, :, s * LANES:(s + 1) * LANES] = jnp.where(
                even, pltpu.roll(last_f, SUBLANES - 1, axis=0), pltpu.roll(first_b, 1, axis=0))

    def out_chunk(c, carry):
        r0 = pl.multiple_of(c * CHUNK, CHUNK)
        for s in range(N_SLAB):
            hsum = (scan_ref[1, s, pl.ds(c, CHUNK, stride=SUBLANES), :]
                    + scan_ref[3, s, pl.ds(c, CHUNK, stride=SUBLANES), :])
            xg = u_ref[pl.ds(r0, CHUNK), D_RNN + s * LANES:D_RNN + (s + 1) * LANES]
            cat_ref[pl.ds(r0, CHUNK), s * LANES:(s + 1) * LANES] = (hsum * jax.nn.gelu(xg)).astype(BF16)
        return carry

    lax.fori_loop(0, SUBLANES, out_chunk, 0)

    xp = u_ref[:, 2 * D_RNN:2 * D_RNN + D_POOL]
    tok = _row_iota((TILE, D_POOL))
    if seq_len == TILE:
        n_rows = TILE // GRID_W
        col_sum, col_cnt = _pool_axis(xp, tok & (GRID_W - 1), GRID_W, 1)
        win_sum, row_cnt = _pool_axis(col_sum, lax.shift_right_logical(tok, GRID_W.bit_length() - 1),
                                      n_rows, GRID_W)
        cnt = row_cnt * col_cnt
    else:
        win_sum, cnt = _pool_axis(xp, tok & (seq_len - 1), seq_len, 1)
    pooled = win_sum / cnt - xp
    cat_ref[:, D_RNN:D_RNN + D_POOL] = (_dot(pooled.astype(BF16), pw_ref[0]) * ps_ref[0]).astype(BF16)

    xf = u_ref[:, 2 * D_RNN + D_POOL:D_IN].astype(BF16)
    y1 = _dot(xf, cdft_ref[...]).astype(BF16)
    for q in range(n_seq):
        rows = slice(q * seq_len, (q + 1) * seq_len)
        four = _dot(dft_ref[0], y1[rows, 0:D_FOURIER]) + _dot(dft_ref[1], y1[rows, D_FOURIER:])
        cat_ref[rows, D_RNN + D_POOL:] = _dot(four.astype(BF16), fw_ref[0]).astype(BF16)

    xo_ref[...] = x_ref[...] + mods[2:3] * _dot(cat_ref[...], wout_ref[0])


def _mixer_call(x, mods, layer, cond_row0, p, h0, h0_index, dft, seq_len, with_state):
    n_tiles = x.shape[0] // TILE
    n_seq = TILE // seq_len
    lyr = lambda *shape: pl.BlockSpec((1,) + shape, lambda g: (layer,) + (0,) * len(shape))
    in_specs = [
        pl.BlockSpec((TILE, D_MODEL), lambda g: (g, 0)),
        pl.BlockSpec((1, 1, N_MOD, D_MODEL), lambda g: (layer, cond_row0(g), 0, 0)),
        lyr(1, D_MODEL),
        lyr(D_MODEL, D_IN),
        lyr(CONV_WIDTH, D_RNN),
        lyr(1, D_RNN),
        lyr(8, 256, 256),
        lyr(4, D_RNN),
        lyr(2, D_RNN),
        pl.BlockSpec((1, 1, 2, D_RNN), h0_index),
        lyr(D_POOL, D_POOL),
        lyr(1, D_POOL),
        pl.BlockSpec((D_FOURIER, 2 * D_FOURIER), lambda g: (0, 0)),
        lyr(D_FOURIER, D_FOURIER),
        pl.BlockSpec((2, seq_len, seq_len), lambda g: (0, 0, 0)),
        lyr(D_MODEL, D_MODEL),
    ]
    out_specs = [pl.BlockSpec((TILE, D_MODEL), lambda g: (g, 0))]
    out_shape = [jax.ShapeDtypeStruct(x.shape, F32)]
    if with_state:
        out_specs.append(pl.BlockSpec((1, 2 * n_seq, D_RNN), lambda g: (g, 0, 0)))
        out_shape.append(jax.ShapeDtypeStruct((n_tiles, 2 * n_seq, D_RNN), F32))
        body = functools.partial(_mixer_kernel, seq_len=seq_len)
    else:
        def body(*refs):
            _mixer_kernel(*refs[:17], None, *refs[17:], seq_len=seq_len)
    return pl.pallas_call(
        body,
        grid=(n_tiles,),
        in_specs=in_specs,
        out_specs=out_specs,
        out_shape=out_shape,
        scratch_shapes=[
            pltpu.VMEM((TILE, D_IN), F32),
            pltpu.VMEM((TILE, D_RNN), F32),
            pltpu.VMEM((TILE, 2 * D_RNN), F32),
            pltpu.VMEM((4, N_SLAB, TILE, LANES), F32),
            pltpu.VMEM((TILE, D_MODEL), BF16),
        ],
        compiler_params=pltpu.CompilerParams(dimension_semantics=("arbitrary",),
                                             vmem_limit_bytes=VMEM_LIMIT),
        name="mixer_ctx" if with_state else "mixer_lat",
    )(x, mods, p['norm1_g'], p['w_in'], p['conv_w'], p['conv_b'], p['w_gate'], p['b_gate'], p['lam'], h0,
      p['pool_w'], p['pool_scale'], p['cdft'], p['fourier_w'], dft, p['w_out'])


def _mlp_kernel(x_ref, mods_ref, g2_ref, w1_ref, w2_ref, fg_ref, o_ref, h2_ref, acc_ref, *, final):
    j = pl.program_id(1)
    mods = mods_ref[0, 0]

    @pl.when(j == 0)
    def _():
        x = x_ref[...]
        ms = jnp.mean(x * x, axis=-1, keepdims=True)
        h = x * lax.rsqrt(ms + EPS) * g2_ref[0]
        h2_ref[...] = (h * (1.0 + mods[4:5]) + mods[3:4]).astype(BF16)
        acc_ref[...] = jnp.zeros_like(acc_ref)

    hid = jnp.square(jnp.maximum(_dot(h2_ref[...], w1_ref[0]), 0.0))
    acc_ref[...] += _dot(hid.astype(BF16), w2_ref[0])

    @pl.when(j == pl.num_programs(1) - 1)
    def _():
        y = x_ref[...] + mods[5:6] * acc_ref[...]
        if final:
            ms = jnp.mean(y * y, axis=-1, keepdims=True)
            y = y * lax.rsqrt(ms + EPS) * fg_ref[...]
        o_ref[...] = y


def _mlp_call(x, mods, layer, cond_row0, p, final):
    n_tiles = x.shape[0] // TILE
    return pl.pallas_call(
        functools.partial(_mlp_kernel, final=final),
        grid=(n_tiles, D_FF // FF_CHUNK),
        in_specs=[
            pl.BlockSpec((TILE, D_MODEL), lambda g, j: (g, 0)),
            pl.BlockSpec((1, 1, N_MOD, D_MODEL), lambda g, j: (layer, cond_row0(g), 0, 0)),
            pl.BlockSpec((1, 1, D_MODEL), lambda g, j: (layer, 0, 0)),
            pl.BlockSpec((1, D_MODEL, FF_CHUNK), lambda g, j: (layer, 0, j)),
            pl.BlockSpec((1, FF_CHUNK, D_MODEL), lambda g, j: (layer, j, 0)),
            pl.BlockSpec((1, D_MODEL), lambda g, j: (0, 0)),
        ],
        out_specs=pl.BlockSpec((TILE, D_MODEL), lambda g, j: (g, 0)),
        out_shape=jax.ShapeDtypeStruct(x.shape, F32),
        scratch_shapes=[pltpu.VMEM((TILE, D_MODEL), BF16), pltpu.VMEM((TILE, D_MODEL), F32)],
        compiler_params=pltpu.CompilerParams(dimension_semantics=("arbitrary", "arbitrary"),
                                             vmem_limit_bytes=VMEM_LIMIT),
        name="mlp",
    )(x, mods, p['norm2_g'], p['mlp_w1'], p['mlp_w2'], p['final_g'])


def _block_diag(w, per_block):
    *lead, n_heads, d, _ = w.shape
    w = w.reshape(*lead, n_heads // per_block, per_block, d, d)
    eye = jnp.eye(per_block, dtype=w.dtype)
    bd = jnp.einsum('...hij,hk->...hikj', w, eye)
    return bd.reshape(*lead, n_heads // per_block, per_block * d, per_block * d)


def _dft_tables(n, scale):
    k = np.arange(n)
    ang = 2.0 * np.pi * ((k[:, None] * k[None, :]) % n) / n
    return np.cos(ang) * scale, np.sin(ang) * scale


def _seq_dft(seq_len):
    c, s = _dft_tables(seq_len, 1.0 / math.sqrt(seq_len))
    return jnp.asarray(np.stack([c, -s]), F32).astype(BF16)


def _channel_dft():
    c, s = _dft_tables(FOURIER_HEAD_DIM, 1.0 / math.sqrt(FOURIER_HEAD_DIM))
    eye = np.eye(N_FOURIER_HEADS)
    return jnp.asarray(np.concatenate([np.kron(eye, c), np.kron(eye, s)], axis=1), F32).astype(BF16)


def kernel(x_prompt, x_sample, state_rglru, c, c_ctx, norm1_g, norm2_g, final_g, w_mod, b_mod, w_in, conv_w, conv_b, lru_wr, lru_br, lru_wi, lru_bi, lru_lambda, pool_w, pool_scale, fourier_w, w_out, mlp_w1, mlp_w2):
    batch, seq, _ = x_prompt.shape
    dec_batch, dec_seq, _ = x_sample.shape
    assert TILE % seq == 0 and (batch * seq) % TILE == 0 and dec_seq == TILE

    w_gate = jnp.stack([_block_diag(lru_wr, 4), _block_diag(lru_wi, 4)], axis=2)
    p = {
        'norm1_g': norm1_g.reshape(DEPTH, 1, D_MODEL),
        'norm2_g': norm2_g.reshape(DEPTH, 1, D_MODEL),
        'final_g': final_g.reshape(1, D_MODEL),
        'w_in': w_in.astype(BF16),
        'conv_w': conv_w,
        'conv_b': conv_b.reshape(DEPTH, 1, D_RNN),
        'w_gate': w_gate.reshape(DEPTH, 8, 256, 256).astype(BF16),
        'b_gate': jnp.stack([lru_br, lru_bi], axis=2).reshape(DEPTH, 4, D_RNN),
        'lam': lru_lambda,
        'pool_w': _block_diag(pool_w, 4).reshape(DEPTH, D_POOL, D_POOL).astype(BF16),
        'pool_scale': pool_scale.reshape(DEPTH, 1, D_POOL),
        'cdft': _channel_dft(),
        'fourier_w': fourier_w.astype(BF16),
        'w_out': w_out.astype(BF16),
        'mlp_w1': mlp_w1.astype(BF16),
        'mlp_w2': mlp_w2.astype(BF16),
    }

    cond = jnp.concatenate([c_ctx[None], c, jnp.zeros((MODS_ROWS - 1 - dec_batch, D_MODEL), F32)], axis=0)
    mods = _mods_call(cond, w_mod, b_mod).reshape(DEPTH, MODS_ROWS, N_MOD, D_MODEL)

    ctx_row = lambda g: 0
    lat_row = lambda g: 1 + g
    h0_ctx = jnp.zeros((1, 1, 2, D_RNN), F32)
    dft_ctx = _seq_dft(seq)
    dft_lat = _seq_dft(dec_seq)

    xc = x_prompt.reshape(batch * seq, D_MODEL)
    xs = x_sample.reshape(dec_batch * dec_seq, D_MODEL)
    states = []
    for l in range(DEPTH):
        final = l == DEPTH - 1
        xc, st = _mixer_call(xc, mods, l, ctx_row, p, h0_ctx, lambda g: (0, 0, 0, 0), dft_ctx, seq, True)
        states.append(st.reshape(batch, 2, D_RNN))
        xc = _mlp_call(xc, mods, l, ctx_row, p, final)
        (xs,) = _mixer_call(xs, mods, l, lat_row, p, state_rglru, lambda g, l=l: (g, l, 0, 0), dft_lat,
                            dec_seq, False)
        xs = _mlp_call(xs, mods, l, lat_row, p, final)

    y_prompt = xc.reshape(batch, seq, D_MODEL)
    y_sample = xs.reshape(dec_batch, dec_seq, D_MODEL)
    new_state = jnp.stack(states, axis=1)
    return (y_prompt, y_sample, new_state)
```

```python
import functools
import math

import numpy as np
import jax
import jax.numpy as jnp
from jax import lax
from jax.experimental import pallas as pl
from jax.experimental.pallas import tpu as pltpu

D_MODEL = 1024
DEPTH = 4
GRID_W = 64
D_RNN = 512
N_LRU_HEADS = 8
LRU_HEAD_DIM = 64
LRU_C = 8.0
CONV_WIDTH = 4
CONV_LEFT = 2
D_POOL = 256
D_FOURIER = 256
N_FOURIER_HEADS = 4
FOURIER_HEAD_DIM = 64
D_IN = 2 * D_RNN + D_POOL + D_FOURIER
D_FF = 4 * D_MODEL
N_MOD = 6
EPS = 1e-6

LANES = 128
SUBLANES = 8
TILE = 1024
CHUNK = TILE // SUBLANES
N_SLAB = D_RNN // LANES
FF_CHUNK = 1024
MODS_ROWS = 8
VMEM_LIMIT = 60 * 1024 * 1024

F32 = jnp.float32
BF16 = jnp.bfloat16


def _dot(a, b):
    return jnp.dot(a, b, preferred_element_type=F32)


def _row_iota(shape):
    return lax.broadcasted_iota(jnp.int32, shape, 0)


def _shift_rows(x, d):
    n = x.shape[0]
    return pltpu.roll(x, (-d) % n, axis=0)


def _mods_kernel(cond_ref, w_ref, b_ref, o_ref):
    s = cond_ref[...]
    s = s * jax.nn.sigmoid(s)
    o_ref[0] = _dot(s.astype(BF16), w_ref[0].astype(BF16)) + b_ref[0]


def _mods_call(cond, w_mod, b_mod):
    n_tiles = (N_MOD * D_MODEL) // D_MODEL
    return pl.pallas_call(
        _mods_kernel,
        grid=(DEPTH, n_tiles),
        in_specs=[
            pl.BlockSpec((MODS_ROWS, D_MODEL), lambda l, j: (0, 0)),
            pl.BlockSpec((1, D_MODEL, D_MODEL), lambda l, j: (l, 0, j)),
            pl.BlockSpec((1, 1, D_MODEL), lambda l, j: (l, 0, j)),
        ],
        out_specs=pl.BlockSpec((1, MODS_ROWS, D_MODEL), lambda l, j: (l, 0, j)),
        out_shape=jax.ShapeDtypeStruct((DEPTH, MODS_ROWS, N_MOD * D_MODEL), F32),
        compiler_params=pltpu.CompilerParams(dimension_semantics=("arbitrary", "arbitrary")),
        name="mods",
    )(cond, w_mod, b_mod.reshape(DEPTH, 1, N_MOD * D_MODEL))


def _window_sums(x, pos, length, unit):
    fwd = x
    bwd = jnp.where(pos >= 1, _shift_rows(x, -unit), 0.0)
    outs = [fwd + bwd]
    for k in (1, 2, 4):
        fwd = fwd + jnp.where(pos + k < length, _shift_rows(fwd, k * unit), 0.0)
        bwd = bwd + jnp.where(pos - k >= 0, _shift_rows(bwd, -k * unit), 0.0)
        outs.append(fwd + bwd)
    return outs


def _pool_axis(x, pos, length, unit):
    s1, s2, s4, s8 = _window_sums(x, pos, length, unit)
    lane = lax.broadcasted_iota(jnp.int32, x.shape, 1)
    sel = jnp.where(lane < 64, s1, jnp.where(lane < 128, s2, jnp.where(lane < 192, s4, s8)))
    half = jnp.where(lane < 64, 1, jnp.where(lane < 128, 2, jnp.where(lane < 192, 4, 8)))
    cnt = jnp.minimum(pos + half, length) - jnp.maximum(pos - half, 0)
    return sel, cnt.astype(F32)


def _mixer_kernel(x_ref, mods_ref, g1_ref, win_ref, cw_ref, cb_ref, wg_ref, bg_ref, lam_ref, h0_ref,
                  pw_ref, ps_ref, cdft_ref, fw_ref, dft_ref, wout_ref,
                  xo_ref, st_ref,
                  u_ref, xc_ref, gate_ref, scan_ref, cat_ref, *, seq_len):
    n_seq = TILE // seq_len
    chunks_per_seq = seq_len // CHUNK
    mods = mods_ref[0, 0]

    x = x_ref[...]
    ms = jnp.mean(x * x, axis=-1, keepdims=True)
    h = x * lax.rsqrt(ms + EPS) * g1_ref[0]
    h = h * (1.0 + mods[1:2]) + mods[0:1]
    u_ref[...] = _dot(h.astype(BF16), win_ref[0])

    xr = u_ref[:, 0:D_RNN]
    pos = _row_iota((TILE, D_RNN)) & (seq_len - 1)
    acc = cb_ref[0] + xr * cw_ref[0, CONV_LEFT:CONV_LEFT + 1, :]
    for k in range(CONV_WIDTH):
        d = k - CONV_LEFT
        if d == 0:
            continue
        valid = (pos + d >= 0) & (pos + d < seq_len)
        acc = acc + jnp.where(valid, _shift_rows(xr, d), 0.0) * cw_ref[0, k:k + 1, :]
    xc_ref[...] = acc

    xcb = xc_ref[...].astype(BF16)
    for d in range(2):
        for gate in range(2):
            for blk in range(2):
                pre = _dot(xcb[:, blk * 256:(blk + 1) * 256], wg_ref[0, d * 4 + gate * 2 + blk])
                gate_ref[gate * N_SLAB + 2 * blk] = pre[:, 0:LANES]
                gate_ref[gate * N_SLAB + 2 * blk + 1] = pre[:, LANES:2 * LANES]
        lam = lam_ref[0, d:d + 1, :]
        softplus_neg_lam = jnp.maximum(-lam, 0.0) + jnp.log1p(jnp.exp(-jnp.abs(lam)))
        half_rate = (-0.5 * LRU_C) * softplus_neg_lam
        b_r = bg_ref[0, 2 * d:2 * d + 1, :]
        b_i = bg_ref[0, 2 * d + 1:2 * d + 2, :]

        def gate_chunk(c, carry, d=d, half_rate=half_rate, b_r=b_r, b_i=b_i):
            r0 = pl.multiple_of(c * CHUNK, CHUNK)
            for s in range(N_SLAB):
                lanes = slice(s * LANES, (s + 1) * LANES)
                r = 0.5 + 0.5 * jnp.tanh(0.5 * (gate_ref[s, pl.ds(r0, CHUNK), :] + b_r[:, lanes]))
                i = 0.5 + 0.5 * jnp.tanh(0.5 * (gate_ref[N_SLAB + s, pl.ds(r0, CHUNK), :] + b_i[:, lanes]))
                th = jnp.tanh(r * half_rate[:, lanes])
                q = 1.0 / (1.0 - th)
                a = (1.0 + th) * q
                b = (2.0 * jnp.sqrt(-th) * q) * (i * xc_ref[pl.ds(r0, CHUNK), lanes])
                scan_ref[2 * d, s, pl.ds(c, CHUNK, stride=SUBLANES), :] = a
                scan_ref[2 * d + 1, s, pl.ds(c, CHUNK, stride=SUBLANES), :] = b
            return carry

        lax.fori_loop(0, SUBLANES, gate_chunk, 0)

    def load(arr, s, t):
        return scan_ref[arr, s, pl.ds(pl.multiple_of(t * SUBLANES, SUBLANES), SUBLANES), :]

    def pass1(t, carry):
        hf, pf, hb, pb = carry
        tb = CHUNK - 1 - t
        nhf, npf, nhb, npb = [], [], [], []
        for s in range(N_SLAB):
            af = load(0, s, t)
            ab = load(2, s, tb)
            nhf.append(af * hf[s] + load(1, s, t))
            npf.append(af * pf[s])
            nhb.append(ab * hb[s] + load(3, s, tb))
            npb.append(ab * pb[s])
        return tuple(nhf), tuple(npf), tuple(nhb), tuple(npb)

    zeros = tuple(jnp.zeros((SUBLANES, LANES), F32) for _ in range(N_SLAB))
    ones = tuple(jnp.ones((SUBLANES, LANES), F32) for _ in range(N_SLAB))
    ef, pf, eb, pb = lax.fori_loop(0, CHUNK, pass1, (zeros, ones, zeros, ones), unroll=4)

    row = _row_iota((SUBLANES, LANES))
    seq_first = (row & (chunks_per_seq - 1)) == 0
    seq_last = (row & (chunks_per_seq - 1)) == chunks_per_seq - 1
    init_f, init_b = [], []
    for s in range(N_SLAB):
        h0f = jnp.broadcast_to(h0_ref[0, 0, 0:1, s * LANES:(s + 1) * LANES], (SUBLANES, LANES))
        h0b = jnp.broadcast_to(h0_ref[0, 0, 1:2, s * LANES:(s + 1) * LANES], (SUBLANES, LANES))
        cf, cb = h0f, h0b
        for _ in range(chunks_per_seq - 1):
            cf = jnp.where(seq_first, h0f, pltpu.roll(ef[s] + pf[s] * cf, 1, axis=0))
            cb = jnp.where(seq_last, h0b, pltpu.roll(eb[s] + pb[s] * cb, SUBLANES - 1, axis=0))
        init_f.append(cf)
        init_b.append(cb)

    def pass2(t, carry):
        hf, hb = carry
        tb = CHUNK - 1 - t
        nhf, nhb = [], []
        for s in range(N_SLAB):
            vf = load(0, s, t) * hf[s] + load(1, s, t)
            vb = load(2, s, tb) * hb[s] + load(3, s, tb)
            gate_ref[s, pl.ds(pl.multiple_of(t * SUBLANES, SUBLANES), SUBLANES), :] = vf
            gate_ref[N_SLAB + s, pl.ds(pl.multiple_of(tb * SUBLANES, SUBLANES), SUBLANES), :] = vb
            nhf.append(vf)
            nhb.append(vb)
        return tuple(nhf), tuple(nhb)

    lax.fori_loop(0, CHUNK, pass2, (tuple(init_f), tuple(init_b)), unroll=4)

    if st_ref is not None:
        even = (_row_iota((SUBLANES, LANES)) & 1) == 0
        for s in range(N_SLAB):
            last_f = gate_ref[s, pl.ds((CHUNK - 1) * SUBLANES, SUBLANES), :]
            first_b = gate_ref[N_SLAB + s, pl.ds(0, SUBLANES), :]
            st_ref[0, :, s * LANES:(s + 1) * LANES] = jnp.where(
                even, pltpu.roll(last_f, SUBLANES - 1, axis=0), pltpu.roll(first_b, 1, axis=0))

    def out_chunk(c, carry):
        r0 = pl.multiple_of(c * CHUNK, CHUNK)
        for s in range(N_SLAB):
            hsum = (gate_ref[s, pl.ds(c, CHUNK, stride=SUBLANES), :]
                    + gate_ref[N_SLAB + s, pl.ds(c, CHUNK, stride=SUBLANES), :])
            xg = u_ref[pl.ds(r0, CHUNK), D_RNN + s * LANES:D_RNN + (s + 1) * LANES]
            cat_ref[pl.ds(r0, CHUNK), s * LANES:(s + 1) * LANES] = (hsum * jax.nn.gelu(xg)).astype(BF16)
        return carry

    lax.fori_loop(0, SUBLANES, out_chunk, 0)

    xp = u_ref[:, 2 * D_RNN:2 * D_RNN + D_POOL]
    tok = _row_iota((TILE, D_POOL))
    if seq_len == TILE:
        n_rows = TILE // GRID_W
        col_sum, col_cnt = _pool_axis(xp, tok & (GRID_W - 1), GRID_W, 1)
        win_sum, row_cnt = _pool_axis(col_sum, lax.shift_right_logical(tok, GRID_W.bit_length() - 1),
                                      n_rows, GRID_W)
        cnt = row_cnt * col_cnt
    else:
        win_sum, cnt = _pool_axis(xp, tok & (seq_len - 1), seq_len, 1)
    pooled = win_sum / cnt - xp
    cat_ref[:, D_RNN:D_RNN + D_POOL] = (_dot(pooled.astype(BF16), pw_ref[0]) * ps_ref[0]).astype(BF16)

    xf = u_ref[:, 2 * D_RNN + D_POOL:D_IN].astype(BF16)
    y1 = _dot(xf, cdft_ref[...]).astype(BF16)
    for q in range(n_seq):
        rows = slice(q * seq_len, (q + 1) * seq_len)
        four = _dot(dft_ref[0], y1[rows, 0:D_FOURIER]) + _dot(dft_ref[1], y1[rows, D_FOURIER:])
        cat_ref[rows, D_RNN + D_POOL:] = _dot(four.astype(BF16), fw_ref[0]).astype(BF16)

    xo_ref[...] = x_ref[...] + mods[2:3] * _dot(cat_ref[...], wout_ref[0])


def _mixer_call(x, mods, layer, cond_row0, p, h0, h0_index, dft, seq_len, with_state):
    n_tiles = x.shape[0] // TILE
    n_seq = TILE // seq_len
    lyr = lambda *shape: pl.BlockSpec((1,) + shape, lambda g: (layer,) + (0,) * len(shape))
    in_specs = [
        pl.BlockSpec((TILE, D_MODEL), lambda g: (g, 0)),
        pl.BlockSpec((1, 1, N_MOD, D_MODEL), lambda g: (layer, cond_row0(g), 0, 0)),
        lyr(1, D_MODEL),
        lyr(D_MODEL, D_IN),
        lyr(CONV_WIDTH, D_RNN),
        lyr(1, D_RNN),
        lyr(8, 256, 256),
        lyr(4, D_RNN),
        lyr(2, D_RNN),
        pl.BlockSpec((1, 1, 2, D_RNN), h0_index),
        lyr(D_POOL, D_POOL),
        lyr(1, D_POOL),
        pl.BlockSpec((D_FOURIER, 2 * D_FOURIER), lambda g: (0, 0)),
        lyr(D_FOURIER, D_FOURIER),
        pl.BlockSpec((2, seq_len, seq_len), lambda g: (0, 0, 0)),
        lyr(D_MODEL, D_MODEL),
    ]
    out_specs = [pl.BlockSpec((TILE, D_MODEL), lambda g: (g, 0))]
    out_shape = [jax.ShapeDtypeStruct(x.shape, F32)]
    if with_state:
        out_specs.append(pl.BlockSpec((1, 2 * n_seq, D_RNN), lambda g: (g, 0, 0)))
        out_shape.append(jax.ShapeDtypeStruct((n_tiles, 2 * n_seq, D_RNN), F32))
        body = functools.partial(_mixer_kernel, seq_len=seq_len)
    else:
        def body(*refs):
            _mixer_kernel(*refs[:17], None, *refs[17:], seq_len=seq_len)
    return pl.pallas_call(
        body,
        grid=(n_tiles,),
        in_specs=in_specs,
        out_specs=out_specs,
        out_shape=out_shape,
        scratch_shapes=[
            pltpu.VMEM((TILE, D_IN), F32),
            pltpu.VMEM((TILE, D_RNN), F32),
            pltpu.VMEM((2 * N_SLAB, TILE, LANES), F32),
            pltpu.VMEM((4, N_SLAB, TILE, LANES), F32),
            pltpu.VMEM((TILE, D_MODEL), BF16),
        ],
        compiler_params=pltpu.CompilerParams(dimension_semantics=("arbitrary",),
                                             vmem_limit_bytes=VMEM_LIMIT),
        name="mixer_ctx" if with_state else "mixer_lat",
    )(x, mods, p['norm1_g'], p['w_in'], p['conv_w'], p['conv_b'], p['w_gate'], p['b_gate'], p['lam'], h0,
      p['pool_w'], p['pool_scale'], p['cdft'], p['fourier_w'], dft, p['w_out'])


def _mlp_kernel(x_ref, mods_ref, g2_ref, w1_ref, w2_ref, fg_ref, o_ref, h2_ref, acc_ref, *, final):
    j = pl.program_id(1)
    mods = mods_ref[0, 0]

    @pl.when(j == 0)
    def _():
        x = x_ref[...]
        ms = jnp.mean(x * x, axis=-1, keepdims=True)
        h = x * lax.rsqrt(ms + EPS) * g2_ref[0]
        h2_ref[...] = (h * (1.0 + mods[4:5]) + mods[3:4]).astype(BF16)
        acc_ref[...] = jnp.zeros_like(acc_ref)

    hid = jnp.square(jnp.maximum(_dot(h2_ref[...], w1_ref[0]), 0.0))
    acc_ref[...] += _dot(hid.astype(BF16), w2_ref[0])

    @pl.when(j == pl.num_programs(1) - 1)
    def _():
        y = x_ref[...] + mods[5:6] * acc_ref[...]
        if final:
            ms = jnp.mean(y * y, axis=-1, keepdims=True)
            y = y * lax.rsqrt(ms + EPS) * fg_ref[...]
        o_ref[...] = y


def _mlp_call(x, mods, layer, cond_row0, p, final):
    n_tiles = x.shape[0] // TILE
    return pl.pallas_call(
        functools.partial(_mlp_kernel, final=final),
        grid=(n_tiles, D_FF // FF_CHUNK),
        in_specs=[
            pl.BlockSpec((TILE, D_MODEL), lambda g, j: (g, 0)),
            pl.BlockSpec((1, 1, N_MOD, D_MODEL), lambda g, j: (layer, cond_row0(g), 0, 0)),
            pl.BlockSpec((1, 1, D_MODEL), lambda g, j: (layer, 0, 0)),
            pl.BlockSpec((1, D_MODEL, FF_CHUNK), lambda g, j: (layer, 0, j)),
            pl.BlockSpec((1, FF_CHUNK, D_MODEL), lambda g, j: (layer, j, 0)),
            pl.BlockSpec((1, D_MODEL), lambda g, j: (0, 0)),
        ],
        out_specs=pl.BlockSpec((TILE, D_MODEL), lambda g, j: (g, 0)),
        out_shape=jax.ShapeDtypeStruct(x.shape, F32),
        scratch_shapes=[pltpu.VMEM((TILE, D_MODEL), BF16), pltpu.VMEM((TILE, D_MODEL), F32)],
        compiler_params=pltpu.CompilerParams(dimension_semantics=("arbitrary", "arbitrary"),
                                             vmem_limit_bytes=VMEM_LIMIT),
        name="mlp",
    )(x, mods, p['norm2_g'], p['mlp_w1'], p['mlp_w2'], p['final_g'])


def _block_diag(w, per_block):
    *lead, n_heads, d, _ = w.shape
    w = w.reshape(*lead, n_heads // per_block, per_block, d, d)
    eye = jnp.eye(per_block, dtype=w.dtype)
    bd = jnp.einsum('...hij,hk->...hikj', w, eye)
    return bd.reshape(*lead, n_heads // per_block, per_block * d, per_block * d)


def _dft_tables(n, scale):
    k = np.arange(n)
    ang = 2.0 * np.pi * ((k[:, None] * k[None, :]) % n) / n
    return np.cos(ang) * scale, np.sin(ang) * scale


def _seq_dft(seq_len):
    c, s = _dft_tables(seq_len, 1.0 / math.sqrt(seq_len))
    return jnp.asarray(np.stack([c, -s]), F32).astype(BF16)


def _channel_dft():
    c, s = _dft_tables(FOURIER_HEAD_DIM, 1.0 / math.sqrt(FOURIER_HEAD_DIM))
    eye = np.eye(N_FOURIER_HEADS)
    return jnp.asarray(np.concatenate([np.kron(eye, c), np.kron(eye, s)], axis=1), F32).astype(BF16)


def kernel(x_prompt, x_sample, state_rglru, c, c_ctx, norm1_g, norm2_g, final_g, w_mod, b_mod, w_in, conv_w, conv_b, lru_wr, lru_br, lru_wi, lru_bi, lru_lambda, pool_w, pool_scale, fourier_w, w_out, mlp_w1, mlp_w2):
    batch, seq, _ = x_prompt.shape
    dec_batch, dec_seq, _ = x_sample.shape
    assert TILE % seq == 0 and (batch * seq) % TILE == 0 and dec_seq == TILE

    w_gate = jnp.stack([_block_diag(lru_wr, 4), _block_diag(lru_wi, 4)], axis=2)
    p = {
        'norm1_g': norm1_g.reshape(DEPTH, 1, D_MODEL),
        'norm2_g': norm2_g.reshape(DEPTH, 1, D_MODEL),
        'final_g': final_g.reshape(1, D_MODEL),
        'w_in': w_in.astype(BF16),
        'conv_w': conv_w,
        'conv_b': conv_b.reshape(DEPTH, 1, D_RNN),
        'w_gate': w_gate.reshape(DEPTH, 8, 256, 256).astype(BF16),
        'b_gate': jnp.stack([lru_br, lru_bi], axis=2).reshape(DEPTH, 4, D_RNN),
        'lam': lru_lambda,
        'pool_w': _block_diag(pool_w, 4).reshape(DEPTH, D_POOL, D_POOL).astype(BF16),
        'pool_scale': pool_scale.reshape(DEPTH, 1, D_POOL),
        'cdft': _channel_dft(),
        'fourier_w': fourier_w.astype(BF16),
        'w_out': w_out.astype(BF16),
        'mlp_w1': mlp_w1.astype(BF16),
        'mlp_w2': mlp_w2.astype(BF16),
    }

    cond = jnp.concatenate([c_ctx[None], c, jnp.zeros((MODS_ROWS - 1 - dec_batch, D_MODEL), F32)], axis=0)
    mods = _mods_call(cond, w_mod, b_mod).reshape(DEPTH, MODS_ROWS, N_MOD, D_MODEL)

    ctx_row = lambda g: 0
    lat_row = lambda g: 1 + g
    h0_ctx = jnp.zeros((1, 1, 2, D_RNN), F32)
    dft_ctx = _seq_dft(seq)
    dft_lat = _seq_dft(dec_seq)

    xc = x_prompt.reshape(batch * seq, D_MODEL)
    xs = x_sample.reshape(dec_batch * dec_seq, D_MODEL)
    states = []
    for l in range(DEPTH):
        final = l == DEPTH - 1
        xc, st = _mixer_call(xc, mods, l, ctx_row, p, h0_ctx, lambda g: (0, 0, 0, 0), dft_ctx, seq, True)
        states.append(st.reshape(batch, 2, D_RNN))
        xc = _mlp_call(xc, mods, l, ctx_row, p, final)
        (xs,) = _mixer_call(xs, mods, l, lat_row, p, state_rglru, lambda g, l=l: (g, l, 0, 0), dft_lat,
                            dec_seq, False)
        xs = _mlp_call(xs, mods, l, lat_row, p, final)

    y_prompt = xc.reshape(batch, seq, D_MODEL)
    y_sample = xs.reshape(dec_batch, dec_seq, D_MODEL)
    new_state = jnp.stack(states, axis=1)
    return (y_prompt, y_sample, new_state)
```

```python
import functools
import math

import numpy as np
import jax
import jax.numpy as jnp
from jax import lax
from jax.experimental import pallas as pl
from jax.experimental.pallas import tpu as pltpu

D_MODEL = 1024
DEPTH = 4
GRID_W = 64
D_RNN = 512
N_LRU_HEADS = 8
LRU_HEAD_DIM = 64
LRU_C = 8.0
CONV_WIDTH = 4
CONV_LEFT = 2
D_POOL = 256
D_FOURIER = 256
N_FOURIER_HEADS = 4
FOURIER_HEAD_DIM = 64
D_IN = 2 * D_RNN + D_POOL + D_FOURIER
D_FF = 4 * D_MODEL
N_MOD = 6
EPS = 1e-6

LANES = 128
SUBLANES = 8
TILE = 1024
CHUNK = TILE // SUBLANES
N_SLAB = D_RNN // LANES
FF_CHUNK = 1024
MODS_ROWS = 8
VMEM_LIMIT = 60 * 1024 * 1024

F32 = jnp.float32
BF16 = jnp.bfloat16


def _dot(a, b):
    return jnp.dot(a, b, preferred_element_type=F32)


def _row_iota(shape):
    return lax.broadcasted_iota(jnp.int32, shape, 0)


def _shift_rows(x, d):
    n = x.shape[0]
    return pltpu.roll(x, (-d) % n, axis=0)


def _mods_kernel(cond_ref, w_ref, b_ref, o_ref):
    s = cond_ref[...]
    s = s * jax.nn.sigmoid(s)
    o_ref[0] = _dot(s.astype(BF16), w_ref[0].astype(BF16)) + b_ref[0]


def _mods_call(cond, w_mod, b_mod):
    n_tiles = (N_MOD * D_MODEL) // D_MODEL
    return pl.pallas_call(
        _mods_kernel,
        grid=(DEPTH, n_tiles),
        in_specs=[
            pl.BlockSpec((MODS_ROWS, D_MODEL), lambda l, j: (0, 0)),
            pl.BlockSpec((1, D_MODEL, D_MODEL), lambda l, j: (l, 0, j)),
            pl.BlockSpec((1, 1, D_MODEL), lambda l, j: (l, 0, j)),
        ],
        out_specs=pl.BlockSpec((1, MODS_ROWS, D_MODEL), lambda l, j: (l, 0, j)),
        out_shape=jax.ShapeDtypeStruct((DEPTH, MODS_ROWS, N_MOD * D_MODEL), F32),
        compiler_params=pltpu.CompilerParams(dimension_semantics=("arbitrary", "arbitrary")),
        name="mods",
    )(cond, w_mod, b_mod.reshape(DEPTH, 1, N_MOD * D_MODEL))


def _window_sums(x, pos, length, unit):
    fwd = x
    bwd = jnp.where(pos >= 1, _shift_rows(x, -unit), 0.0)
    outs = [fwd + bwd]
    for k in (1, 2, 4):
        fwd = fwd + jnp.where(pos + k < length, _shift_rows(fwd, k * unit), 0.0)
        bwd = bwd + jnp.where(pos - k >= 0, _shift_rows(bwd, -k * unit), 0.0)
        outs.append(fwd + bwd)
    return outs


def _pool_axis(x, pos, length, unit):
    s1, s2, s4, s8 = _window_sums(x, pos, length, unit)
    lane = lax.broadcasted_iota(jnp.int32, x.shape, 1)
    sel = jnp.where(lane < 64, s1, jnp.where(lane < 128, s2, jnp.where(lane < 192, s4, s8)))
    half = jnp.where(lane < 64, 1, jnp.where(lane < 128, 2, jnp.where(lane < 192, 4, 8)))
    cnt = jnp.minimum(pos + half, length) - jnp.maximum(pos - half, 0)
    return sel, cnt.astype(F32)


def _mixer_kernel(x_ref, mods_ref, g1_ref, win_ref, cw_ref, cb_ref, wg_ref, bg_ref, lam_ref, h0_ref,
                  pw_ref, ps_ref, cdft_ref, fw_ref, dft_ref, wout_ref,
                  xo_ref, st_ref,
                  u_ref, xc_ref, gate_ref, scan_ref, cat_ref, *, seq_len):
    n_seq = TILE // seq_len
    chunks_per_seq = seq_len // CHUNK
    mods = mods_ref[0, 0]

    x = x_ref[...]
    ms = jnp.mean(x * x, axis=-1, keepdims=True)
    h = x * lax.rsqrt(ms + EPS) * g1_ref[0]
    h = h * (1.0 + mods[1:2]) + mods[0:1]
    u_ref[...] = _dot(h.astype(BF16), win_ref[0])

    xr = u_ref[:, 0:D_RNN]
    pos = _row_iota((TILE, D_RNN)) & (seq_len - 1)
    acc = cb_ref[0] + xr * cw_ref[0, CONV_LEFT:CONV_LEFT + 1, :]
    for k in range(CONV_WIDTH):
        d = k - CONV_LEFT
        if d == 0:
            continue
        valid = (pos + d >= 0) & (pos + d < seq_len)
        acc = acc + jnp.where(valid, _shift_rows(xr, d), 0.0) * cw_ref[0, k:k + 1, :]
    xc_ref[...] = acc

    xcb = xc_ref[...].astype(BF16)
    for d in range(2):
        for gate in range(2):
            for blk in range(2):
                pre = _dot(xcb[:, blk * 256:(blk + 1) * 256], wg_ref[0, d * 4 + gate * 2 + blk])
                gate_ref[gate * N_SLAB + 2 * blk] = pre[:, 0:LANES]
                gate_ref[gate * N_SLAB + 2 * blk + 1] = pre[:, LANES:2 * LANES]
        lam = lam_ref[0, d:d + 1, :]
        softplus_neg_lam = jnp.maximum(-lam, 0.0) + jnp.log1p(jnp.exp(-jnp.abs(lam)))
        half_rate = (-0.5 * LRU_C) * softplus_neg_lam
        b_r = bg_ref[0, 2 * d:2 * d + 1, :]
        b_i = bg_ref[0, 2 * d + 1:2 * d + 2, :]

        def gate_chunk(c, carry, d=d, half_rate=half_rate, b_r=b_r, b_i=b_i):
            r0 = pl.multiple_of(c * CHUNK, CHUNK)
            for s in range(N_SLAB):
                lanes = slice(s * LANES, (s + 1) * LANES)
                r = 0.5 + 0.5 * jnp.tanh(0.5 * (gate_ref[s, pl.ds(r0, CHUNK), :] + b_r[:, lanes]))
                i = 0.5 + 0.5 * jnp.tanh(0.5 * (gate_ref[N_SLAB + s, pl.ds(r0, CHUNK), :] + b_i[:, lanes]))
                th = jnp.tanh(r * half_rate[:, lanes])
                q = 1.0 / (1.0 - th)
                a = (1.0 + th) * q
                b = (2.0 * jnp.sqrt(-th) * q) * (i * xc_ref[pl.ds(r0, CHUNK), lanes])
                scan_ref[2 * d, s, pl.ds(c, CHUNK, stride=SUBLANES), :] = a
                scan_ref[2 * d + 1, s, pl.ds(c, CHUNK, stride=SUBLANES), :] = b
            return carry

        lax.fori_loop(0, SUBLANES, gate_chunk, 0)

    def load(arr, s, t):
        return scan_ref[arr, s, pl.ds(pl.multiple_of(t * SUBLANES, SUBLANES), SUBLANES), :]

    def pass1(t, carry):
        hf, pf, hb, pb = carry
        tb = CHUNK - 1 - t
        nhf, npf, nhb, npb = [], [], [], []
        for s in range(N_SLAB):
            af = load(0, s, t)
            ab = load(2, s, tb)
            nhf.append(af * hf[s] + load(1, s, t))
            npf.append(af * pf[s])
            nhb.append(ab * hb[s] + load(3, s, tb))
            npb.append(ab * pb[s])
        return tuple(nhf), tuple(npf), tuple(nhb), tuple(npb)

    zeros = tuple(jnp.zeros((SUBLANES, LANES), F32) for _ in range(N_SLAB))
    ones = tuple(jnp.ones((SUBLANES, LANES), F32) for _ in range(N_SLAB))
    ef, pf, eb, pb = lax.fori_loop(0, CHUNK, pass1, (zeros, ones, zeros, ones), unroll=4)

    row = _row_iota((SUBLANES, LANES))
    seq_first = (row & (chunks_per_seq - 1)) == 0
    seq_last = (row & (chunks_per_seq - 1)) == chunks_per_seq - 1
    init_f, init_b = [], []
    for s in range(N_SLAB):
        h0f = jnp.broadcast_to(h0_ref[0, 0, 0:1, s * LANES:(s + 1) * LANES], (SUBLANES, LANES))
        h0b = jnp.broadcast_to(h0_ref[0, 0, 1:2, s * LANES:(s + 1) * LANES], (SUBLANES, LANES))
        cf, cb = h0f, h0b
        for _ in range(chunks_per_seq - 1):
            cf = jnp.where(seq_first, h0f, pltpu.roll(ef[s] + pf[s] * cf, 1, axis=0))
            cb = jnp.where(seq_last, h0b, pltpu.roll(eb[s] + pb[s] * cb, SUBLANES - 1, axis=0))
        init_f.append(cf)
        init_b.append(cb)

    def pass2(t, carry):
        hf, hb = carry
        tb = CHUNK - 1 - t
        nhf, nhb = [], []
        for s in range(N_SLAB):
            vf = load(0, s, t) * hf[s] + load(1, s, t)
            vb = load(2, s, tb) * hb[s] + load(3, s, tb)
            gate_ref[s, pl.ds(pl.multiple_of(t * SUBLANES, SUBLANES), SUBLANES), :] = vf
            gate_ref[N_SLAB + s, pl.ds(pl.multiple_of(tb * SUBLANES, SUBLANES), SUBLANES), :] = vb
            nhf.append(vf)
            nhb.append(vb)
        return tuple(nhf), tuple(nhb)

    lax.fori_loop(0, CHUNK, pass2, (tuple(init_f), tuple(init_b)), unroll=4)

    if st_ref is not None:
        even = (_row_iota((SUBLANES, LANES)) & 1) == 0
        for s in range(N_SLAB):
            last_f = gate_ref[s, pl.ds((CHUNK - 1) * SUBLANES, SUBLANES), :]
            first_b = gate_ref[N_SLAB + s, pl.ds(0, SUBLANES), :]
            st_ref[0, :, s * LANES:(s + 1) * LANES] = jnp.where(
                even, pltpu.roll(last_f, SUBLANES - 1, axis=0), pltpu.roll(first_b, 1, axis=0))

    def out_chunk(c, carry):
        r0 = pl.multiple_of(c * CHUNK, CHUNK)
        for s in range(N_SLAB):
            hsum = (gate_ref[s, pl.ds(c, CHUNK, stride=SUBLANES), :]
                    + gate_ref[N_SLAB + s, pl.ds(c, CHUNK, stride=SUBLANES), :])
            xg = u_ref[pl.ds(r0, CHUNK), D_RNN + s * LANES:D_RNN + (s + 1) * LANES]
            cat_ref[pl.ds(r0, CHUNK), s * LANES:(s + 1) * LANES] = (hsum * jax.nn.gelu(xg)).astype(BF16)
        return carry

    lax.fori_loop(0, SUBLANES, out_chunk, 0)

    xp = u_ref[:, 2 * D_RNN:2 * D_RNN + D_POOL]
    tok = _row_iota((TILE, D_POOL))
    if seq_len == TILE:
        n_rows = TILE // GRID_W
        col_sum, col_cnt = _pool_axis(xp, tok & (GRID_W - 1), GRID_W, 1)
        win_sum, row_cnt = _pool_axis(col_sum, lax.shift_right_logical(tok, GRID_W.bit_length() - 1),
                                      n_rows, GRID_W)
        cnt = row_cnt * col_cnt
    else:
        win_sum, cnt = _pool_axis(xp, tok & (seq_len - 1), seq_len, 1)
    pooled = win_sum / cnt - xp
    cat_ref[:, D_RNN:D_RNN + D_POOL] = (_dot(pooled.astype(BF16), pw_ref[0]) * ps_ref[0]).astype(BF16)

    xf = u_ref[:, 2 * D_RNN + D_POOL:D_IN].astype(BF16)
    y1 = _dot(xf, cdft_ref[...]).astype(BF16)
    for q in range(n_seq):
        rows = slice(q * seq_len, (q + 1) * seq_len)
        four = _dot(dft_ref[0], y1[rows, 0:D_FOURIER]) + _dot(dft_ref[1], y1[rows, D_FOURIER:])
        cat_ref[rows, D_RNN + D_POOL:] = _dot(four.astype(BF16), fw_ref[0]).astype(BF16)

    xo_ref[...] = x_ref[...] + mods[2:3] * _dot(cat_ref[...], wout_ref[0])


def _mixer_call(x, mods, layer, cond_row0, p, h0, h0_index, dft, seq_len, with_state):
    n_tiles = x.shape[0] // TILE
    n_seq = TILE // seq_len
    lyr = lambda *shape: pl.BlockSpec((1,) + shape, lambda g: (layer,) + (0,) * len(shape))
    in_specs = [
        pl.BlockSpec((TILE, D_MODEL), lambda g: (g, 0)),
        pl.BlockSpec((1, 1, N_MOD, D_MODEL), lambda g: (layer, cond_row0(g), 0, 0)),
        lyr(1, D_MODEL),
        lyr(D_MODEL, D_IN),
        lyr(CONV_WIDTH, D_RNN),
        lyr(1, D_RNN),
        lyr(8, 256, 256),
        lyr(4, D_RNN),
        lyr(2, D_RNN),
        pl.BlockSpec((1, 1, 2, D_RNN), h0_index),
        lyr(D_POOL, D_POOL),
        lyr(1, D_POOL),
        pl.BlockSpec((D_FOURIER, 2 * D_FOURIER), lambda g: (0, 0)),
        lyr(D_FOURIER, D_FOURIER),
        pl.BlockSpec((2, seq_len, seq_len), lambda g: (0, 0, 0)),
        lyr(D_MODEL, D_MODEL),
    ]
    out_specs = [pl.BlockSpec((TILE, D_MODEL), lambda g: (g, 0))]
    out_shape = [jax.ShapeDtypeStruct(x.shape, F32)]
    if with_state:
        out_specs.append(pl.BlockSpec((1, 2 * n_seq, D_RNN), lambda g: (g, 0, 0)))
        out_shape.append(jax.ShapeDtypeStruct((n_tiles, 2 * n_seq, D_RNN), F32))
        body = functools.partial(_mixer_kernel, seq_len=seq_len)
    else:
        def body(*refs):
            _mixer_kernel(*refs[:17], None, *refs[17:], seq_len=seq_len)
    return pl.pallas_call(
        body,
        grid=(n_tiles,),
        in_specs=in_specs,
        out_specs=out_specs,
        out_shape=out_shape,
        scratch_shapes=[
            pltpu.VMEM((TILE, D_IN), F32),
            pltpu.VMEM((TILE, D_RNN), F32),
            pltpu.VMEM((2 * N_SLAB, TILE, LANES), F32),
            pltpu.VMEM((4, N_SLAB, TILE, LANES), F32),
            pltpu.VMEM((TILE, D_MODEL), BF16),
        ],
        compiler_params=pltpu.CompilerParams(dimension_semantics=("arbitrary",),
                                             vmem_limit_bytes=VMEM_LIMIT),
        name="mixer_ctx" if with_state else "mixer_lat",
    )(x, mods, p['norm1_g'], p['w_in'], p['conv_w'], p['conv_b'], p['w_gate'], p['b_gate'], p['lam'], h0,
      p['pool_w'], p['pool_scale'], p['cdft'], p['fourier_w'], dft, p['w_out'])


def _mlp_kernel(x_ref, mods_ref, g2_ref, w1_ref, w2_ref, fg_ref, o_ref, h2_ref, acc_ref, *, final):
    j = pl.program_id(1)
    mods = mods_ref[0, 0]

    @pl.when(j == 0)
    def _():
        x = x_ref[...]
        ms = jnp.mean(x * x, axis=-1, keepdims=True)
        h = x * lax.rsqrt(ms + EPS) * g2_ref[0]
        h2_ref[...] = (h * (1.0 + mods[4:5]) + mods[3:4]).astype(BF16)
        acc_ref[...] = jnp.zeros_like(acc_ref)

    hid = jnp.square(jnp.maximum(_dot(h2_ref[...], w1_ref[0].astype(BF16)), 0.0))
    acc_ref[...] += _dot(hid.astype(BF16), w2_ref[0].astype(BF16))

    @pl.when(j == pl.num_programs(1) - 1)
    def _():
        y = x_ref[...] + mods[5:6] * acc_ref[...]
        if final:
            ms = jnp.mean(y * y, axis=-1, keepdims=True)
            y = y * lax.rsqrt(ms + EPS) * fg_ref[...]
        o_ref[...] = y


def _mlp_call(x, mods, layer, cond_row0, p, final):
    n_tiles = x.shape[0] // TILE
    return pl.pallas_call(
        functools.partial(_mlp_kernel, final=final),
        grid=(n_tiles, D_FF // FF_CHUNK),
        in_specs=[
            pl.BlockSpec((TILE, D_MODEL), lambda g, j: (g, 0)),
            pl.BlockSpec((1, 1, N_MOD, D_MODEL), lambda g, j: (layer, cond_row0(g), 0, 0)),
            pl.BlockSpec((1, 1, D_MODEL), lambda g, j: (layer, 0, 0)),
            pl.BlockSpec((1, D_MODEL, FF_CHUNK), lambda g, j: (layer, 0, j)),
            pl.BlockSpec((1, FF_CHUNK, D_MODEL), lambda g, j: (layer, j, 0)),
            pl.BlockSpec((1, D_MODEL), lambda g, j: (0, 0)),
        ],
        out_specs=pl.BlockSpec((TILE, D_MODEL), lambda g, j: (g, 0)),
        out_shape=jax.ShapeDtypeStruct(x.shape, F32),
        scratch_shapes=[pltpu.VMEM((TILE, D_MODEL), BF16), pltpu.VMEM((TILE, D_MODEL), F32)],
        compiler_params=pltpu.CompilerParams(dimension_semantics=("arbitrary", "arbitrary"),
                                             vmem_limit_bytes=VMEM_LIMIT),
        name="mlp",
    )(x, mods, p['norm2_g'], p['mlp_w1'], p['mlp_w2'], p['final_g'])


def _block_diag(w, per_block):
    *lead, n_heads, d, _ = w.shape
    w = w.reshape(*lead, n_heads // per_block, per_block, d, d)
    eye = jnp.eye(per_block, dtype=w.dtype)
    bd = jnp.einsum('...hij,hk->...hikj', w, eye)
    return bd.reshape(*lead, n_heads // per_block, per_block * d, per_block * d)


def _dft_tables(n, scale):
    k = np.arange(n)
    ang = 2.0 * np.pi * ((k[:, None] * k[None, :]) % n) / n
    return np.cos(ang) * scale, np.sin(ang) * scale


def _seq_dft(seq_len):
    c, s = _dft_tables(seq_len, 1.0 / math.sqrt(seq_len))
    return jnp.asarray(np.stack([c, -s]), F32).astype(BF16)


def _channel_dft():
    c, s = _dft_tables(FOURIER_HEAD_DIM, 1.0 / math.sqrt(FOURIER_HEAD_DIM))
    eye = np.eye(N_FOURIER_HEADS)
    return jnp.asarray(np.concatenate([np.kron(eye, c), np.kron(eye, s)], axis=1), F32).astype(BF16)


def kernel(x_prompt, x_sample, state_rglru, c, c_ctx, norm1_g, norm2_g, final_g, w_mod, b_mod, w_in, conv_w, conv_b, lru_wr, lru_br, lru_wi, lru_bi, lru_lambda, pool_w, pool_scale, fourier_w, w_out, mlp_w1, mlp_w2):
    batch, seq, _ = x_prompt.shape
    dec_batch, dec_seq, _ = x_sample.shape
    assert TILE % seq == 0 and (batch * seq) % TILE == 0 and dec_seq == TILE

    w_gate = jnp.stack([_block_diag(lru_wr, 4), _block_diag(lru_wi, 4)], axis=2)
    p = {
        'norm1_g': norm1_g.reshape(DEPTH, 1, D_MODEL),
        'norm2_g': norm2_g.reshape(DEPTH, 1, D_MODEL),
        'final_g': final_g.reshape(1, D_MODEL),
        'w_in': w_in.astype(BF16),
        'conv_w': conv_w,
        'conv_b': conv_b.reshape(DEPTH, 1, D_RNN),
        'w_gate': w_gate.reshape(DEPTH, 8, 256, 256).astype(BF16),
        'b_gate': jnp.stack([lru_br, lru_bi], axis=2).reshape(DEPTH, 4, D_RNN),
        'lam': lru_lambda,
        'pool_w': _block_diag(pool_w, 4).reshape(DEPTH, D_POOL, D_POOL).astype(BF16),
        'pool_scale': pool_scale.reshape(DEPTH, 1, D_POOL),
        'cdft': _channel_dft(),
        'fourier_w': fourier_w.astype(BF16),
        'w_out': w_out.astype(BF16),
        'mlp_w1': mlp_w1,
        'mlp_w2': mlp_w2,
    }

    cond = jnp.concatenate([c_ctx[None], c, jnp.zeros((MODS_ROWS - 1 - dec_batch, D_MODEL), F32)], axis=0)
    mods = _mods_call(cond, w_mod, b_mod).reshape(DEPTH, MODS_ROWS, N_MOD, D_MODEL)

    ctx_row = lambda g: 0
    lat_row = lambda g: 1 + g
    h0_ctx = jnp.zeros((1, 1, 2, D_RNN), F32)
    dft_ctx = _seq_dft(seq)
    dft_lat = _seq_dft(dec_seq)

    xc = x_prompt.reshape(batch * seq, D_MODEL)
    xs = x_sample.reshape(dec_batch * dec_seq, D_MODEL)
    states = []
    for l in range(DEPTH):
        final = l == DEPTH - 1
        xc, st = _mixer_call(xc, mods, l, ctx_row, p, h0_ctx, lambda g: (0, 0, 0, 0), dft_ctx, seq, True)
        states.append(st.reshape(batch, 2, D_RNN))
        xc = _mlp_call(xc, mods, l, ctx_row, p, final)
        (xs,) = _mixer_call(xs, mods, l, lat_row, p, state_rglru, lambda g, l=l: (g, l, 0, 0), dft_lat,
                            dec_seq, False)
        xs = _mlp_call(xs, mods, l, lat_row, p, final)

    y_prompt = xc.reshape(batch, seq, D_MODEL)
    y_sample = xs.reshape(dec_batch, dec_seq, D_MODEL)
    new_state = jnp.stack(states, axis=1)
    return (y_prompt, y_sample, new_state)
```

```python
import functools
import math

import numpy as np
import jax
import jax.numpy as jnp
from jax import lax
from jax.experimental import pallas as pl
from jax.experimental.pallas import tpu as pltpu

D_MODEL = 1024
DEPTH = 4
GRID_W = 64
D_RNN = 512
N_LRU_HEADS = 8
LRU_HEAD_DIM = 64
LRU_C = 8.0
CONV_WIDTH = 4
CONV_LEFT = 2
D_POOL = 256
D_FOURIER = 256
N_FOURIER_HEADS = 4
FOURIER_HEAD_DIM = 64
D_IN = 2 * D_RNN + D_POOL + D_FOURIER
D_FF = 4 * D_MODEL
N_MOD = 6
EPS = 1e-6

LANES = 128
SUBLANES = 8
TILE = 1024
CHUNK = TILE // SUBLANES
N_SLAB = D_RNN // LANES
FF_CHUNK = 1024
MODS_ROWS = 8
VMEM_LIMIT = 60 * 1024 * 1024

F32 = jnp.float32
BF16 = jnp.bfloat16


def _dot(a, b):
    return jnp.dot(a, b, preferred_element_type=F32)


def _row_iota(shape):
    return lax.broadcasted_iota(jnp.int32, shape, 0)


def _shift_rows(x, d):
    n = x.shape[0]
    return pltpu.roll(x, (-d) % n, axis=0)


def _mods_kernel(cond_ref, w_ref, b_ref, o_ref):
    s = cond_ref[...]
    s = s * jax.nn.sigmoid(s)
    o_ref[0] = _dot(s.astype(BF16), w_ref[0].astype(BF16)) + b_ref[0]


def _mods_call(cond, w_mod, b_mod):
    n_tiles = (N_MOD * D_MODEL) // D_MODEL
    return pl.pallas_call(
        _mods_kernel,
        grid=(DEPTH, n_tiles),
        in_specs=[
            pl.BlockSpec((MODS_ROWS, D_MODEL), lambda l, j: (0, 0)),
            pl.BlockSpec((1, D_MODEL, D_MODEL), lambda l, j: (l, 0, j)),
            pl.BlockSpec((1, 1, D_MODEL), lambda l, j: (l, 0, j)),
        ],
        out_specs=pl.BlockSpec((1, MODS_ROWS, D_MODEL), lambda l, j: (l, 0, j)),
        out_shape=jax.ShapeDtypeStruct((DEPTH, MODS_ROWS, N_MOD * D_MODEL), F32),
        compiler_params=pltpu.CompilerParams(dimension_semantics=("arbitrary", "arbitrary")),
        name="mods",
    )(cond, w_mod, b_mod.reshape(DEPTH, 1, N_MOD * D_MODEL))


def _window_sums(x, pos, length, unit):
    fwd = x
    bwd = jnp.where(pos >= 1, _shift_rows(x, -unit), 0.0)
    outs = [fwd + bwd]
    for k in (1, 2, 4):
        fwd = fwd + jnp.where(pos + k < length, _shift_rows(fwd, k * unit), 0.0)
        bwd = bwd + jnp.where(pos - k >= 0, _shift_rows(bwd, -k * unit), 0.0)
        outs.append(fwd + bwd)
    return outs


def _pool_axis(x, pos, length, unit):
    s1, s2, s4, s8 = _window_sums(x, pos, length, unit)
    lane = lax.broadcasted_iota(jnp.int32, x.shape, 1)
    sel = jnp.where(lane < 64, s1, jnp.where(lane < 128, s2, jnp.where(lane < 192, s4, s8)))
    half = jnp.where(lane < 64, 1, jnp.where(lane < 128, 2, jnp.where(lane < 192, 4, 8)))
    cnt = jnp.minimum(pos + half, length) - jnp.maximum(pos - half, 0)
    return sel, cnt.astype(F32)


def _mixer_kernel(x_ref, mods_ref, g1_ref, win_ref, cw_ref, cb_ref, wg_ref, bg_ref, lam_ref, h0_ref,
                  pw_ref, ps_ref, cdft_ref, fw_ref, dft_ref, wout_ref,
                  xo_ref, st_ref,
                  u_ref, xc_ref, gate_ref, scan_ref, cat_ref, *, seq_len):
    n_seq = TILE // seq_len
    chunks_per_seq = seq_len // CHUNK
    mods = mods_ref[0, 0]

    x = x_ref[...]
    ms = jnp.mean(x * x, axis=-1, keepdims=True)
    h = x * lax.rsqrt(ms + EPS) * g1_ref[0]
    h = h * (1.0 + mods[1:2]) + mods[0:1]
    u_ref[...] = _dot(h.astype(BF16), win_ref[0])

    xr = u_ref[:, 0:D_RNN]
    pos = _row_iota((TILE, D_RNN)) & (seq_len - 1)
    acc = cb_ref[0] + xr * cw_ref[0, CONV_LEFT:CONV_LEFT + 1, :]
    for k in range(CONV_WIDTH):
        d = k - CONV_LEFT
        if d == 0:
            continue
        valid = (pos + d >= 0) & (pos + d < seq_len)
        acc = acc + jnp.where(valid, _shift_rows(xr, d), 0.0) * cw_ref[0, k:k + 1, :]
    xc_ref[...] = acc

    xcb = xc_ref[...].astype(BF16)
    for d in range(2):
        for gate in range(2):
            for blk in range(2):
                pre = _dot(xcb[:, blk * 256:(blk + 1) * 256], wg_ref[0, d * 4 + gate * 2 + blk])
                gate_ref[gate * N_SLAB + 2 * blk] = pre[:, 0:LANES]
                gate_ref[gate * N_SLAB + 2 * blk + 1] = pre[:, LANES:2 * LANES]
        lam = lam_ref[0, d:d + 1, :]
        softplus_neg_lam = jnp.maximum(-lam, 0.0) + jnp.log1p(jnp.exp(-jnp.abs(lam)))
        half_rate = (-0.5 * LRU_C) * softplus_neg_lam
        b_r = bg_ref[0, 2 * d:2 * d + 1, :]
        b_i = bg_ref[0, 2 * d + 1:2 * d + 2, :]

        def gate_chunk(c, carry, d=d, half_rate=half_rate, b_r=b_r, b_i=b_i):
            r0 = pl.multiple_of(c * CHUNK, CHUNK)
            for s in range(N_SLAB):
                lanes = slice(s * LANES, (s + 1) * LANES)
                r = 0.5 + 0.5 * jnp.tanh(0.5 * (gate_ref[s, pl.ds(r0, CHUNK), :] + b_r[:, lanes]))
                i = 0.5 + 0.5 * jnp.tanh(0.5 * (gate_ref[N_SLAB + s, pl.ds(r0, CHUNK), :] + b_i[:, lanes]))
                th = jnp.tanh(r * half_rate[:, lanes])
                q = 1.0 / (1.0 - th)
                a = (1.0 + th) * q
                b = (2.0 * jnp.sqrt(-th) * q) * (i * xc_ref[pl.ds(r0, CHUNK), lanes])
                scan_ref[2 * d, s, pl.ds(c, CHUNK, stride=SUBLANES), :] = a
                scan_ref[2 * d + 1, s, pl.ds(c, CHUNK, stride=SUBLANES), :] = b
            return carry

        lax.fori_loop(0, SUBLANES, gate_chunk, 0)

    def load(arr, s, t):
        return scan_ref[arr, s, pl.ds(pl.multiple_of(t * SUBLANES, SUBLANES), SUBLANES), :]

    def pass1(t, carry):
        hf, pf, hb, pb = carry
        tb = CHUNK - 1 - t
        nhf, npf, nhb, npb = [], [], [], []
        for s in range(N_SLAB):
            af = load(0, s, t)
            ab = load(2, s, tb)
            nhf.append(af * hf[s] + load(1, s, t))
            npf.append(af * pf[s])
            nhb.append(ab * hb[s] + load(3, s, tb))
            npb.append(ab * pb[s])
        return tuple(nhf), tuple(npf), tuple(nhb), tuple(npb)

    zeros = tuple(jnp.zeros((SUBLANES, LANES), F32) for _ in range(N_SLAB))
    ones = tuple(jnp.ones((SUBLANES, LANES), F32) for _ in range(N_SLAB))
    ef, pf, eb, pb = lax.fori_loop(0, CHUNK, pass1, (zeros, ones, zeros, ones), unroll=4)

    row = _row_iota((SUBLANES, LANES))
    seq_first = (row & (chunks_per_seq - 1)) == 0
    seq_last = (row & (chunks_per_seq - 1)) == chunks_per_seq - 1
    init_f, init_b = [], []
    for s in range(N_SLAB):
        h0f = jnp.broadcast_to(h0_ref[0, 0, 0:1, s * LANES:(s + 1) * LANES], (SUBLANES, LANES))
        h0b = jnp.broadcast_to(h0_ref[0, 0, 1:2, s * LANES:(s + 1) * LANES], (SUBLANES, LANES))
        cf, cb = h0f, h0b
        for _ in range(chunks_per_seq - 1):
            cf = jnp.where(seq_first, h0f, pltpu.roll(ef[s] + pf[s] * cf, 1, axis=0))
            cb = jnp.where(seq_last, h0b, pltpu.roll(eb[s] + pb[s] * cb, SUBLANES - 1, axis=0))
        init_f.append(cf)
        init_b.append(cb)

    def pass2(t, carry):
        hf, hb = carry
        tb = CHUNK - 1 - t
        nhf, nhb = [], []
        for s in range(N_SLAB):
            vf = load(0, s, t) * hf[s] + load(1, s, t)
            vb = load(2, s, tb) * hb[s] + load(3, s, tb)
            gate_ref[s, pl.ds(pl.multiple_of(t * SUBLANES, SUBLANES), SUBLANES), :] = vf
            gate_ref[N_SLAB + s, pl.ds(pl.multiple_of(tb * SUBLANES, SUBLANES), SUBLANES), :] = vb
            nhf.append(vf)
            nhb.append(vb)
        return tuple(nhf), tuple(nhb)

    lax.fori_loop(0, CHUNK, pass2, (tuple(init_f), tuple(init_b)), unroll=4)

    if st_ref is not None:
        even = (_row_iota((SUBLANES, LANES)) & 1) == 0
        for s in range(N_SLAB):
            last_f = gate_ref[s, pl.ds((CHUNK - 1) * SUBLANES, SUBLANES), :]
            first_b = gate_ref[N_SLAB + s, pl.ds(0, SUBLANES), :]
            st_ref[0, :, s * LANES:(s + 1) * LANES] = jnp.where(
                even, pltpu.roll(last_f, SUBLANES - 1, axis=0), pltpu.roll(first_b, 1, axis=0))

    def out_chunk(c, carry):
        r0 = pl.multiple_of(c * CHUNK, CHUNK)
        for s in range(N_SLAB):
            hsum = (gate_ref[s, pl.ds(c, CHUNK, stride=SUBLANES), :]
                    + gate_ref[N_SLAB + s, pl.ds(c, CHUNK, stride=SUBLANES), :])
            xg = u_ref[pl.ds(r0, CHUNK), D_RNN + s * LANES:D_RNN + (s + 1) * LANES]
            cat_ref[pl.ds(r0, CHUNK), s * LANES:(s + 1) * LANES] = (hsum * jax.nn.gelu(xg)).astype(BF16)
        return carry

    lax.fori_loop(0, SUBLANES, out_chunk, 0)

    xp = u_ref[:, 2 * D_RNN:2 * D_RNN + D_POOL]
    tok = _row_iota((TILE, D_POOL))
    if seq_len == TILE:
        n_rows = TILE // GRID_W
        col_sum, col_cnt = _pool_axis(xp, tok & (GRID_W - 1), GRID_W, 1)
        win_sum, row_cnt = _pool_axis(col_sum, lax.shift_right_logical(tok, GRID_W.bit_length() - 1),
                                      n_rows, GRID_W)
        cnt = row_cnt * col_cnt
    else:
        win_sum, cnt = _pool_axis(xp, tok & (seq_len - 1), seq_len, 1)
    pooled = win_sum / cnt - xp
    cat_ref[:, D_RNN:D_RNN + D_POOL] = (_dot(pooled.astype(BF16), pw_ref[0]) * ps_ref[0]).astype(BF16)

    xf = u_ref[:, 2 * D_RNN + D_POOL:D_IN].astype(BF16)
    y1 = _dot(xf, cdft_ref[...]).astype(BF16)
    for q in range(n_seq):
        rows = slice(q * seq_len, (q + 1) * seq_len)
        four = _dot(dft_ref[0], y1[rows, 0:D_FOURIER]) + _dot(dft_ref[1], y1[rows, D_FOURIER:])
        cat_ref[rows, D_RNN + D_POOL:] = _dot(four.astype(BF16), fw_ref[0]).astype(BF16)

    xo_ref[...] = x_ref[...] + mods[2:3] * _dot(cat_ref[...], wout_ref[0])


def _mixer_call(x, mods, layer, cond_row0, p, h0, h0_index, dft, seq_len, with_state):
    n_tiles = x.shape[0] // TILE
    n_seq = TILE // seq_len
    lyr = lambda *shape: pl.BlockSpec((1,) + shape, lambda g: (layer,) + (0,) * len(shape))
    in_specs = [
        pl.BlockSpec((TILE, D_MODEL), lambda g: (g, 0)),
        pl.BlockSpec((1, 1, N_MOD, D_MODEL), lambda g: (layer, cond_row0(g), 0, 0)),
        lyr(1, D_MODEL),
        lyr(D_MODEL, D_IN),
        lyr(CONV_WIDTH, D_RNN),
        lyr(1, D_RNN),
        lyr(8, 256, 256),
        lyr(4, D_RNN),
        lyr(2, D_RNN),
        pl.BlockSpec((1, 1, 2, D_RNN), h0_index),
        lyr(D_POOL, D_POOL),
        lyr(1, D_POOL),
        pl.BlockSpec((D_FOURIER, 2 * D_FOURIER), lambda g: (0, 0)),
        lyr(D_FOURIER, D_FOURIER),
        pl.BlockSpec((2, seq_len, seq_len), lambda g: (0, 0, 0)),
        lyr(D_MODEL, D_MODEL),
    ]
    out_specs = [pl.BlockSpec((TILE, D_MODEL), lambda g: (g, 0))]
    out_shape = [jax.ShapeDtypeStruct(x.shape, F32)]
    if with_state:
        out_specs.append(pl.BlockSpec((1, 2 * n_seq, D_RNN), lambda g: (g, 0, 0)))
        out_shape.append(jax.ShapeDtypeStruct((n_tiles, 2 * n_seq, D_RNN), F32))
        body = functools.partial(_mixer_kernel, seq_len=seq_len)
    else:
        def body(*refs):
            _mixer_kernel(*refs[:17], None, *refs[17:], seq_len=seq_len)
    return pl.pallas_call(
        body,
        grid=(n_tiles,),
        in_specs=in_specs,
        out_specs=out_specs,
        out_shape=out_shape,
        scratch_shapes=[
            pltpu.VMEM((TILE, D_IN), F32),
            pltpu.VMEM((TILE, D_RNN), F32),
            pltpu.VMEM((2 * N_SLAB, TILE, LANES), F32),
            pltpu.VMEM((4, N_SLAB, TILE, LANES), F32),
            pltpu.VMEM((TILE, D_MODEL), BF16),
        ],
        compiler_params=pltpu.CompilerParams(dimension_semantics=("arbitrary",),
                                             vmem_limit_bytes=VMEM_LIMIT),
        name="mixer_ctx" if with_state else "mixer_lat",
    )(x, mods, p['norm1_g'], p['w_in'], p['conv_w'], p['conv_b'], p['w_gate'], p['b_gate'], p['lam'], h0,
      p['pool_w'], p['pool_scale'], p['cdft'], p['fourier_w'], dft, p['w_out'])


def _mlp_kernel(x_ref, mods_ref, g2_ref, w1_ref, w2_ref, fg_ref, o_ref, h2_ref, acc_ref, *, final):
    j = pl.program_id(1)
    last = pl.num_programs(1) - 1
    mods = mods_ref[0, 0]

    def chunk_out(h2):
        hid = jnp.square(jnp.maximum(_dot(h2, w1_ref[0].astype(BF16)), 0.0))
        return _dot(hid.astype(BF16), w2_ref[0].astype(BF16))

    @pl.when(j == 0)
    def _():
        x = x_ref[...]
        ms = jnp.mean(x * x, axis=-1, keepdims=True)
        h = x * lax.rsqrt(ms + EPS) * g2_ref[0]
        h2 = (h * (1.0 + mods[4:5]) + mods[3:4]).astype(BF16)
        h2_ref[...] = h2
        acc_ref[...] = chunk_out(h2)

    @pl.when((j > 0) & (j < last))
    def _():
        acc_ref[...] += chunk_out(h2_ref[...])

    @pl.when(j == last)
    def _():
        y = x_ref[...] + mods[5:6] * (acc_ref[...] + chunk_out(h2_ref[...]))
        if final:
            ms = jnp.mean(y * y, axis=-1, keepdims=True)
            y = y * lax.rsqrt(ms + EPS) * fg_ref[...]
        o_ref[...] = y


def _mlp_call(x, mods, layer, cond_row0, p, final):
    n_tiles = x.shape[0] // TILE
    return pl.pallas_call(
        functools.partial(_mlp_kernel, final=final),
        grid=(n_tiles, D_FF // FF_CHUNK),
        in_specs=[
            pl.BlockSpec((TILE, D_MODEL), lambda g, j: (g, 0)),
            pl.BlockSpec((1, 1, N_MOD, D_MODEL), lambda g, j: (layer, cond_row0(g), 0, 0)),
            pl.BlockSpec((1, 1, D_MODEL), lambda g, j: (layer, 0, 0)),
            pl.BlockSpec((1, D_MODEL, FF_CHUNK), lambda g, j: (layer, 0, j)),
            pl.BlockSpec((1, FF_CHUNK, D_MODEL), lambda g, j: (layer, j, 0)),
            pl.BlockSpec((1, D_MODEL), lambda g, j: (0, 0)),
        ],
        out_specs=pl.BlockSpec((TILE, D_MODEL), lambda g, j: (g, 0)),
        out_shape=jax.ShapeDtypeStruct(x.shape, F32),
        scratch_shapes=[pltpu.VMEM((TILE, D_MODEL), BF16), pltpu.VMEM((TILE, D_MODEL), F32)],
        compiler_params=pltpu.CompilerParams(dimension_semantics=("arbitrary", "arbitrary"),
                                             vmem_limit_bytes=VMEM_LIMIT),
        name="mlp",
    )(x, mods, p['norm2_g'], p['mlp_w1'], p['mlp_w2'], p['final_g'])


def _block_diag(w, per_block):
    *lead, n_heads, d, _ = w.shape
    w = w.reshape(*lead, n_heads // per_block, per_block, d, d)
    eye = jnp.eye(per_block, dtype=w.dtype)
    bd = jnp.einsum('...hij,hk->...hikj', w, eye)
    return bd.reshape(*lead, n_heads // per_block, per_block * d, per_block * d)


def _dft_tables(n, scale):
    k = np.arange(n)
    ang = 2.0 * np.pi * ((k[:, None] * k[None, :]) % n) / n
    return np.cos(ang) * scale, np.sin(ang) * scale


def _seq_dft(seq_len):
    c, s = _dft_tables(seq_len, 1.0 / math.sqrt(seq_len))
    return jnp.asarray(np.stack([c, -s]), F32).astype(BF16)


def _channel_dft():
    c, s = _dft_tables(FOURIER_HEAD_DIM, 1.0 / math.sqrt(FOURIER_HEAD_DIM))
    eye = np.eye(N_FOURIER_HEADS)
    return jnp.asarray(np.concatenate([np.kron(eye, c), np.kron(eye, s)], axis=1), F32).astype(BF16)


def kernel(x_prompt, x_sample, state_rglru, c, c_ctx, norm1_g, norm2_g, final_g, w_mod, b_mod, w_in, conv_w, conv_b, lru_wr, lru_br, lru_wi, lru_bi, lru_lambda, pool_w, pool_scale, fourier_w, w_out, mlp_w1, mlp_w2):
    batch, seq, _ = x_prompt.shape
    dec_batch, dec_seq, _ = x_sample.shape
    assert TILE % seq == 0 and (batch * seq) % TILE == 0 and dec_seq == TILE

    w_gate = jnp.stack([_block_diag(lru_wr, 4), _block_diag(lru_wi, 4)], axis=2)
    p = {
        'norm1_g': norm1_g.reshape(DEPTH, 1, D_MODEL),
        'norm2_g': norm2_g.reshape(DEPTH, 1, D_MODEL),
        'final_g': final_g.reshape(1, D_MODEL),
        'w_in': w_in.astype(BF16),
        'conv_w': conv_w,
        'conv_b': conv_b.reshape(DEPTH, 1, D_RNN),
        'w_gate': w_gate.reshape(DEPTH, 8, 256, 256).astype(BF16),
        'b_gate': jnp.stack([lru_br, lru_bi], axis=2).reshape(DEPTH, 4, D_RNN),
        'lam': lru_lambda,
        'pool_w': _block_diag(pool_w, 4).reshape(DEPTH, D_POOL, D_POOL).astype(BF16),
        'pool_scale': pool_scale.reshape(DEPTH, 1, D_POOL),
        'cdft': _channel_dft(),
        'fourier_w': fourier_w.astype(BF16),
        'w_out': w_out.astype(BF16),
        'mlp_w1': mlp_w1,
        'mlp_w2': mlp_w2,
    }

    cond = jnp.concatenate([c_ctx[None], c, jnp.zeros((MODS_ROWS - 1 - dec_batch, D_MODEL), F32)], axis=0)
    mods = _mods_call(cond, w_mod, b_mod).reshape(DEPTH, MODS_ROWS, N_MOD, D_MODEL)

    ctx_row = lambda g: 0
    lat_row = lambda g: 1 + g
    h0_ctx = jnp.zeros((1, 1, 2, D_RNN), F32)
    dft_ctx = _seq_dft(seq)
    dft_lat = _seq_dft(dec_seq)

    xc = x_prompt.reshape(batch * seq, D_MODEL)
    xs = x_sample.reshape(dec_batch * dec_seq, D_MODEL)
    states = []
    for l in range(DEPTH):
        final = l == DEPTH - 1
        xc, st = _mixer_call(xc, mods, l, ctx_row, p, h0_ctx, lambda g: (0, 0, 0, 0), dft_ctx, seq, True)
        states.append(st.reshape(batch, 2, D_RNN))
        xc = _mlp_call(xc, mods, l, ctx_row, p, final)
        (xs,) = _mixer_call(xs, mods, l, lat_row, p, state_rglru, lambda g, l=l: (g, l, 0, 0), dft_lat,
                            dec_seq, False)
        xs = _mlp_call(xs, mods, l, lat_row, p, final)

    y_prompt = xc.reshape(batch, seq, D_MODEL)
    y_sample = xs.reshape(dec_batch, dec_seq, D_MODEL)
    new_state = jnp.stack(states, axis=1)
    return (y_prompt, y_sample, new_state)
```

```python
import functools
import math

import numpy as np
import jax
import jax.numpy as jnp
from jax import lax
from jax.experimental import pallas as pl
from jax.experimental.pallas import tpu as pltpu

D_MODEL = 1024
DEPTH = 4
GRID_W = 64
D_RNN = 512
N_LRU_HEADS = 8
LRU_HEAD_DIM = 64
LRU_C = 8.0
CONV_WIDTH = 4
CONV_LEFT = 2
D_POOL = 256
D_FOURIER = 256
N_FOURIER_HEADS = 4
FOURIER_HEAD_DIM = 64
D_IN = 2 * D_RNN + D_POOL + D_FOURIER
D_FF = 4 * D_MODEL
N_MOD = 6
EPS = 1e-6

LANES = 128
SUBLANES = 8
TILE = 1024
CHUNK = TILE // SUBLANES
N_SLAB = D_RNN // LANES
SLAB_ROWS = TILE + SUBLANES
SCAN_GROUP = 4
SCAN_UNROLL = 4
HEADS_PER_BLOCK = 4
BLOCK = HEADS_PER_BLOCK * LRU_HEAD_DIM
FF_CHUNK = 1024
MODS_ROWS = 8
VMEM_LIMIT = 60 * 1024 * 1024

F32 = jnp.float32
BF16 = jnp.bfloat16


def _dot(a, b):
    return jnp.dot(a, b, preferred_element_type=F32)


def _row_iota(shape):
    return lax.broadcasted_iota(jnp.int32, shape, 0)


def _shift_rows(x, d):
    n = x.shape[0]
    return pltpu.roll(x, (-d) % n, axis=0)


def _mods_kernel(cond_ref, w_ref, b_ref, o_ref):
    s = cond_ref[...]
    s = s * jax.nn.sigmoid(s)
    o_ref[0] = _dot(s.astype(BF16), w_ref[0].astype(BF16)) + b_ref[0]


def _mods_call(cond, w_mod, b_mod):
    n_tiles = (N_MOD * D_MODEL) // D_MODEL
    return pl.pallas_call(
        _mods_kernel,
        grid=(DEPTH, n_tiles),
        in_specs=[
            pl.BlockSpec((MODS_ROWS, D_MODEL), lambda l, j: (0, 0)),
            pl.BlockSpec((1, D_MODEL, D_MODEL), lambda l, j: (l, 0, j)),
            pl.BlockSpec((1, 1, D_MODEL), lambda l, j: (l, 0, j)),
        ],
        out_specs=pl.BlockSpec((1, MODS_ROWS, D_MODEL), lambda l, j: (l, 0, j)),
        out_shape=jax.ShapeDtypeStruct((DEPTH, MODS_ROWS, N_MOD * D_MODEL), F32),
        compiler_params=pltpu.CompilerParams(dimension_semantics=("arbitrary", "arbitrary")),
        name="mods",
    )(cond, w_mod, b_mod.reshape(DEPTH, 1, N_MOD * D_MODEL))


def _window_sums(x, pos, length, unit):
    fwd = x
    bwd = jnp.where(pos >= 1, _shift_rows(x, -unit), 0.0)
    outs = [fwd + bwd]
    for k in (1, 2, 4):
        fwd = fwd + jnp.where(pos + k < length, _shift_rows(fwd, k * unit), 0.0)
        bwd = bwd + jnp.where(pos - k >= 0, _shift_rows(bwd, -k * unit), 0.0)
        outs.append(fwd + bwd)
    return outs


def _pool_axis(x, pos, length, unit):
    s1, s2, s4, s8 = _window_sums(x, pos, length, unit)
    lane = lax.broadcasted_iota(jnp.int32, x.shape, 1)
    sel = jnp.where(lane < 64, s1, jnp.where(lane < 128, s2, jnp.where(lane < 192, s4, s8)))
    half = jnp.where(lane < 64, 1, jnp.where(lane < 128, 2, jnp.where(lane < 192, 4, 8)))
    cnt = jnp.minimum(pos + half, length) - jnp.maximum(pos - half, 0)
    return sel, cnt.astype(F32)


def _mixer_kernel(x_ref, mods_ref, g1_ref, win_ref, cw_ref, cb_ref, wr_ref, wi_ref, bg_ref, lam_ref, h0_ref,
                  pw_ref, ps_ref, cdft_ref, fw_ref, dft_ref, wout_ref,
                  xo_ref, st_ref,
                  wg_ref, pwd_ref, u_ref, xc_ref, gate_ref, scan_ref, cat_ref, *, seq_len):
    n_seq = TILE // seq_len
    chunks_per_seq = seq_len // CHUNK
    mods = mods_ref[0, 0]

    @pl.when(pl.program_id(0) == 0)
    def _():
        wg_ref[...] = jnp.zeros_like(wg_ref)
        pwd_ref[...] = jnp.zeros_like(pwd_ref)
        for hd in range(HEADS_PER_BLOCK):
            rows = slice(hd * LRU_HEAD_DIM, (hd + 1) * LRU_HEAD_DIM)
            pwd_ref[rows, rows] = pw_ref[0, hd].astype(BF16)
            for d in range(2):
                for gate, w_ref in enumerate((wr_ref, wi_ref)):
                    for blk in range(2):
                        w = w_ref[0, d, blk * HEADS_PER_BLOCK + hd]
                        wg_ref[d * 4 + gate * 2 + blk, rows, rows] = (0.5 * w).astype(BF16)

    x = x_ref[...]
    ms = jnp.mean(x * x, axis=-1, keepdims=True)
    h = x * lax.rsqrt(ms + EPS) * g1_ref[0]
    h = h * (1.0 + mods[1:2]) + mods[0:1]
    u_ref[...] = _dot(h.astype(BF16), win_ref[0])

    xr = u_ref[:, 0:D_RNN]
    pos = _row_iota((TILE, D_RNN)) & (seq_len - 1)
    acc = cb_ref[0] + xr * cw_ref[0, CONV_LEFT:CONV_LEFT + 1, :]
    for k in range(CONV_WIDTH):
        d = k - CONV_LEFT
        if d == 0:
            continue
        valid = (pos + d >= 0) & (pos + d < seq_len)
        acc = acc + jnp.where(valid, _shift_rows(xr, d), 0.0) * cw_ref[0, k:k + 1, :]
    xc_ref[...] = acc

    xcb = xc_ref[...].astype(BF16)
    for d in range(2):
        for gate in range(2):
            for blk in range(2):
                pre = _dot(xcb[:, blk * BLOCK:(blk + 1) * BLOCK], wg_ref[d * 4 + gate * 2 + blk])
                gate_ref[gate * N_SLAB + 2 * blk, 0:TILE, :] = pre[:, 0:LANES]
                gate_ref[gate * N_SLAB + 2 * blk + 1, 0:TILE, :] = pre[:, LANES:2 * LANES]
        lam = lam_ref[0, d:d + 1, :]
        softplus_neg_lam = jnp.maximum(-lam, 0.0) + jnp.log1p(jnp.exp(-jnp.abs(lam)))
        quarter_rate = (-0.25 * LRU_C) * softplus_neg_lam
        hb_r = 0.5 * bg_ref[0, 2 * d:2 * d + 1, :]
        hb_i = 0.5 * bg_ref[0, 2 * d + 1:2 * d + 2, :]

        def gate_chunk(c, carry, d=d, quarter_rate=quarter_rate, hb_r=hb_r, hb_i=hb_i):
            r0 = pl.multiple_of(c * CHUNK, CHUNK)
            for s in range(N_SLAB):
                lanes = slice(s * LANES, (s + 1) * LANES)
                t_r = jnp.tanh(gate_ref[s, pl.ds(r0, CHUNK), :] + hb_r[:, lanes])
                t_i = jnp.tanh(gate_ref[N_SLAB + s, pl.ds(r0, CHUNK), :] + hb_i[:, lanes])
                rate = quarter_rate[:, lanes]
                th = jnp.tanh(rate + rate * t_r)
                q = 1.0 / (1.0 - th)
                a = (1.0 + th) * q
                neg = -th
                root = jnp.where(neg > 0.0, neg * lax.rsqrt(neg), 0.0)
                b = (root * q) * ((1.0 + t_i) * xc_ref[pl.ds(r0, CHUNK), lanes])
                scan_ref[2 * d, s, pl.ds(c, CHUNK, stride=SUBLANES), :] = a
                scan_ref[2 * d + 1, s, pl.ds(c, CHUNK, stride=SUBLANES), :] = b
            return carry

        lax.fori_loop(0, SUBLANES, gate_chunk, 0)

    def load(arr, s, t):
        return scan_ref[arr, s, pl.ds(pl.multiple_of(t * SUBLANES, SUBLANES), SUBLANES), :]

    def compose(first, second):
        (a1, b1), (a2, b2) = first, second
        return a2 * a1, a2 * b1 + b2

    def group_steps(arr, s, t0, sign):
        return [(load(arr, s, t0 + sign * k), load(arr + 1, s, t0 + sign * k)) for k in range(SCAN_GROUP)]

    def pass1(i, carry):
        hf, pf, hb, pb = carry
        nhf, npf, nhb, npb = [], [], [], []
        for s in range(N_SLAB):
            f = group_steps(0, s, i * SCAN_GROUP, 1)
            af, bf = compose(compose(f[0], f[1]), compose(f[2], f[3]))
            nhf.append(af * hf[s] + bf)
            npf.append(af * pf[s])
            g = group_steps(2, s, CHUNK - 1 - i * SCAN_GROUP, -1)
            ab, bb = compose(compose(g[0], g[1]), compose(g[2], g[3]))
            nhb.append(ab * hb[s] + bb)
            npb.append(ab * pb[s])
        return tuple(nhf), tuple(npf), tuple(nhb), tuple(npb)

    zeros = tuple(jnp.zeros((SUBLANES, LANES), F32) for _ in range(N_SLAB))
    ones = tuple(jnp.ones((SUBLANES, LANES), F32) for _ in range(N_SLAB))
    ef, pf, eb, pb = lax.fori_loop(0, CHUNK // SCAN_GROUP, pass1, (zeros, ones, zeros, ones),
                                   unroll=SCAN_UNROLL)

    row = _row_iota((SUBLANES, LANES))
    seq_first = (row & (chunks_per_seq - 1)) == 0
    seq_last = (row & (chunks_per_seq - 1)) == chunks_per_seq - 1
    init_f, init_b = [], []
    for s in range(N_SLAB):
        h0f = jnp.broadcast_to(h0_ref[0, 0, 0:1, s * LANES:(s + 1) * LANES], (SUBLANES, LANES))
        h0b = jnp.broadcast_to(h0_ref[0, 0, 1:2, s * LANES:(s + 1) * LANES], (SUBLANES, LANES))
        cf, cb = h0f, h0b
        for _ in range(chunks_per_seq - 1):
            cf = jnp.where(seq_first, h0f, pltpu.roll(ef[s] + pf[s] * cf, 1, axis=0))
            cb = jnp.where(seq_last, h0b, pltpu.roll(eb[s] + pb[s] * cb, SUBLANES - 1, axis=0))
        init_f.append(cf)
        init_b.append(cb)

    def run_group(steps, h, out_slab, t0, sign):
        for k in range(0, SCAN_GROUP, 2):
            a0, b0 = steps[k]
            a01, b01 = compose(steps[k], steps[k + 1])
            h_even = a0 * h + b0
            h = a01 * h + b01
            for step, val in ((k, h_even), (k + 1, h)):
                t = t0 + sign * step
                gate_ref[out_slab, pl.ds(pl.multiple_of(t * SUBLANES, SUBLANES), SUBLANES), :] = val
        return h

    def pass2(i, carry):
        hf, hb = carry
        nhf, nhb = [], []
        for s in range(N_SLAB):
            tf0 = i * SCAN_GROUP
            tb0 = CHUNK - 1 - i * SCAN_GROUP
            nhf.append(run_group(group_steps(0, s, tf0, 1), hf[s], s, tf0, 1))
            nhb.append(run_group(group_steps(2, s, tb0, -1), hb[s], N_SLAB + s, tb0, -1))
        return tuple(nhf), tuple(nhb)

    lax.fori_loop(0, CHUNK // SCAN_GROUP, pass2, (tuple(init_f), tuple(init_b)), unroll=SCAN_UNROLL)

    if st_ref is not None:
        even = (_row_iota((SUBLANES, LANES)) & 1) == 0
        for s in range(N_SLAB):
            last_f = gate_ref[s, pl.ds((CHUNK - 1) * SUBLANES, SUBLANES), :]
            first_b = gate_ref[N_SLAB + s, pl.ds(0, SUBLANES), :]
            st_ref[0, :, s * LANES:(s + 1) * LANES] = jnp.where(
                even, pltpu.roll(last_f, SUBLANES - 1, axis=0), pltpu.roll(first_b, 1, axis=0))

    def out_chunk(c, carry):
        r0 = pl.multiple_of(c * CHUNK, CHUNK)
        for s in range(N_SLAB):
            hsum = (gate_ref[s, pl.ds(c, CHUNK, stride=SUBLANES), :]
                    + gate_ref[N_SLAB + s, pl.ds(c, CHUNK, stride=SUBLANES), :])
            xg = u_ref[pl.ds(r0, CHUNK), D_RNN + s * LANES:D_RNN + (s + 1) * LANES]
            cat_ref[pl.ds(r0, CHUNK), s * LANES:(s + 1) * LANES] = (hsum * jax.nn.gelu(xg)).astype(BF16)
        return carry

    lax.fori_loop(0, SUBLANES, out_chunk, 0)

    xp = u_ref[:, 2 * D_RNN:2 * D_RNN + D_POOL]
    tok = _row_iota((TILE, D_POOL))
    if seq_len == TILE:
        n_rows = TILE // GRID_W
        col_sum, col_cnt = _pool_axis(xp, tok & (GRID_W - 1), GRID_W, 1)
        win_sum, row_cnt = _pool_axis(col_sum, lax.shift_right_logical(tok, GRID_W.bit_length() - 1),
                                      n_rows, GRID_W)
        cnt = row_cnt * col_cnt
    else:
        win_sum, cnt = _pool_axis(xp, tok & (seq_len - 1), seq_len, 1)
    pooled = win_sum / cnt - xp
    cat_ref[:, D_RNN:D_RNN + D_POOL] = (_dot(pooled.astype(BF16), pwd_ref[...]) * ps_ref[0]).astype(BF16)

    xf = u_ref[:, 2 * D_RNN + D_POOL:D_IN].astype(BF16)
    y1 = _dot(xf, cdft_ref[...]).astype(BF16)
    for q in range(n_seq):
        rows = slice(q * seq_len, (q + 1) * seq_len)
        four = _dot(dft_ref[0], y1[rows, 0:D_FOURIER]) + _dot(dft_ref[1], y1[rows, D_FOURIER:])
        cat_ref[rows, D_RNN + D_POOL:] = _dot(four.astype(BF16), fw_ref[0]).astype(BF16)

    xo_ref[...] = x_ref[...] + mods[2:3] * _dot(cat_ref[...], wout_ref[0])


def _mixer_call(x, mods, layer, cond_row0, p, h0, h0_index, dft, seq_len, with_state):
    n_tiles = x.shape[0] // TILE
    n_seq = TILE // seq_len
    lyr = lambda *shape: pl.BlockSpec((1,) + shape, lambda g: (layer,) + (0,) * len(shape))
    in_specs = [
        pl.BlockSpec((TILE, D_MODEL), lambda g: (g, 0)),
        pl.BlockSpec((1, 1, N_MOD, D_MODEL), lambda g: (layer, cond_row0(g), 0, 0)),
        lyr(1, D_MODEL),
        lyr(D_MODEL, D_IN),
        lyr(CONV_WIDTH, D_RNN),
        lyr(1, D_RNN),
        lyr(2, N_LRU_HEADS, LRU_HEAD_DIM, LRU_HEAD_DIM),
        lyr(2, N_LRU_HEADS, LRU_HEAD_DIM, LRU_HEAD_DIM),
        lyr(4, D_RNN),
        lyr(2, D_RNN),
        pl.BlockSpec((1, 1, 2, D_RNN), h0_index),
        lyr(HEADS_PER_BLOCK, LRU_HEAD_DIM, LRU_HEAD_DIM),
        lyr(1, D_POOL),
        pl.BlockSpec((D_FOURIER, 2 * D_FOURIER), lambda g: (0, 0)),
        lyr(D_FOURIER, D_FOURIER),
        pl.BlockSpec((2, seq_len, seq_len), lambda g: (0, 0, 0)),
        lyr(D_MODEL, D_MODEL),
    ]
    out_specs = [pl.BlockSpec((TILE, D_MODEL), lambda g: (g, 0))]
    out_shape = [jax.ShapeDtypeStruct(x.shape, F32)]
    if with_state:
        out_specs.append(pl.BlockSpec((1, 2 * n_seq, D_RNN), lambda g: (g, 0, 0)))
        out_shape.append(jax.ShapeDtypeStruct((n_tiles, 2 * n_seq, D_RNN), F32))
        body = functools.partial(_mixer_kernel, seq_len=seq_len)
    else:
        def body(*refs):
            _mixer_kernel(*refs[:18], None, *refs[18:], seq_len=seq_len)
    return pl.pallas_call(
        body,
        grid=(n_tiles,),
        in_specs=in_specs,
        out_specs=out_specs,
        out_shape=out_shape,
        scratch_shapes=[
            pltpu.VMEM((8, BLOCK, BLOCK), BF16),
            pltpu.VMEM((D_POOL, D_POOL), BF16),
            pltpu.VMEM((TILE, D_IN), F32),
            pltpu.VMEM((TILE, D_RNN), F32),
            pltpu.VMEM((2 * N_SLAB, SLAB_ROWS, LANES), F32),
            pltpu.VMEM((4, N_SLAB, SLAB_ROWS, LANES), F32),
            pltpu.VMEM((TILE, D_MODEL), BF16),
        ],
        compiler_params=pltpu.CompilerParams(dimension_semantics=("arbitrary",),
                                             vmem_limit_bytes=VMEM_LIMIT),
        name="mixer_ctx" if with_state else "mixer_lat",
    )(x, mods, p['norm1_g'], p['w_in'], p['conv_w'], p['conv_b'], p['lru_wr'], p['lru_wi'], p['b_gate'], p['lam'], h0,
      p['pool_w'], p['pool_scale'], p['cdft'], p['fourier_w'], dft, p['w_out'])


def _mlp_kernel(x_ref, mods_ref, g2_ref, w1_ref, w2_ref, fg_ref, o_ref, h2_ref, acc_ref, *, final):
    j = pl.program_id(1)
    last = pl.num_programs(1) - 1
    mods = mods_ref[0, 0]

    def chunk_out(h2):
        hid = jnp.square(jnp.maximum(_dot(h2, w1_ref[0].astype(BF16)), 0.0))
        return _dot(hid.astype(BF16), w2_ref[0].astype(BF16))

    @pl.when(j == 0)
    def _():
        x = x_ref[...]
        ms = jnp.mean(x * x, axis=-1, keepdims=True)
        h = x * lax.rsqrt(ms + EPS) * g2_ref[0]
        h2 = (h * (1.0 + mods[4:5]) + mods[3:4]).astype(BF16)
        h2_ref[...] = h2
        acc_ref[...] = chunk_out(h2)

    @pl.when((j > 0) & (j < last))
    def _():
        acc_ref[...] += chunk_out(h2_ref[...])

    @pl.when(j == last)
    def _():
        y = x_ref[...] + mods[5:6] * (acc_ref[...] + chunk_out(h2_ref[...]))
        if final:
            ms = jnp.mean(y * y, axis=-1, keepdims=True)
            y = y * lax.rsqrt(ms + EPS) * fg_ref[...]
        o_ref[...] = y


def _mlp_call(x, mods, layer, cond_row0, p, final):
    n_tiles = x.shape[0] // TILE
    return pl.pallas_call(
        functools.partial(_mlp_kernel, final=final),
        grid=(n_tiles, D_FF // FF_CHUNK),
        in_specs=[
            pl.BlockSpec((TILE, D_MODEL), lambda g, j: (g, 0)),
            pl.BlockSpec((1, 1, N_MOD, D_MODEL), lambda g, j: (layer, cond_row0(g), 0, 0)),
            pl.BlockSpec((1, 1, D_MODEL), lambda g, j: (layer, 0, 0)),
            pl.BlockSpec((1, D_MODEL, FF_CHUNK), lambda g, j: (layer, 0, j)),
            pl.BlockSpec((1, FF_CHUNK, D_MODEL), lambda g, j: (layer, j, 0)),
            pl.BlockSpec((1, D_MODEL), lambda g, j: (0, 0)),
        ],
        out_specs=pl.BlockSpec((TILE, D_MODEL), lambda g, j: (g, 0)),
        out_shape=jax.ShapeDtypeStruct(x.shape, F32),
        scratch_shapes=[pltpu.VMEM((TILE, D_MODEL), BF16), pltpu.VMEM((TILE, D_MODEL), F32)],
        compiler_params=pltpu.CompilerParams(dimension_semantics=("arbitrary", "arbitrary"),
                                             vmem_limit_bytes=VMEM_LIMIT),
        name="mlp",
    )(x, mods, p['norm2_g'], p['mlp_w1'], p['mlp_w2'], p['final_g'])


def _dft_tables(n, scale):
    k = np.arange(n)
    ang = 2.0 * np.pi * ((k[:, None] * k[None, :]) % n) / n
    return np.cos(ang) * scale, np.sin(ang) * scale


def _seq_dft(seq_len):
    c, s = _dft_tables(seq_len, 1.0 / math.sqrt(seq_len))
    return jnp.asarray(np.stack([c, -s]), F32).astype(BF16)


def _channel_dft():
    c, s = _dft_tables(FOURIER_HEAD_DIM, 1.0 / math.sqrt(FOURIER_HEAD_DIM))
    eye = np.eye(N_FOURIER_HEADS)
    return jnp.asarray(np.concatenate([np.kron(eye, c), np.kron(eye, s)], axis=1), F32).astype(BF16)


def kernel(x_prompt, x_sample, state_rglru, c, c_ctx, norm1_g, norm2_g, final_g, w_mod, b_mod, w_in, conv_w, conv_b, lru_wr, lru_br, lru_wi, lru_bi, lru_lambda, pool_w, pool_scale, fourier_w, w_out, mlp_w1, mlp_w2):
    batch, seq, _ = x_prompt.shape
    dec_batch, dec_seq, _ = x_sample.shape
    assert TILE % seq == 0 and (batch * seq) % TILE == 0 and dec_seq == TILE

    p = {
        'norm1_g': norm1_g.reshape(DEPTH, 1, D_MODEL),
        'norm2_g': norm2_g.reshape(DEPTH, 1, D_MODEL),
        'final_g': final_g.reshape(1, D_MODEL),
        'w_in': w_in.astype(BF16),
        'conv_w': conv_w,
        'conv_b': conv_b.reshape(DEPTH, 1, D_RNN),
        'lru_wr': lru_wr,
        'lru_wi': lru_wi,
        'b_gate': jnp.stack([lru_br, lru_bi], axis=2).reshape(DEPTH, 4, D_RNN),
        'lam': lru_lambda,
        'pool_w': pool_w,
        'pool_scale': pool_scale.reshape(DEPTH, 1, D_POOL),
        'cdft': _channel_dft(),
        'fourier_w': fourier_w.astype(BF16),
        'w_out': w_out.astype(BF16),
        'mlp_w1': mlp_w1,
        'mlp_w2': mlp_w2,
    }

    cond = jnp.concatenate([c_ctx[None], c, jnp.zeros((MODS_ROWS - 1 - dec_batch, D_MODEL), F32)], axis=0)
    mods = _mods_call(cond, w_mod, b_mod).reshape(DEPTH, MODS_ROWS, N_MOD, D_MODEL)

    ctx_row = lambda g: 0
    lat_row = lambda g: 1 + g
    h0_ctx = jnp.zeros((1, 1, 2, D_RNN), F32)
    dft_ctx = _seq_dft(seq)
    dft_lat = _seq_dft(dec_seq)

    xc = x_prompt.reshape(batch * seq, D_MODEL)
    xs = x_sample.reshape(dec_batch * dec_seq, D_MODEL)
    states = []
    for l in range(DEPTH):
        final = l == DEPTH - 1
        xc, st = _mixer_call(xc, mods, l, ctx_row, p, h0_ctx, lambda g: (0, 0, 0, 0), dft_ctx, seq, True)
        states.append(st.reshape(batch, 2, D_RNN))
        xc = _mlp_call(xc, mods, l, ctx_row, p, final)
        (xs,) = _mixer_call(xs, mods, l, lat_row, p, state_rglru, lambda g, l=l: (g, l, 0, 0), dft_lat,
                            dec_seq, False)
        xs = _mlp_call(xs, mods, l, lat_row, p, final)

    y_prompt = xc.reshape(batch, seq, D_MODEL)
    y_sample = xs.reshape(dec_batch, dec_seq, D_MODEL)
    new_state = jnp.stack(states, axis=1)
    return (y_prompt, y_sample, new_state)
```

```python
import functools
import math

import numpy as np
import jax
import jax.numpy as jnp
from jax import lax
from jax.experimental import pallas as pl
from jax.experimental.pallas import tpu as pltpu

D_MODEL = 1024
DEPTH = 4
GRID_W = 64
D_RNN = 512
N_LRU_HEADS = 8
LRU_HEAD_DIM = 64
LRU_C = 8.0
CONV_WIDTH = 4
CONV_LEFT = 2
D_POOL = 256
D_FOURIER = 256
N_FOURIER_HEADS = 4
FOURIER_HEAD_DIM = 64
D_IN = 2 * D_RNN + D_POOL + D_FOURIER
D_FF = 4 * D_MODEL
N_MOD = 6
EPS = 1e-6

LANES = 128
SUBLANES = 8
TILE = 1024
CHUNK = TILE // SUBLANES
N_SLAB = D_RNN // LANES
SLAB_ROWS = TILE + SUBLANES
SCAN_GROUP = 4
SCAN_UNROLL = 4
HEADS_PER_BLOCK = 4
BLOCK = HEADS_PER_BLOCK * LRU_HEAD_DIM
FF_CHUNK = 1024
MODS_ROWS = 8
VMEM_LIMIT = 60 * 1024 * 1024

F32 = jnp.float32
BF16 = jnp.bfloat16


def _dot(a, b):
    return jnp.dot(a, b, preferred_element_type=F32)


def _row_iota(shape):
    return lax.broadcasted_iota(jnp.int32, shape, 0)


def _shift_rows(x, d):
    n = x.shape[0]
    return pltpu.roll(x, (-d) % n, axis=0)


def _mods_kernel(cond_ref, w_ref, b_ref, o_ref):
    s = cond_ref[...]
    s = s * jax.nn.sigmoid(s)
    o_ref[0] = _dot(s.astype(BF16), w_ref[0].astype(BF16)) + b_ref[0]


def _mods_call(cond, w_mod, b_mod):
    n_tiles = (N_MOD * D_MODEL) // D_MODEL
    return pl.pallas_call(
        _mods_kernel,
        grid=(DEPTH, n_tiles),
        in_specs=[
            pl.BlockSpec((MODS_ROWS, D_MODEL), lambda l, j: (0, 0)),
            pl.BlockSpec((1, D_MODEL, D_MODEL), lambda l, j: (l, 0, j)),
            pl.BlockSpec((1, 1, D_MODEL), lambda l, j: (l, 0, j)),
        ],
        out_specs=pl.BlockSpec((1, MODS_ROWS, D_MODEL), lambda l, j: (l, 0, j)),
        out_shape=jax.ShapeDtypeStruct((DEPTH, MODS_ROWS, N_MOD * D_MODEL), F32),
        compiler_params=pltpu.CompilerParams(dimension_semantics=("arbitrary", "arbitrary")),
        name="mods",
    )(cond, w_mod, b_mod.reshape(DEPTH, 1, N_MOD * D_MODEL))


def _window_sums(x, pos, length, unit):
    fwd = x
    bwd = jnp.where(pos >= 1, _shift_rows(x, -unit), 0.0)
    outs = [fwd + bwd]
    for k in (1, 2, 4):
        fwd = fwd + jnp.where(pos + k < length, _shift_rows(fwd, k * unit), 0.0)
        bwd = bwd + jnp.where(pos - k >= 0, _shift_rows(bwd, -k * unit), 0.0)
        outs.append(fwd + bwd)
    return outs


def _pool_axis(x, pos, length, unit):
    s1, s2, s4, s8 = _window_sums(x, pos, length, unit)
    lane = lax.broadcasted_iota(jnp.int32, x.shape, 1)
    sel = jnp.where(lane < 64, s1, jnp.where(lane < 128, s2, jnp.where(lane < 192, s4, s8)))
    half = jnp.where(lane < 64, 1, jnp.where(lane < 128, 2, jnp.where(lane < 192, 4, 8)))
    cnt = jnp.minimum(pos + half, length) - jnp.maximum(pos - half, 0)
    return sel, cnt.astype(F32)


def _mixer_kernel(x_ref, mods_ref, g1_ref, win_ref, cw_ref, cb_ref, wr_ref, wi_ref, bg_ref, lam_ref, h0_ref,
                  pw_ref, ps_ref, cdft_ref, fw_ref, dft_ref, wout_ref,
                  xo_ref, st_ref,
                  wg_ref, pwd_ref, u_ref, xc_ref, gate_ref, scan_ref, cat_ref, *, seq_len):
    n_seq = TILE // seq_len
    chunks_per_seq = seq_len // CHUNK
    mods = mods_ref[0, 0]

    @pl.when(pl.program_id(0) == 0)
    def _():
        wg_ref[...] = jnp.zeros_like(wg_ref)
        pwd_ref[...] = jnp.zeros_like(pwd_ref)
        for hd in range(HEADS_PER_BLOCK):
            rows = slice(hd * LRU_HEAD_DIM, (hd + 1) * LRU_HEAD_DIM)
            pwd_ref[rows, rows] = pw_ref[0, hd].astype(BF16)
            for d in range(2):
                for gate, w_ref in enumerate((wr_ref, wi_ref)):
                    for blk in range(2):
                        w = w_ref[0, d, blk * HEADS_PER_BLOCK + hd]
                        wg_ref[d * 4 + gate * 2 + blk, rows, rows] = (0.5 * w).astype(BF16)

    x = x_ref[...]
    ms = jnp.mean(x * x, axis=-1, keepdims=True)
    h = x * lax.rsqrt(ms + EPS) * g1_ref[0]
    h = h * (1.0 + mods[1:2]) + mods[0:1]
    u_ref[...] = _dot(h.astype(BF16), win_ref[0])

    xr = u_ref[:, 0:D_RNN]
    pos = _row_iota((TILE, D_RNN)) & (seq_len - 1)
    acc = cb_ref[0] + xr * cw_ref[0, CONV_LEFT:CONV_LEFT + 1, :]
    for k in range(CONV_WIDTH):
        d = k - CONV_LEFT
        if d == 0:
            continue
        valid = (pos + d >= 0) & (pos + d < seq_len)
        acc = acc + jnp.where(valid, _shift_rows(xr, d), 0.0) * cw_ref[0, k:k + 1, :]
    xc_ref[...] = acc

    xcb = xc_ref[...].astype(BF16)
    for d in range(2):
        for gate in range(2):
            for blk in range(2):
                pre = _dot(xcb[:, blk * BLOCK:(blk + 1) * BLOCK], wg_ref[d * 4 + gate * 2 + blk])
                gate_ref[gate * N_SLAB + 2 * blk, 0:TILE, :] = pre[:, 0:LANES]
                gate_ref[gate * N_SLAB + 2 * blk + 1, 0:TILE, :] = pre[:, LANES:2 * LANES]
        lam = lam_ref[0, d:d + 1, :]
        softplus_neg_lam = jnp.maximum(-lam, 0.0) + jnp.log1p(jnp.exp(-jnp.abs(lam)))
        half_rate = (-0.5 * LRU_C) * softplus_neg_lam
        hb_r = 0.5 * bg_ref[0, 2 * d:2 * d + 1, :]
        hb_i = 0.5 * bg_ref[0, 2 * d + 1:2 * d + 2, :]

        def gate_chunk(c, carry, d=d, half_rate=half_rate, hb_r=hb_r, hb_i=hb_i):
            r0 = c * CHUNK
            for s in range(N_SLAB):
                lanes = slice(s * LANES, (s + 1) * LANES)
                t_r = jnp.tanh(gate_ref[s, pl.ds(r0, CHUNK), :] + hb_r[:, lanes])
                t_i = jnp.tanh(gate_ref[N_SLAB + s, pl.ds(r0, CHUNK), :] + hb_i[:, lanes])
                rate = half_rate[:, lanes]
                a = jnp.exp(rate + rate * t_r)
                m2 = 1.0 - a * a
                root = jnp.where(m2 > 0.0, m2 * lax.rsqrt(m2), 0.0)
                b = root * ((0.5 + 0.5 * t_i) * xc_ref[pl.ds(r0, CHUNK), lanes])
                scan_ref[2 * d, s, pl.ds(c, CHUNK, stride=SUBLANES), :] = a
                scan_ref[2 * d + 1, s, pl.ds(c, CHUNK, stride=SUBLANES), :] = b
            return carry

        for c in range(SUBLANES):
            gate_chunk(c, 0)

    def load(arr, s, t):
        return scan_ref[arr, s, pl.ds(pl.multiple_of(t * SUBLANES, SUBLANES), SUBLANES), :]

    def compose(first, second):
        (a1, b1), (a2, b2) = first, second
        return a2 * a1, a2 * b1 + b2

    def group_steps(arr, s, t0, sign):
        return [(load(arr, s, t0 + sign * k), load(arr + 1, s, t0 + sign * k)) for k in range(SCAN_GROUP)]

    def pass1(i, carry):
        hf, pf, hb, pb = carry
        nhf, npf, nhb, npb = [], [], [], []
        for s in range(N_SLAB):
            f = group_steps(0, s, i * SCAN_GROUP, 1)
            af, bf = compose(compose(f[0], f[1]), compose(f[2], f[3]))
            nhf.append(af * hf[s] + bf)
            npf.append(af * pf[s])
            g = group_steps(2, s, CHUNK - 1 - i * SCAN_GROUP, -1)
            ab, bb = compose(compose(g[0], g[1]), compose(g[2], g[3]))
            nhb.append(ab * hb[s] + bb)
            npb.append(ab * pb[s])
        return tuple(nhf), tuple(npf), tuple(nhb), tuple(npb)

    zeros = tuple(jnp.zeros((SUBLANES, LANES), F32) for _ in range(N_SLAB))
    ones = tuple(jnp.ones((SUBLANES, LANES), F32) for _ in range(N_SLAB))
    ef, pf, eb, pb = lax.fori_loop(0, CHUNK // SCAN_GROUP, pass1, (zeros, ones, zeros, ones),
                                   unroll=SCAN_UNROLL)

    row = _row_iota((SUBLANES, LANES))
    seq_first = (row & (chunks_per_seq - 1)) == 0
    seq_last = (row & (chunks_per_seq - 1)) == chunks_per_seq - 1
    init_f, init_b = [], []
    for s in range(N_SLAB):
        h0f = jnp.broadcast_to(h0_ref[0, 0, 0:1, s * LANES:(s + 1) * LANES], (SUBLANES, LANES))
        h0b = jnp.broadcast_to(h0_ref[0, 0, 1:2, s * LANES:(s + 1) * LANES], (SUBLANES, LANES))
        cf, cb = h0f, h0b
        for _ in range(chunks_per_seq - 1):
            cf = jnp.where(seq_first, h0f, pltpu.roll(ef[s] + pf[s] * cf, 1, axis=0))
            cb = jnp.where(seq_last, h0b, pltpu.roll(eb[s] + pb[s] * cb, SUBLANES - 1, axis=0))
        init_f.append(cf)
        init_b.append(cb)

    def run_group(steps, h, out_slab, t0, sign):
        for k in range(0, SCAN_GROUP, 2):
            a0, b0 = steps[k]
            a01, b01 = compose(steps[k], steps[k + 1])
            h_even = a0 * h + b0
            h = a01 * h + b01
            for step, val in ((k, h_even), (k + 1, h)):
                t = t0 + sign * step
                gate_ref[out_slab, pl.ds(pl.multiple_of(t * SUBLANES, SUBLANES), SUBLANES), :] = val
        return h

    def pass2(i, carry):
        hf, hb = carry
        nhf, nhb = [], []
        for s in range(N_SLAB):
            tf0 = i * SCAN_GROUP
            tb0 = CHUNK - 1 - i * SCAN_GROUP
            nhf.append(run_group(group_steps(0, s, tf0, 1), hf[s], s, tf0, 1))
            nhb.append(run_group(group_steps(2, s, tb0, -1), hb[s], N_SLAB + s, tb0, -1))
        return tuple(nhf), tuple(nhb)

    lax.fori_loop(0, CHUNK // SCAN_GROUP, pass2, (tuple(init_f), tuple(init_b)), unroll=SCAN_UNROLL)

    if st_ref is not None:
        even = (_row_iota((SUBLANES, LANES)) & 1) == 0
        for s in range(N_SLAB):
            last_f = gate_ref[s, pl.ds((CHUNK - 1) * SUBLANES, SUBLANES), :]
            first_b = gate_ref[N_SLAB + s, pl.ds(0, SUBLANES), :]
            st_ref[0, :, s * LANES:(s + 1) * LANES] = jnp.where(
                even, pltpu.roll(last_f, SUBLANES - 1, axis=0), pltpu.roll(first_b, 1, axis=0))

    def out_chunk(c, carry):
        r0 = c * CHUNK
        for s in range(N_SLAB):
            hsum = (gate_ref[s, pl.ds(c, CHUNK, stride=SUBLANES), :]
                    + gate_ref[N_SLAB + s, pl.ds(c, CHUNK, stride=SUBLANES), :])
            xg = u_ref[pl.ds(r0, CHUNK), D_RNN + s * LANES:D_RNN + (s + 1) * LANES]
            cat_ref[pl.ds(r0, CHUNK), s * LANES:(s + 1) * LANES] = (hsum * jax.nn.gelu(xg)).astype(BF16)
        return carry

    for c in range(SUBLANES):
        out_chunk(c, 0)

    xp = u_ref[:, 2 * D_RNN:2 * D_RNN + D_POOL]
    tok = _row_iota((TILE, D_POOL))
    if seq_len == TILE:
        n_rows = TILE // GRID_W
        col_sum, col_cnt = _pool_axis(xp, tok & (GRID_W - 1), GRID_W, 1)
        win_sum, row_cnt = _pool_axis(col_sum, lax.shift_right_logical(tok, GRID_W.bit_length() - 1),
                                      n_rows, GRID_W)
        cnt = row_cnt * col_cnt
    else:
        win_sum, cnt = _pool_axis(xp, tok & (seq_len - 1), seq_len, 1)
    pooled = win_sum / cnt - xp
    cat_ref[:, D_RNN:D_RNN + D_POOL] = (_dot(pooled.astype(BF16), pwd_ref[...]) * ps_ref[0]).astype(BF16)

    xf = u_ref[:, 2 * D_RNN + D_POOL:D_IN].astype(BF16)
    y1 = _dot(xf, cdft_ref[...]).astype(BF16)
    for q in range(n_seq):
        rows = slice(q * seq_len, (q + 1) * seq_len)
        four = _dot(dft_ref[0], y1[rows, 0:D_FOURIER]) + _dot(dft_ref[1], y1[rows, D_FOURIER:])
        cat_ref[rows, D_RNN + D_POOL:] = _dot(four.astype(BF16), fw_ref[0]).astype(BF16)

    xo_ref[...] = x_ref[...] + mods[2:3] * _dot(cat_ref[...], wout_ref[0])


def _mixer_call(x, mods, layer, cond_row0, p, h0, h0_index, dft, seq_len, with_state):
    n_tiles = x.shape[0] // TILE
    n_seq = TILE // seq_len
    once = pl.Buffered(1)
    lyr = lambda *shape: pl.BlockSpec((1,) + shape, lambda g: (layer,) + (0,) * len(shape), pipeline_mode=once)
    in_specs = [
        pl.BlockSpec((TILE, D_MODEL), lambda g: (g, 0)),
        pl.BlockSpec((1, 1, N_MOD, D_MODEL), lambda g: (layer, cond_row0(g), 0, 0)),
        lyr(1, D_MODEL),
        lyr(D_MODEL, D_IN),
        lyr(CONV_WIDTH, D_RNN),
        lyr(1, D_RNN),
        lyr(2, N_LRU_HEADS, LRU_HEAD_DIM, LRU_HEAD_DIM),
        lyr(2, N_LRU_HEADS, LRU_HEAD_DIM, LRU_HEAD_DIM),
        lyr(4, D_RNN),
        lyr(2, D_RNN),
        pl.BlockSpec((1, 1, 2, D_RNN), h0_index),
        lyr(HEADS_PER_BLOCK, LRU_HEAD_DIM, LRU_HEAD_DIM),
        lyr(1, D_POOL),
        pl.BlockSpec((D_FOURIER, 2 * D_FOURIER), lambda g: (0, 0), pipeline_mode=once),
        lyr(D_FOURIER, D_FOURIER),
        pl.BlockSpec((2, seq_len, seq_len), lambda g: (0, 0, 0), pipeline_mode=once),
        lyr(D_MODEL, D_MODEL),
    ]
    out_specs = [pl.BlockSpec((TILE, D_MODEL), lambda g: (g, 0))]
    out_shape = [jax.ShapeDtypeStruct(x.shape, F32)]
    if with_state:
        out_specs.append(pl.BlockSpec((1, 2 * n_seq, D_RNN), lambda g: (g, 0, 0)))
        out_shape.append(jax.ShapeDtypeStruct((n_tiles, 2 * n_seq, D_RNN), F32))
        body = functools.partial(_mixer_kernel, seq_len=seq_len)
    else:
        def body(*refs):
            _mixer_kernel(*refs[:18], None, *refs[18:], seq_len=seq_len)
    return pl.pallas_call(
        body,
        grid=(n_tiles,),
        in_specs=in_specs,
        out_specs=out_specs,
        out_shape=out_shape,
        scratch_shapes=[
            pltpu.VMEM((8, BLOCK, BLOCK), BF16),
            pltpu.VMEM((D_POOL, D_POOL), BF16),
            pltpu.VMEM((TILE, D_IN), F32),
            pltpu.VMEM((TILE, D_RNN), F32),
            pltpu.VMEM((2 * N_SLAB, SLAB_ROWS, LANES), F32),
            pltpu.VMEM((4, N_SLAB, SLAB_ROWS, LANES), F32),
            pltpu.VMEM((TILE, D_MODEL), BF16),
        ],
        compiler_params=pltpu.CompilerParams(dimension_semantics=("arbitrary",),
                                             vmem_limit_bytes=VMEM_LIMIT),
        name="mixer_ctx" if with_state else "mixer_lat",
    )(x, mods, p['norm1_g'], p['w_in'], p['conv_w'], p['conv_b'], p['lru_wr'], p['lru_wi'], p['b_gate'], p['lam'], h0,
      p['pool_w'], p['pool_scale'], p['cdft'], p['fourier_w'], dft, p['w_out'])


def _mlp_kernel(x_ref, mods_ref, g2_ref, w1_ref, w2_ref, fg_ref, o_ref, h2_ref, acc_ref, *, final):
    j = pl.program_id(1)
    last = pl.num_programs(1) - 1
    mods = mods_ref[0, 0]

    def chunk_out(h2):
        hid = jnp.square(jnp.maximum(_dot(h2, w1_ref[0].astype(BF16)), 0.0))
        return _dot(hid.astype(BF16), w2_ref[0].astype(BF16))

    @pl.when(j == 0)
    def _():
        x = x_ref[...]
        ms = jnp.mean(x * x, axis=-1, keepdims=True)
        h = x * lax.rsqrt(ms + EPS) * g2_ref[0]
        h2 = (h * (1.0 + mods[4:5]) + mods[3:4]).astype(BF16)
        h2_ref[...] = h2
        acc_ref[...] = chunk_out(h2)

    @pl.when((j > 0) & (j < last))
    def _():
        acc_ref[...] += chunk_out(h2_ref[...])

    @pl.when(j == last)
    def _():
        y = x_ref[...] + mods[5:6] * (acc_ref[...] + chunk_out(h2_ref[...]))
        if final:
            ms = jnp.mean(y * y, axis=-1, keepdims=True)
            y = y * lax.rsqrt(ms + EPS) * fg_ref[...]
        o_ref[...] = y


def _mlp_call(x, mods, layer, cond_row0, p, final):
    n_tiles = x.shape[0] // TILE
    return pl.pallas_call(
        functools.partial(_mlp_kernel, final=final),
        grid=(n_tiles, D_FF // FF_CHUNK),
        in_specs=[
            pl.BlockSpec((TILE, D_MODEL), lambda g, j: (g, 0)),
            pl.BlockSpec((1, 1, N_MOD, D_MODEL), lambda g, j: (layer, cond_row0(g), 0, 0)),
            pl.BlockSpec((1, 1, D_MODEL), lambda g, j: (layer, 0, 0)),
            pl.BlockSpec((1, D_MODEL, FF_CHUNK), lambda g, j: (layer, 0, j)),
            pl.BlockSpec((1, FF_CHUNK, D_MODEL), lambda g, j: (layer, j, 0)),
            pl.BlockSpec((1, D_MODEL), lambda g, j: (0, 0)),
        ],
        out_specs=pl.BlockSpec((TILE, D_MODEL), lambda g, j: (g, 0)),
        out_shape=jax.ShapeDtypeStruct(x.shape, F32),
        scratch_shapes=[pltpu.VMEM((TILE, D_MODEL), BF16), pltpu.VMEM((TILE, D_MODEL), F32)],
        compiler_params=pltpu.CompilerParams(dimension_semantics=("arbitrary", "arbitrary"),
                                             vmem_limit_bytes=VMEM_LIMIT),
        name="mlp",
    )(x, mods, p['norm2_g'], p['mlp_w1'], p['mlp_w2'], p['final_g'])


def _dft_tables(n, scale):
    k = np.arange(n)
    ang = 2.0 * np.pi * ((k[:, None] * k[None, :]) % n) / n
    return np.cos(ang) * scale, np.sin(ang) * scale


def _seq_dft(seq_len):
    c, s = _dft_tables(seq_len, 1.0 / math.sqrt(seq_len))
    return jnp.asarray(np.stack([c, -s]), F32).astype(BF16)


def _channel_dft():
    c, s = _dft_tables(FOURIER_HEAD_DIM, 1.0 / math.sqrt(FOURIER_HEAD_DIM))
    eye = np.eye(N_FOURIER_HEADS)
    return jnp.asarray(np.concatenate([np.kron(eye, c), np.kron(eye, s)], axis=1), F32).astype(BF16)


def kernel(x_prompt, x_sample, state_rglru, c, c_ctx, norm1_g, norm2_g, final_g, w_mod, b_mod, w_in, conv_w, conv_b, lru_wr, lru_br, lru_wi, lru_bi, lru_lambda, pool_w, pool_scale, fourier_w, w_out, mlp_w1, mlp_w2):
    batch, seq, _ = x_prompt.shape
    dec_batch, dec_seq, _ = x_sample.shape
    assert TILE % seq == 0 and (batch * seq) % TILE == 0 and dec_seq == TILE

    p = {
        'norm1_g': norm1_g.reshape(DEPTH, 1, D_MODEL),
        'norm2_g': norm2_g.reshape(DEPTH, 1, D_MODEL),
        'final_g': final_g.reshape(1, D_MODEL),
        'w_in': w_in.astype(BF16),
        'conv_w': conv_w,
        'conv_b': conv_b.reshape(DEPTH, 1, D_RNN),
        'lru_wr': lru_wr,
        'lru_wi': lru_wi,
        'b_gate': jnp.stack([lru_br, lru_bi], axis=2).reshape(DEPTH, 4, D_RNN),
        'lam': lru_lambda,
        'pool_w': pool_w,
        'pool_scale': pool_scale.reshape(DEPTH, 1, D_POOL),
        'cdft': _channel_dft(),
        'fourier_w': fourier_w.astype(BF16),
        'w_out': w_out.astype(BF16),
        'mlp_w1': mlp_w1,
        'mlp_w2': mlp_w2,
    }

    cond = jnp.concatenate([c_ctx[None], c, jnp.zeros((MODS_ROWS - 1 - dec_batch, D_MODEL), F32)], axis=0)
    mods = _mods_call(cond, w_mod, b_mod).reshape(DEPTH, MODS_ROWS, N_MOD, D_MODEL)

    ctx_row = lambda g: 0
    lat_row = lambda g: 1 + g
    h0_ctx = jnp.zeros((1, 1, 2, D_RNN), F32)
    dft_ctx = _seq_dft(seq)
    dft_lat = _seq_dft(dec_seq)

    xc = x_prompt.reshape(batch * seq, D_MODEL)
    xs = x_sample.reshape(dec_batch * dec_seq, D_MODEL)
    states = []
    for l in range(DEPTH):
        final = l == DEPTH - 1
        xc, st = _mixer_call(xc, mods, l, ctx_row, p, h0_ctx, lambda g: (0, 0, 0, 0), dft_ctx, seq, True)
        states.append(st.reshape(batch, 2, D_RNN))
        xc = _mlp_call(xc, mods, l, ctx_row, p, final)
        (xs,) = _mixer_call(xs, mods, l, lat_row, p, state_rglru, lambda g, l=l: (g, l, 0, 0), dft_lat,
                            dec_seq, False)
        xs = _mlp_call(xs, mods, l, lat_row, p, final)

    y_prompt = xc.reshape(batch, seq, D_MODEL)
    y_sample = xs.reshape(dec_batch, dec_seq, D_MODEL)
    new_state = jnp.stack(states, axis=1)
    return (y_prompt, y_sample, new_state)
```

```python
import functools
import math

import numpy as np
import jax
import jax.numpy as jnp
from jax import lax
from jax.experimental import pallas as pl
from jax.experimental.pallas import tpu as pltpu

D_MODEL = 1024
DEPTH = 4
GRID_W = 64
D_RNN = 512
N_LRU_HEADS = 8
LRU_HEAD_DIM = 64
LRU_C = 8.0
CONV_WIDTH = 4
CONV_LEFT = 2
D_POOL = 256
D_FOURIER = 256
N_FOURIER_HEADS = 4
FOURIER_HEAD_DIM = 64
D_IN = 2 * D_RNN + D_POOL + D_FOURIER
D_FF = 4 * D_MODEL
N_MOD = 6
EPS = 1e-6

LANES = 128
SUBLANES = 8
TILE = 1024
CHUNK = TILE // SUBLANES
N_SLAB = D_RNN // LANES
SLAB_ROWS = TILE + SUBLANES
SCAN_GROUP = 4
HEADS_PER_BLOCK = 4
BLOCK = HEADS_PER_BLOCK * LRU_HEAD_DIM
FF_CHUNK = 1024
MODS_ROWS = 8
VMEM_LIMIT = 62 * 1024 * 1024

F32 = jnp.float32
BF16 = jnp.bfloat16


def _dot(a, b):
    return jnp.dot(a, b, preferred_element_type=F32)


def _row_iota(shape):
    return lax.broadcasted_iota(jnp.int32, shape, 0)


def _shift_rows(x, d):
    n = x.shape[0]
    return pltpu.roll(x, (-d) % n, axis=0)


def _rms_mod(x, gain, scale, shift):
    ms = jnp.mean(x * x, axis=-1, keepdims=True)
    return (x * lax.rsqrt(ms + EPS) * gain) * (1.0 + scale) + shift


def _mods_kernel(cond_ref, w_ref, b_ref, o_ref):
    s = cond_ref[...]
    s = s * jax.nn.sigmoid(s)
    o_ref[0] = _dot(s.astype(BF16), w_ref[0].astype(BF16)) + b_ref[0]


def _mods_call(cond, w_mod, b_mod):
    n_tiles = (N_MOD * D_MODEL) // D_MODEL
    return pl.pallas_call(
        _mods_kernel,
        grid=(DEPTH, n_tiles),
        in_specs=[
            pl.BlockSpec((MODS_ROWS, D_MODEL), lambda l, j: (0, 0)),
            pl.BlockSpec((1, D_MODEL, D_MODEL), lambda l, j: (l, 0, j)),
            pl.BlockSpec((1, 1, D_MODEL), lambda l, j: (l, 0, j)),
        ],
        out_specs=pl.BlockSpec((1, MODS_ROWS, D_MODEL), lambda l, j: (l, 0, j)),
        out_shape=jax.ShapeDtypeStruct((DEPTH, MODS_ROWS, N_MOD * D_MODEL), F32),
        compiler_params=pltpu.CompilerParams(dimension_semantics=("arbitrary", "arbitrary")),
        name="mods",
    )(cond, w_mod, b_mod.reshape(DEPTH, 1, N_MOD * D_MODEL))


def _window_sums(x, pos, length, unit):
    fwd = x
    bwd = jnp.where(pos >= 1, _shift_rows(x, -unit), 0.0)
    outs = [fwd + bwd]
    for k in (1, 2, 4):
        fwd = fwd + jnp.where(pos + k < length, _shift_rows(fwd, k * unit), 0.0)
        bwd = bwd + jnp.where(pos - k >= 0, _shift_rows(bwd, -k * unit), 0.0)
        outs.append(fwd + bwd)
    return outs


def _pool_axis(x, pos, length, unit):
    s1, s2, s4, s8 = _window_sums(x, pos, length, unit)
    lane = lax.broadcasted_iota(jnp.int32, x.shape, 1)
    sel = jnp.where(lane < 64, s1, jnp.where(lane < 128, s2, jnp.where(lane < 192, s4, s8)))
    half = jnp.where(lane < 64, 1, jnp.where(lane < 128, 2, jnp.where(lane < 192, 4, 8)))
    cnt = jnp.minimum(pos + half, length) - jnp.maximum(pos - half, 0)
    return sel, cnt.astype(F32)


def _pack_block_diagonal(wr_ref, wi_ref, pw_ref, wg_ref, pwd_ref):
    wg_ref[...] = jnp.zeros_like(wg_ref)
    pwd_ref[...] = jnp.zeros_like(pwd_ref)
    for hd in range(HEADS_PER_BLOCK):
        rows = slice(hd * LRU_HEAD_DIM, (hd + 1) * LRU_HEAD_DIM)
        pwd_ref[rows, rows] = pw_ref[0, hd].astype(BF16)
        for d in range(2):
            for gate, w_ref in enumerate((wr_ref, wi_ref)):
                for blk in range(2):
                    w = w_ref[0, d, blk * HEADS_PER_BLOCK + hd]
                    wg_ref[d * 4 + gate * 2 + blk, rows, rows] = (0.5 * w).astype(BF16)


def _normalize(x_ref, mods_ref, g1_ref, hn_ref):
    mods = mods_ref[0, 0]
    hn_ref[...] = _rms_mod(x_ref[...], g1_ref[0], mods[1:2], mods[0:1]).astype(BF16)


def _project(hn_ref, win_ref, u_ref, rows):
    u_ref[rows, :] = _dot(hn_ref[rows, :], win_ref[0])


def _mix_from_projection(u_ref, cw_ref, cb_ref, pwd_ref, ps_ref, cdft_ref, fw_ref, dft_ref,
                         xg_ref, xc_ref, cat_ref, *, seq_len):
    n_seq = TILE // seq_len

    xr = u_ref[:, 0:D_RNN]
    pos = _row_iota((TILE, D_RNN)) & (seq_len - 1)
    acc = cb_ref[0] + xr * cw_ref[0, CONV_LEFT:CONV_LEFT + 1, :]
    for k in range(CONV_WIDTH):
        d = k - CONV_LEFT
        if d == 0:
            continue
        valid = (pos + d >= 0) & (pos + d < seq_len)
        acc = acc + jnp.where(valid, _shift_rows(xr, d), 0.0) * cw_ref[0, k:k + 1, :]
    xc_ref[...] = acc
    xg_ref[...] = u_ref[:, D_RNN:2 * D_RNN]

    xp = u_ref[:, 2 * D_RNN:2 * D_RNN + D_POOL]
    tok = _row_iota((TILE, D_POOL))
    if seq_len == TILE:
        n_rows = TILE // GRID_W
        col_sum, col_cnt = _pool_axis(xp, tok & (GRID_W - 1), GRID_W, 1)
        win_sum, row_cnt = _pool_axis(col_sum, lax.shift_right_logical(tok, GRID_W.bit_length() - 1),
                                      n_rows, GRID_W)
        cnt = row_cnt * col_cnt
    else:
        win_sum, cnt = _pool_axis(xp, tok & (seq_len - 1), seq_len, 1)
    pooled = win_sum / cnt - xp
    cat_ref[:, D_RNN:D_RNN + D_POOL] = (_dot(pooled.astype(BF16), pwd_ref[...]) * ps_ref[0]).astype(BF16)

    xf = u_ref[:, 2 * D_RNN + D_POOL:D_IN].astype(BF16)
    y1 = _dot(xf, cdft_ref[...]).astype(BF16)
    for q in range(n_seq):
        rows = slice(q * seq_len, (q + 1) * seq_len)
        four = _dot(dft_ref[0], y1[rows, 0:D_FOURIER]) + _dot(dft_ref[1], y1[rows, D_FOURIER:])
        cat_ref[rows, D_RNN + D_POOL:] = _dot(four.astype(BF16), fw_ref[0]).astype(BF16)


def _tile_rows(t):
    return pl.ds(t * SUBLANES, SUBLANES)


def _compose(first, second):
    (a1, b1), (a2, b2) = first, second
    return a2 * a1, a2 * b1 + b2


def _rglru_and_output(mods_ref, bg_ref, lam_ref, h0_ref, wout_ref, o_ref, st_ref,
                      wg_ref, xg_ref, xc_ref, gate_ref, scan_ref, cat_ref, *, seq_len, after_gate_matmuls):
    chunks_per_seq = seq_len // CHUNK
    mods = mods_ref[0, 0]

    xcb = xc_ref[...].astype(BF16)
    for d in range(2):
        for gate in range(2):
            for blk in range(2):
                pre = _dot(xcb[:, blk * BLOCK:(blk + 1) * BLOCK], wg_ref[d * 4 + gate * 2 + blk])
                gate_ref[gate * N_SLAB + 2 * blk, 0:TILE, :] = pre[:, 0:LANES]
                gate_ref[gate * N_SLAB + 2 * blk + 1, 0:TILE, :] = pre[:, LANES:2 * LANES]
        after_gate_matmuls[d]()
        lam = lam_ref[0, d:d + 1, :]
        softplus_neg_lam = jnp.maximum(-lam, 0.0) + jnp.log1p(jnp.exp(-jnp.abs(lam)))
        half_rate = (-0.5 * LRU_C) * softplus_neg_lam
        hb_r = 0.5 * bg_ref[0, 2 * d:2 * d + 1, :]
        hb_i = 0.5 * bg_ref[0, 2 * d + 1:2 * d + 2, :]
        for c in range(SUBLANES):
            r0 = c * CHUNK
            for s in range(N_SLAB):
                lanes = slice(s * LANES, (s + 1) * LANES)
                t_r = jnp.tanh(gate_ref[s, pl.ds(r0, CHUNK), :] + hb_r[:, lanes])
                t_i = jnp.tanh(gate_ref[N_SLAB + s, pl.ds(r0, CHUNK), :] + hb_i[:, lanes])
                rate = half_rate[:, lanes]
                a = jnp.exp(rate + rate * t_r)
                m2 = 1.0 - a * a
                root = jnp.where(m2 > 0.0, m2 * lax.rsqrt(m2), 0.0)
                b = root * ((0.5 + 0.5 * t_i) * xc_ref[pl.ds(r0, CHUNK), lanes])
                scan_ref[2 * d, s, pl.ds(c, CHUNK, stride=SUBLANES), :] = a
                scan_ref[2 * d + 1, s, pl.ds(c, CHUNK, stride=SUBLANES), :] = b

    def group_steps(arr, s, t0, sign):
        return [(scan_ref[arr, s, _tile_rows(t0 + sign * k), :], scan_ref[arr + 1, s, _tile_rows(t0 + sign * k), :])
                for k in range(SCAN_GROUP)]

    zero = jnp.zeros((SUBLANES, LANES), F32)
    one = jnp.ones((SUBLANES, LANES), F32)
    ef, pf, eb, pb = [zero] * N_SLAB, [one] * N_SLAB, [zero] * N_SLAB, [one] * N_SLAB
    for i in range(CHUNK // SCAN_GROUP):
        for s in range(N_SLAB):
            f = group_steps(0, s, i * SCAN_GROUP, 1)
            a_f, b_f = _compose(_compose(f[0], f[1]), _compose(f[2], f[3]))
            ef[s], pf[s] = a_f * ef[s] + b_f, a_f * pf[s]
            g = group_steps(2, s, CHUNK - 1 - i * SCAN_GROUP, -1)
            a_b, b_b = _compose(_compose(g[0], g[1]), _compose(g[2], g[3]))
            eb[s], pb[s] = a_b * eb[s] + b_b, a_b * pb[s]

    row = _row_iota((SUBLANES, LANES))
    seq_first = (row & (chunks_per_seq - 1)) == 0
    seq_last = (row & (chunks_per_seq - 1)) == chunks_per_seq - 1
    hf, hb = [], []
    for s in range(N_SLAB):
        if h0_ref is None:
            h0f = h0b = zero
        else:
            h0f = jnp.broadcast_to(h0_ref[0, 0, 0:1, s * LANES:(s + 1) * LANES], (SUBLANES, LANES))
            h0b = jnp.broadcast_to(h0_ref[0, 0, 1:2, s * LANES:(s + 1) * LANES], (SUBLANES, LANES))
        cf, cb = h0f, h0b
        for _ in range(chunks_per_seq - 1):
            cf = jnp.where(seq_first, h0f, pltpu.roll(ef[s] + pf[s] * cf, 1, axis=0))
            cb = jnp.where(seq_last, h0b, pltpu.roll(eb[s] + pb[s] * cb, SUBLANES - 1, axis=0))
        hf.append(cf)
        hb.append(cb)

    def run_group(steps, h, out_slab, t0, sign):
        for k in range(0, SCAN_GROUP, 2):
            a0, b0 = steps[k]
            a01, b01 = _compose(steps[k], steps[k + 1])
            h_even = a0 * h + b0
            h = a01 * h + b01
            gate_ref[out_slab, _tile_rows(t0 + sign * k), :] = h_even
            gate_ref[out_slab, _tile_rows(t0 + sign * (k + 1)), :] = h
        return h

    for i in range(CHUNK // SCAN_GROUP):
        for s in range(N_SLAB):
            tf0 = i * SCAN_GROUP
            tb0 = CHUNK - 1 - i * SCAN_GROUP
            hf[s] = run_group(group_steps(0, s, tf0, 1), hf[s], s, tf0, 1)
            hb[s] = run_group(group_steps(2, s, tb0, -1), hb[s], N_SLAB + s, tb0, -1)

    if h0_ref is None:
        even = (row & 1) == 0
        for s in range(N_SLAB):
            last_f = gate_ref[s, _tile_rows(CHUNK - 1), :]
            first_b = gate_ref[N_SLAB + s, _tile_rows(0), :]
            st_ref[0, :, s * LANES:(s + 1) * LANES] = jnp.where(
                even, pltpu.roll(last_f, SUBLANES - 1, axis=0), pltpu.roll(first_b, 1, axis=0))
    else:
        st_ref[...] = jnp.zeros_like(st_ref)

    for c in range(SUBLANES):
        r0 = c * CHUNK
        for s in range(N_SLAB):
            lanes = slice(s * LANES, (s + 1) * LANES)
            hsum = (gate_ref[s, pl.ds(c, CHUNK, stride=SUBLANES), :]
                    + gate_ref[N_SLAB + s, pl.ds(c, CHUNK, stride=SUBLANES), :])
            cat_ref[pl.ds(r0, CHUNK), lanes] = (hsum * jax.nn.gelu(xg_ref[pl.ds(r0, CHUNK), lanes])).astype(BF16)

    o_ref[...] = (mods[2:3] * _dot(cat_ref[...], wout_ref[0])).astype(o_ref.dtype)


def _mixer_kernel(x_ref, mods_in_ref, mods_out_ref, g1_ref, win_ref, cw_ref, cb_ref, wr_ref, wi_ref, bg_ref,
                  lam_ref, h0_ref, pw_ref, ps_ref, cdft_ref, fw_ref, dftc_ref, dftl_ref, wout_ref,
                  o_ref, st_ref,
                  wg_ref, pwd_ref, u_ref, xg_ref, xc_ref, gate_ref, scan_ref, cat_ref, hn_ref, *, n_lat, n_tiles):
    step = pl.program_id(0)

    def project(rows):
        _project(hn_ref, win_ref, u_ref, rows)

    def mix(seq_len, dft_ref, h0, and_project):
        if and_project:
            _normalize(x_ref, mods_in_ref, g1_ref, hn_ref)
        _mix_from_projection(u_ref, cw_ref, cb_ref, pwd_ref, ps_ref, cdft_ref, fw_ref, dft_ref,
                             xg_ref, xc_ref, cat_ref, seq_len=seq_len)
        halves = [functools.partial(project, slice(k * TILE // 2, (k + 1) * TILE // 2)) for k in range(2)]
        _rglru_and_output(mods_out_ref, bg_ref, lam_ref, h0, wout_ref, o_ref, st_ref,
                          wg_ref, xg_ref, xc_ref, gate_ref, scan_ref, cat_ref, seq_len=seq_len,
                          after_gate_matmuls=halves if and_project else [lambda: None] * 2)

    @pl.when(step == 0)
    def _():
        _pack_block_diagonal(wr_ref, wi_ref, pw_ref, wg_ref, pwd_ref)
        _normalize(x_ref, mods_in_ref, g1_ref, hn_ref)
        project(slice(0, TILE))

    @pl.when((step >= 1) & (step <= n_lat))
    def _():
        mix(TILE, dftl_ref, h0_ref, True)

    @pl.when((step > n_lat) & (step < n_tiles))
    def _():
        mix(TILE // 4, dftc_ref, None, True)

    @pl.when(step == n_tiles)
    def _():
        mix(TILE // 4, dftc_ref, None, False)


def _mixer_call(x, mods, state, layer, p, n_ctx, n_lat):
    n_tiles = n_ctx + n_lat
    seq = TILE // 4

    def block_of(t):
        t = jnp.clip(t, 0, n_tiles - 1)
        return jnp.where(t < n_lat, t + n_ctx, t - n_lat)

    def cond_row(t):
        t = jnp.clip(t, 0, n_tiles - 1)
        return jnp.where(t < n_lat, t + 1, 0)

    once = pl.Buffered(1)
    lyr = lambda *shape: pl.BlockSpec((1,) + shape, lambda s: (layer,) + (0,) * len(shape), pipeline_mode=once)
    in_specs = [
        pl.BlockSpec((TILE, D_MODEL), lambda s: (block_of(s), 0)),
        pl.BlockSpec((1, 1, N_MOD, D_MODEL), lambda s: (layer, cond_row(s), 0, 0)),
        pl.BlockSpec((1, 1, N_MOD, D_MODEL), lambda s: (layer, cond_row(s - 1), 0, 0)),
        lyr(1, D_MODEL),
        lyr(D_MODEL, D_IN),
        lyr(CONV_WIDTH, D_RNN),
        lyr(1, D_RNN),
        lyr(2, N_LRU_HEADS, LRU_HEAD_DIM, LRU_HEAD_DIM),
        lyr(2, N_LRU_HEADS, LRU_HEAD_DIM, LRU_HEAD_DIM),
        lyr(4, D_RNN),
        lyr(2, D_RNN),
        pl.BlockSpec((1, 1, 2, D_RNN), lambda s: (jnp.clip(s - 1, 0, n_lat - 1), layer, 0, 0)),
        lyr(HEADS_PER_BLOCK, LRU_HEAD_DIM, LRU_HEAD_DIM),
        lyr(1, D_POOL),
        pl.BlockSpec((D_FOURIER, 2 * D_FOURIER), lambda s: (0, 0), pipeline_mode=once),
        lyr(D_FOURIER, D_FOURIER),
        pl.BlockSpec((2, seq, seq), lambda s: (0, 0, 0), pipeline_mode=once),
        pl.BlockSpec((2, TILE, TILE), lambda s: (0, 0, 0), pipeline_mode=once),
        lyr(D_MODEL, D_MODEL),
    ]
    return pl.pallas_call(
        functools.partial(_mixer_kernel, n_lat=n_lat, n_tiles=n_tiles),
        grid=(n_tiles + 1,),
        in_specs=in_specs,
        out_specs=[pl.BlockSpec((TILE, D_MODEL), lambda s: (block_of(s - 1), 0)),
                   pl.BlockSpec((1, 2 * (TILE // seq), D_RNN), lambda s: (block_of(s - 1), 0, 0))],
        out_shape=[jax.ShapeDtypeStruct(x.shape, BF16),
                   jax.ShapeDtypeStruct((n_tiles, 2 * (TILE // seq), D_RNN), F32)],
        scratch_shapes=[
            pltpu.VMEM((8, BLOCK, BLOCK), BF16),
            pltpu.VMEM((D_POOL, D_POOL), BF16),
            pltpu.VMEM((TILE, D_IN), F32),
            pltpu.VMEM((TILE, D_RNN), F32),
            pltpu.VMEM((TILE, D_RNN), F32),
            pltpu.VMEM((2 * N_SLAB, SLAB_ROWS, LANES), F32),
            pltpu.VMEM((4, N_SLAB, SLAB_ROWS, LANES), F32),
            pltpu.VMEM((TILE, D_MODEL), BF16),
            pltpu.VMEM((TILE, D_MODEL), BF16),
        ],
        compiler_params=pltpu.CompilerParams(dimension_semantics=("arbitrary",),
                                             vmem_limit_bytes=VMEM_LIMIT),
        name="mixer",
    )(x, mods, mods, p['norm1_g'], p['w_in'], p['conv_w'], p['conv_b'], p['lru_wr'], p['lru_wi'], p['b_gate'],
      p['lam'], state, p['pool_w'], p['pool_scale'], p['cdft'], p['fourier_w'], p['dft_ctx'], p['dft_lat'],
      p['w_out'])


def _mlp_kernel(x_ref, o_ref, mods_ref, g2_ref, w1_ref, w2_ref, fg_ref, y_ref, h2_ref, acc_ref, *, final):
    j = pl.program_id(1)
    last = pl.num_programs(1) - 1
    mods = mods_ref[0, 0]

    def chunk_out(h2):
        hid = jnp.square(jnp.maximum(_dot(h2, w1_ref[0].astype(BF16)), 0.0))
        return _dot(hid.astype(BF16), w2_ref[0].astype(BF16))

    @pl.when(j == 0)
    def _():
        h2 = _rms_mod(x_ref[...] + o_ref[...].astype(F32), g2_ref[0], mods[4:5], mods[3:4]).astype(BF16)
        h2_ref[...] = h2
        acc_ref[...] = chunk_out(h2)

    @pl.when((j > 0) & (j < last))
    def _():
        acc_ref[...] += chunk_out(h2_ref[...])

    @pl.when(j == last)
    def _():
        y = (x_ref[...] + o_ref[...].astype(F32)) + mods[5:6] * (acc_ref[...] + chunk_out(h2_ref[...]))
        if final:
            ms = jnp.mean(y * y, axis=-1, keepdims=True)
            y = y * lax.rsqrt(ms + EPS) * fg_ref[...]
        y_ref[...] = y


def _mlp_call(x, o, mods, layer, p, n_ctx, final):
    n_tiles = x.shape[0] // TILE
    cond_row = lambda g: jnp.maximum(g - (n_ctx - 1), 0)
    return pl.pallas_call(
        functools.partial(_mlp_kernel, final=final),
        grid=(n_tiles, D_FF // FF_CHUNK),
        in_specs=[
            pl.BlockSpec((TILE, D_MODEL), lambda g, j: (g, 0)),
            pl.BlockSpec((TILE, D_MODEL), lambda g, j: (g, 0)),
            pl.BlockSpec((1, 1, N_MOD, D_MODEL), lambda g, j: (layer, cond_row(g), 0, 0)),
            pl.BlockSpec((1, 1, D_MODEL), lambda g, j: (layer, 0, 0)),
            pl.BlockSpec((1, D_MODEL, FF_CHUNK), lambda g, j: (layer, 0, j)),
            pl.BlockSpec((1, FF_CHUNK, D_MODEL), lambda g, j: (layer, j, 0)),
            pl.BlockSpec((1, D_MODEL), lambda g, j: (0, 0)),
        ],
        out_specs=pl.BlockSpec((TILE, D_MODEL), lambda g, j: (g, 0)),
        out_shape=jax.ShapeDtypeStruct(x.shape, F32),
        scratch_shapes=[pltpu.VMEM((TILE, D_MODEL), BF16), pltpu.VMEM((TILE, D_MODEL), F32)],
        compiler_params=pltpu.CompilerParams(dimension_semantics=("arbitrary", "arbitrary"),
                                             vmem_limit_bytes=VMEM_LIMIT),
        name="mlp",
    )(x, o, mods, p['norm2_g'], p['mlp_w1'], p['mlp_w2'], p['final_g'])


def _dft_tables(n, scale):
    k = np.arange(n)
    ang = 2.0 * np.pi * ((k[:, None] * k[None, :]) % n) / n
    return np.cos(ang) * scale, np.sin(ang) * scale


def _seq_dft(seq_len):
    c, s = _dft_tables(seq_len, 1.0 / math.sqrt(seq_len))
    return jnp.asarray(np.stack([c, -s]), F32).astype(BF16)


def _channel_dft():
    c, s = _dft_tables(FOURIER_HEAD_DIM, 1.0 / math.sqrt(FOURIER_HEAD_DIM))
    eye = np.eye(N_FOURIER_HEADS)
    return jnp.asarray(np.concatenate([np.kron(eye, c), np.kron(eye, s)], axis=1), F32).astype(BF16)


def kernel(x_prompt, x_sample, state_rglru, c, c_ctx, norm1_g, norm2_g, final_g, w_mod, b_mod, w_in, conv_w, conv_b, lru_wr, lru_br, lru_wi, lru_bi, lru_lambda, pool_w, pool_scale, fourier_w, w_out, mlp_w1, mlp_w2):
    batch, seq, _ = x_prompt.shape
    dec_batch, dec_seq, _ = x_sample.shape
    assert seq == TILE // 4 and (batch * seq) % TILE == 0 and dec_seq == TILE
    n_ctx = batch * seq // TILE
    n_lat = dec_batch

    p = {
        'norm1_g': norm1_g.reshape(DEPTH, 1, D_MODEL),
        'norm2_g': norm2_g.reshape(DEPTH, 1, D_MODEL),
        'final_g': final_g.reshape(1, D_MODEL),
        'w_in': w_in.astype(BF16),
        'conv_w': conv_w,
        'conv_b': conv_b.reshape(DEPTH, 1, D_RNN),
        'lru_wr': lru_wr,
        'lru_wi': lru_wi,
        'b_gate': jnp.stack([lru_br, lru_bi], axis=2).reshape(DEPTH, 4, D_RNN),
        'lam': lru_lambda,
        'pool_w': pool_w,
        'pool_scale': pool_scale.reshape(DEPTH, 1, D_POOL),
        'cdft': _channel_dft(),
        'fourier_w': fourier_w.astype(BF16),
        'dft_ctx': _seq_dft(seq),
        'dft_lat': _seq_dft(dec_seq),
        'w_out': w_out.astype(BF16),
        'mlp_w1': mlp_w1,
        'mlp_w2': mlp_w2,
    }

    cond = jnp.concatenate([c_ctx[None], c, jnp.zeros((MODS_ROWS - 1 - dec_batch, D_MODEL), F32)], axis=0)
    mods = _mods_call(cond, w_mod, b_mod).reshape(DEPTH, MODS_ROWS, N_MOD, D_MODEL)

    x = jnp.concatenate([x_prompt.reshape(batch * seq, D_MODEL), x_sample.reshape(dec_batch * dec_seq, D_MODEL)])
    states = []
    for l in range(DEPTH):
        o, st = _mixer_call(x, mods, state_rglru, l, p, n_ctx, n_lat)
        states.append(st[:n_ctx].reshape(batch, 2, D_RNN))
        x = _mlp_call(x, o, mods, l, p, n_ctx, l == DEPTH - 1)

    y_prompt = x[:batch * seq].reshape(batch, seq, D_MODEL)
    y_sample = x[batch * seq:].reshape(dec_batch, dec_seq, D_MODEL)
    new_state = jnp.stack(states, axis=1)
    return (y_prompt, y_sample, new_state)
```

```python
import functools
import math

import numpy as np
import jax
import jax.numpy as jnp
from jax import lax
from jax.experimental import pallas as pl
from jax.experimental.pallas import tpu as pltpu

D_MODEL = 1024
DEPTH = 4
GRID_W = 64
D_RNN = 512
N_LRU_HEADS = 8
LRU_HEAD_DIM = 64
LRU_C = 8.0
CONV_WIDTH = 4
CONV_LEFT = 2
D_POOL = 256
D_FOURIER = 256
N_FOURIER_HEADS = 4
FOURIER_HEAD_DIM = 64
D_IN = 2 * D_RNN + D_POOL + D_FOURIER
D_FF = 4 * D_MODEL
N_MOD = 6
EPS = 1e-6

LANES = 128
SUBLANES = 8
TILE = 1024
CHUNK = TILE // SUBLANES
N_SLAB = D_RNN // LANES
SLAB_ROWS = TILE + SUBLANES
SCAN_GROUP = 4
HEADS_PER_BLOCK = 4
BLOCK = HEADS_PER_BLOCK * LRU_HEAD_DIM
FF_CHUNK = 1024
MODS_ROWS = 8
VMEM_LIMIT = 62 * 1024 * 1024

F32 = jnp.float32
BF16 = jnp.bfloat16


def _dot(a, b):
    return jnp.dot(a, b, preferred_element_type=F32)


def _row_iota(shape):
    return lax.broadcasted_iota(jnp.int32, shape, 0)


def _shift_rows(x, d):
    n = x.shape[0]
    return pltpu.roll(x, (-d) % n, axis=0)


def _rms_mod(x, gain, scale, shift):
    ms = jnp.mean(x * x, axis=-1, keepdims=True)
    return (x * lax.rsqrt(ms + EPS) * gain) * (1.0 + scale) + shift


def _mods_kernel(cond_ref, w_ref, b_ref, o_ref):
    s = cond_ref[...]
    s = s * jax.nn.sigmoid(s)
    o_ref[0] = _dot(s.astype(BF16), w_ref[0].astype(BF16)) + b_ref[0]


def _mods_call(cond, w_mod, b_mod):
    n_tiles = (N_MOD * D_MODEL) // D_MODEL
    return pl.pallas_call(
        _mods_kernel,
        grid=(DEPTH, n_tiles),
        in_specs=[
            pl.BlockSpec((MODS_ROWS, D_MODEL), lambda l, j: (0, 0)),
            pl.BlockSpec((1, D_MODEL, D_MODEL), lambda l, j: (l, 0, j)),
            pl.BlockSpec((1, 1, D_MODEL), lambda l, j: (l, 0, j)),
        ],
        out_specs=pl.BlockSpec((1, MODS_ROWS, D_MODEL), lambda l, j: (l, 0, j)),
        out_shape=jax.ShapeDtypeStruct((DEPTH, MODS_ROWS, N_MOD * D_MODEL), F32),
        compiler_params=pltpu.CompilerParams(dimension_semantics=("arbitrary", "arbitrary")),
        name="mods",
    )(cond, w_mod, b_mod.reshape(DEPTH, 1, N_MOD * D_MODEL))


def _window_sums(x, pos, length, unit):
    fwd = x
    bwd = jnp.where(pos >= 1, _shift_rows(x, -unit), 0.0)
    outs = [fwd + bwd]
    for k in (1, 2, 4):
        fwd = fwd + jnp.where(pos + k < length, _shift_rows(fwd, k * unit), 0.0)
        bwd = bwd + jnp.where(pos - k >= 0, _shift_rows(bwd, -k * unit), 0.0)
        outs.append(fwd + bwd)
    return outs


def _pool_axis(x, pos, length, unit):
    s1, s2, s4, s8 = _window_sums(x, pos, length, unit)
    lane = lax.broadcasted_iota(jnp.int32, x.shape, 1)
    sel = jnp.where(lane < 64, s1, jnp.where(lane < 128, s2, jnp.where(lane < 192, s4, s8)))
    half = jnp.where(lane < 64, 1, jnp.where(lane < 128, 2, jnp.where(lane < 192, 4, 8)))
    cnt = jnp.minimum(pos + half, length) - jnp.maximum(pos - half, 0)
    return sel, cnt.astype(F32)


def _pack_block_diagonal(wr_ref, wi_ref, pw_ref, wg_ref, pwd_ref):
    wg_ref[...] = jnp.zeros_like(wg_ref)
    pwd_ref[...] = jnp.zeros_like(pwd_ref)
    for hd in range(HEADS_PER_BLOCK):
        rows = slice(hd * LRU_HEAD_DIM, (hd + 1) * LRU_HEAD_DIM)
        pwd_ref[rows, rows] = pw_ref[0, hd].astype(BF16)
        for d in range(2):
            for gate, w_ref in enumerate((wr_ref, wi_ref)):
                for blk in range(2):
                    w = w_ref[0, d, blk * HEADS_PER_BLOCK + hd]
                    wg_ref[d * 4 + gate * 2 + blk, rows, rows] = (0.5 * w).astype(BF16)


def _normalize(x_ref, mods_ref, g1_ref, hn_ref):
    mods = mods_ref[0, 0]
    hn_ref[...] = _rms_mod(x_ref[...], g1_ref[0], mods[1:2], mods[0:1]).astype(BF16)


def _project(hn_ref, win_ref, u_ref, rows):
    u_ref[rows, :] = _dot(hn_ref[rows, :], win_ref[0])


def _mix_from_projection(u_ref, cw_ref, cb_ref, pwd_ref, ps_ref, cdft_ref, fw_ref, dft_ref,
                         xg_ref, xc_ref, cat_ref, *, seq_len):
    n_seq = TILE // seq_len

    xr = u_ref[:, 0:D_RNN]
    pos = _row_iota((TILE, D_RNN)) & (seq_len - 1)
    acc = cb_ref[0] + xr * cw_ref[0, CONV_LEFT:CONV_LEFT + 1, :]
    for k in range(CONV_WIDTH):
        d = k - CONV_LEFT
        if d == 0:
            continue
        valid = (pos + d >= 0) & (pos + d < seq_len)
        acc = acc + jnp.where(valid, _shift_rows(xr, d), 0.0) * cw_ref[0, k:k + 1, :]
    xc_ref[...] = acc
    xg_ref[...] = u_ref[:, D_RNN:2 * D_RNN]

    xp = u_ref[:, 2 * D_RNN:2 * D_RNN + D_POOL]
    tok = _row_iota((TILE, D_POOL))
    if seq_len == TILE:
        n_rows = TILE // GRID_W
        col_sum, col_cnt = _pool_axis(xp, tok & (GRID_W - 1), GRID_W, 1)
        win_sum, row_cnt = _pool_axis(col_sum, lax.shift_right_logical(tok, GRID_W.bit_length() - 1),
                                      n_rows, GRID_W)
        cnt = row_cnt * col_cnt
    else:
        win_sum, cnt = _pool_axis(xp, tok & (seq_len - 1), seq_len, 1)
    pooled = win_sum / cnt - xp
    cat_ref[:, D_RNN:D_RNN + D_POOL] = (_dot(pooled.astype(BF16), pwd_ref[...]) * ps_ref[0]).astype(BF16)

    xf = u_ref[:, 2 * D_RNN + D_POOL:D_IN].astype(BF16)
    y1 = _dot(xf, cdft_ref[...]).astype(BF16)
    for q in range(n_seq):
        rows = slice(q * seq_len, (q + 1) * seq_len)
        four = _dot(dft_ref[0], y1[rows, 0:D_FOURIER]) + _dot(dft_ref[1], y1[rows, D_FOURIER:])
        cat_ref[rows, D_RNN + D_POOL:] = _dot(four.astype(BF16), fw_ref[0]).astype(BF16)


def _tile_rows(t):
    return pl.ds(t * SUBLANES, SUBLANES)


def _compose(first, second):
    (a1, b1), (a2, b2) = first, second
    return a2 * a1, a2 * b1 + b2


def _rglru_and_output(mods_ref, bg_ref, lam_ref, h0_ref, wout_ref, o_ref, st_ref,
                      wg_ref, xg_ref, xc_ref, gate_ref, scan_ref, cat_ref, *, seq_len, after_gate_matmuls):
    chunks_per_seq = seq_len // CHUNK
    mods = mods_ref[0, 0]

    xcb = xc_ref[...].astype(BF16)
    for d in range(2):
        for gate in range(2):
            for blk in range(2):
                pre = _dot(xcb[:, blk * BLOCK:(blk + 1) * BLOCK], wg_ref[d * 4 + gate * 2 + blk])
                gate_ref[gate * N_SLAB + 2 * blk, 0:TILE, :] = pre[:, 0:LANES]
                gate_ref[gate * N_SLAB + 2 * blk + 1, 0:TILE, :] = pre[:, LANES:2 * LANES]
        after_gate_matmuls[d]()
        lam = lam_ref[0, d:d + 1, :]
        softplus_neg_lam = jnp.maximum(-lam, 0.0) + jnp.log1p(jnp.exp(-jnp.abs(lam)))
        half_rate = (-0.5 * LRU_C) * softplus_neg_lam
        hb_r = 0.5 * bg_ref[0, 2 * d:2 * d + 1, :]
        hb_i = 0.5 * bg_ref[0, 2 * d + 1:2 * d + 2, :]
        for c in range(SUBLANES):
            r0 = c * CHUNK
            for s in range(N_SLAB):
                lanes = slice(s * LANES, (s + 1) * LANES)
                t_r = jnp.tanh(gate_ref[s, pl.ds(r0, CHUNK), :] + hb_r[:, lanes])
                t_i = jnp.tanh(gate_ref[N_SLAB + s, pl.ds(r0, CHUNK), :] + hb_i[:, lanes])
                rate = half_rate[:, lanes]
                a = jnp.exp(rate + rate * t_r)
                m2 = 1.0 - a * a
                root = jnp.where(m2 > 0.0, m2 * lax.rsqrt(m2), 0.0)
                b = root * ((0.5 + 0.5 * t_i) * xc_ref[pl.ds(r0, CHUNK), lanes])
                scan_ref[2 * d, s, pl.ds(c, CHUNK, stride=SUBLANES), :] = a
                scan_ref[2 * d + 1, s, pl.ds(c, CHUNK, stride=SUBLANES), :] = b

    def group_steps(arr, s, t0, sign):
        return [(scan_ref[arr, s, _tile_rows(t0 + sign * k), :], scan_ref[arr + 1, s, _tile_rows(t0 + sign * k), :])
                for k in range(SCAN_GROUP)]

    zero = jnp.zeros((SUBLANES, LANES), F32)
    one = jnp.ones((SUBLANES, LANES), F32)
    ef, pf, eb, pb = [zero] * N_SLAB, [one] * N_SLAB, [zero] * N_SLAB, [one] * N_SLAB
    for i in range(CHUNK // SCAN_GROUP):
        for s in range(N_SLAB):
            f = group_steps(0, s, i * SCAN_GROUP, 1)
            a_f, b_f = _compose(_compose(f[0], f[1]), _compose(f[2], f[3]))
            ef[s], pf[s] = a_f * ef[s] + b_f, a_f * pf[s]
            g = group_steps(2, s, CHUNK - 1 - i * SCAN_GROUP, -1)
            a_b, b_b = _compose(_compose(g[0], g[1]), _compose(g[2], g[3]))
            eb[s], pb[s] = a_b * eb[s] + b_b, a_b * pb[s]

    row = _row_iota((SUBLANES, LANES))
    seq_first = (row & (chunks_per_seq - 1)) == 0
    seq_last = (row & (chunks_per_seq - 1)) == chunks_per_seq - 1
    hf, hb = [], []
    for s in range(N_SLAB):
        if h0_ref is None:
            h0f = h0b = zero
        else:
            h0f = jnp.broadcast_to(h0_ref[0, 0, 0:1, s * LANES:(s + 1) * LANES], (SUBLANES, LANES))
            h0b = jnp.broadcast_to(h0_ref[0, 0, 1:2, s * LANES:(s + 1) * LANES], (SUBLANES, LANES))
        cf, cb = h0f, h0b
        for _ in range(chunks_per_seq - 1):
            cf = jnp.where(seq_first, h0f, pltpu.roll(ef[s] + pf[s] * cf, 1, axis=0))
            cb = jnp.where(seq_last, h0b, pltpu.roll(eb[s] + pb[s] * cb, SUBLANES - 1, axis=0))
        hf.append(cf)
        hb.append(cb)

    def run_group(steps, h, out_slab, t0, sign):
        for k in range(0, SCAN_GROUP, 2):
            a0, b0 = steps[k]
            a01, b01 = _compose(steps[k], steps[k + 1])
            h_even = a0 * h + b0
            h = a01 * h + b01
            gate_ref[out_slab, _tile_rows(t0 + sign * k), :] = h_even
            gate_ref[out_slab, _tile_rows(t0 + sign * (k + 1)), :] = h
        return h

    for i in range(CHUNK // SCAN_GROUP):
        for s in range(N_SLAB):
            tf0 = i * SCAN_GROUP
            tb0 = CHUNK - 1 - i * SCAN_GROUP
            hf[s] = run_group(group_steps(0, s, tf0, 1), hf[s], s, tf0, 1)
            hb[s] = run_group(group_steps(2, s, tb0, -1), hb[s], N_SLAB + s, tb0, -1)

    if h0_ref is None:
        even = (row & 1) == 0
        for s in range(N_SLAB):
            last_f = gate_ref[s, _tile_rows(CHUNK - 1), :]
            first_b = gate_ref[N_SLAB + s, _tile_rows(0), :]
            st_ref[0, :, s * LANES:(s + 1) * LANES] = jnp.where(
                even, pltpu.roll(last_f, SUBLANES - 1, axis=0), pltpu.roll(first_b, 1, axis=0))
    else:
        st_ref[...] = jnp.zeros_like(st_ref)

    for c in range(SUBLANES):
        r0 = c * CHUNK
        for s in range(N_SLAB):
            lanes = slice(s * LANES, (s + 1) * LANES)
            hsum = (gate_ref[s, pl.ds(c, CHUNK, stride=SUBLANES), :]
                    + gate_ref[N_SLAB + s, pl.ds(c, CHUNK, stride=SUBLANES), :])
            cat_ref[pl.ds(r0, CHUNK), lanes] = (hsum * jax.nn.gelu(xg_ref[pl.ds(r0, CHUNK), lanes])).astype(BF16)

    o_ref[...] = (mods[2:3] * _dot(cat_ref[...], wout_ref[0])).astype(o_ref.dtype)


def _mixer_kernel(x_ref, mods_in_ref, mods_out_ref, g1_ref, win_ref, cw_ref, cb_ref, wr_ref, wi_ref, bg_ref,
                  lam_ref, h0_ref, pw_ref, ps_ref, cdft_ref, fw_ref, dft_ref, wout_ref,
                  o_ref, st_ref,
                  wg_ref, pwd_ref, u_ref, xg_ref, xc_ref, gate_ref, scan_ref, cat_ref, hn_ref, *, seq_len):
    step = pl.program_id(0)

    def project(rows):
        _project(hn_ref, win_ref, u_ref, rows)

    @pl.when(step == 0)
    def _():
        _pack_block_diagonal(wr_ref, wi_ref, pw_ref, wg_ref, pwd_ref)
        _normalize(x_ref, mods_in_ref, g1_ref, hn_ref)
        project(slice(0, TILE))

    @pl.when(step > 0)
    def _():
        _normalize(x_ref, mods_in_ref, g1_ref, hn_ref)

    @pl.when(step > 0)
    def _():
        _mix_from_projection(u_ref, cw_ref, cb_ref, pwd_ref, ps_ref, cdft_ref, fw_ref, dft_ref,
                             xg_ref, xc_ref, cat_ref, seq_len=seq_len)
        halves = [functools.partial(project, slice(k * TILE // 2, (k + 1) * TILE // 2)) for k in range(2)]
        _rglru_and_output(mods_out_ref, bg_ref, lam_ref, h0_ref if seq_len == TILE else None, wout_ref, o_ref,
                          st_ref, wg_ref, xg_ref, xc_ref, gate_ref, scan_ref, cat_ref, seq_len=seq_len,
                          after_gate_matmuls=halves)


def _mixer_call(x, mods, state, layer, p, seq_len):
    n_tiles = x.shape[0] // TILE
    latent = seq_len == TILE
    n_seq = TILE // seq_len
    tile_in = lambda s: jnp.minimum(s, n_tiles - 1)
    tile_out = lambda s: jnp.maximum(s - 1, 0)
    cond_row = (lambda t: 1 + t) if latent else (lambda t: 0)
    once = pl.Buffered(1)
    lyr = lambda *shape: pl.BlockSpec((1,) + shape, lambda s: (layer,) + (0,) * len(shape), pipeline_mode=once)
    in_specs = [
        pl.BlockSpec((TILE, D_MODEL), lambda s: (tile_in(s), 0)),
        pl.BlockSpec((1, 1, N_MOD, D_MODEL), lambda s: (layer, cond_row(tile_in(s)), 0, 0)),
        pl.BlockSpec((1, 1, N_MOD, D_MODEL), lambda s: (layer, cond_row(tile_out(s)), 0, 0)),
        lyr(1, D_MODEL),
        lyr(D_MODEL, D_IN),
        lyr(CONV_WIDTH, D_RNN),
        lyr(1, D_RNN),
        lyr(2, N_LRU_HEADS, LRU_HEAD_DIM, LRU_HEAD_DIM),
        lyr(2, N_LRU_HEADS, LRU_HEAD_DIM, LRU_HEAD_DIM),
        lyr(4, D_RNN),
        lyr(2, D_RNN),
        pl.BlockSpec((1, 1, 2, D_RNN), lambda s: (tile_out(s) if latent else 0, layer, 0, 0)),
        lyr(HEADS_PER_BLOCK, LRU_HEAD_DIM, LRU_HEAD_DIM),
        lyr(1, D_POOL),
        pl.BlockSpec((D_FOURIER, 2 * D_FOURIER), lambda s: (0, 0), pipeline_mode=once),
        lyr(D_FOURIER, D_FOURIER),
        pl.BlockSpec((2, seq_len, seq_len), lambda s: (0, 0, 0), pipeline_mode=once),
        lyr(D_MODEL, D_MODEL),
    ]
    return pl.pallas_call(
        functools.partial(_mixer_kernel, seq_len=seq_len),
        grid=(n_tiles + 1,),
        in_specs=in_specs,
        out_specs=[pl.BlockSpec((TILE, D_MODEL), lambda s: (tile_out(s), 0)),
                   pl.BlockSpec((1, 2 * n_seq, D_RNN), lambda s: (tile_out(s), 0, 0))],
        out_shape=[jax.ShapeDtypeStruct(x.shape, BF16),
                   jax.ShapeDtypeStruct((n_tiles, 2 * n_seq, D_RNN), F32)],
        scratch_shapes=[
            pltpu.VMEM((8, BLOCK, BLOCK), BF16),
            pltpu.VMEM((D_POOL, D_POOL), BF16),
            pltpu.VMEM((TILE, D_IN), F32),
            pltpu.VMEM((TILE, D_RNN), F32),
            pltpu.VMEM((TILE, D_RNN), F32),
            pltpu.VMEM((2 * N_SLAB, SLAB_ROWS, LANES), F32),
            pltpu.VMEM((4, N_SLAB, SLAB_ROWS, LANES), F32),
            pltpu.VMEM((TILE, D_MODEL), BF16),
            pltpu.VMEM((TILE, D_MODEL), BF16),
        ],
        compiler_params=pltpu.CompilerParams(dimension_semantics=("arbitrary",),
                                             vmem_limit_bytes=VMEM_LIMIT),
        name="mixer_lat" if latent else "mixer_ctx",
    )(x, mods, mods, p['norm1_g'], p['w_in'], p['conv_w'], p['conv_b'], p['lru_wr'], p['lru_wi'], p['b_gate'],
      p['lam'], state, p['pool_w'], p['pool_scale'], p['cdft'], p['fourier_w'],
      p['dft_lat'] if latent else p['dft_ctx'], p['w_out'])


def _mlp_kernel(x_ref, o_ref, mods_ref, g2_ref, w1_ref, w2_ref, fg_ref, y_ref, h2_ref, acc_ref, *, final):
    j = pl.program_id(1)
    last = pl.num_programs(1) - 1
    mods = mods_ref[0, 0]

    def chunk_out(h2):
        hid = jnp.square(jnp.maximum(_dot(h2, w1_ref[0].astype(BF16)), 0.0))
        return _dot(hid.astype(BF16), w2_ref[0].astype(BF16))

    @pl.when(j == 0)
    def _():
        h2 = _rms_mod(x_ref[...] + o_ref[...].astype(F32), g2_ref[0], mods[4:5], mods[3:4]).astype(BF16)
        h2_ref[...] = h2
        acc_ref[...] = chunk_out(h2)

    @pl.when((j > 0) & (j < last))
    def _():
        acc_ref[...] += chunk_out(h2_ref[...])

    @pl.when(j == last)
    def _():
        y = (x_ref[...] + o_ref[...].astype(F32)) + mods[5:6] * (acc_ref[...] + chunk_out(h2_ref[...]))
        if final:
            ms = jnp.mean(y * y, axis=-1, keepdims=True)
            y = y * lax.rsqrt(ms + EPS) * fg_ref[...]
        y_ref[...] = y


def _mlp_call(x, o, mods, layer, p, latent, final):
    n_tiles = x.shape[0] // TILE
    cond_row = (lambda g: 1 + g) if latent else (lambda g: 0)
    return pl.pallas_call(
        functools.partial(_mlp_kernel, final=final),
        grid=(n_tiles, D_FF // FF_CHUNK),
        in_specs=[
            pl.BlockSpec((TILE, D_MODEL), lambda g, j: (g, 0)),
            pl.BlockSpec((TILE, D_MODEL), lambda g, j: (g, 0)),
            pl.BlockSpec((1, 1, N_MOD, D_MODEL), lambda g, j: (layer, cond_row(g), 0, 0)),
            pl.BlockSpec((1, 1, D_MODEL), lambda g, j: (layer, 0, 0)),
            pl.BlockSpec((1, D_MODEL, FF_CHUNK), lambda g, j: (layer, 0, j)),
            pl.BlockSpec((1, FF_CHUNK, D_MODEL), lambda g, j: (layer, j, 0)),
            pl.BlockSpec((1, D_MODEL), lambda g, j: (0, 0)),
        ],
        out_specs=pl.BlockSpec((TILE, D_MODEL), lambda g, j: (g, 0)),
        out_shape=jax.ShapeDtypeStruct(x.shape, F32),
        scratch_shapes=[pltpu.VMEM((TILE, D_MODEL), BF16), pltpu.VMEM((TILE, D_MODEL), F32)],
        compiler_params=pltpu.CompilerParams(dimension_semantics=("arbitrary", "arbitrary"),
                                             vmem_limit_bytes=VMEM_LIMIT),
        name="mlp",
    )(x, o, mods, p['norm2_g'], p['mlp_w1'], p['mlp_w2'], p['final_g'])


def _dft_tables(n, scale):
    k = np.arange(n)
    ang = 2.0 * np.pi * ((k[:, None] * k[None, :]) % n) / n
    return np.cos(ang) * scale, np.sin(ang) * scale


def _seq_dft(seq_len):
    c, s = _dft_tables(seq_len, 1.0 / math.sqrt(seq_len))
    return jnp.asarray(np.stack([c, -s]), F32).astype(BF16)


def _channel_dft():
    c, s = _dft_tables(FOURIER_HEAD_DIM, 1.0 / math.sqrt(FOURIER_HEAD_DIM))
    eye = np.eye(N_FOURIER_HEADS)
    return jnp.asarray(np.concatenate([np.kron(eye, c), np.kron(eye, s)], axis=1), F32).astype(BF16)


def kernel(x_prompt, x_sample, state_rglru, c, c_ctx, norm1_g, norm2_g, final_g, w_mod, b_mod, w_in, conv_w, conv_b, lru_wr, lru_br, lru_wi, lru_bi, lru_lambda, pool_w, pool_scale, fourier_w, w_out, mlp_w1, mlp_w2):
    batch, seq, _ = x_prompt.shape
    dec_batch, dec_seq, _ = x_sample.shape
    assert seq == TILE // 4 and (batch * seq) % TILE == 0 and dec_seq == TILE

    p = {
        'norm1_g': norm1_g.reshape(DEPTH, 1, D_MODEL),
        'norm2_g': norm2_g.reshape(DEPTH, 1, D_MODEL),
        'final_g': final_g.reshape(1, D_MODEL),
        'w_in': w_in.astype(BF16),
        'conv_w': conv_w,
        'conv_b': conv_b.reshape(DEPTH, 1, D_RNN),
        'lru_wr': lru_wr,
        'lru_wi': lru_wi,
        'b_gate': jnp.stack([lru_br, lru_bi], axis=2).reshape(DEPTH, 4, D_RNN),
        'lam': lru_lambda,
        'pool_w': pool_w,
        'pool_scale': pool_scale.reshape(DEPTH, 1, D_POOL),
        'cdft': _channel_dft(),
        'fourier_w': fourier_w.astype(BF16),
        'dft_ctx': _seq_dft(seq),
        'dft_lat': _seq_dft(dec_seq),
        'w_out': w_out.astype(BF16),
        'mlp_w1': mlp_w1,
        'mlp_w2': mlp_w2,
    }

    cond = jnp.concatenate([c_ctx[None], c, jnp.zeros((MODS_ROWS - 1 - dec_batch, D_MODEL), F32)], axis=0)
    mods = _mods_call(cond, w_mod, b_mod).reshape(DEPTH, MODS_ROWS, N_MOD, D_MODEL)

    xc = x_prompt.reshape(batch * seq, D_MODEL)
    xs = x_sample.reshape(dec_batch * dec_seq, D_MODEL)
    states = []
    for l in range(DEPTH):
        final = l == DEPTH - 1
        o, st = _mixer_call(xc, mods, state_rglru, l, p, seq)
        states.append(st.reshape(batch, 2, D_RNN))
        xc = _mlp_call(xc, o, mods, l, p, False, final)
        o, _ = _mixer_call(xs, mods, state_rglru, l, p, dec_seq)
        xs = _mlp_call(xs, o, mods, l, p, True, final)

    y_prompt = xc.reshape(batch, seq, D_MODEL)
    y_sample = xs.reshape(dec_batch, dec_seq, D_MODEL)
    new_state = jnp.stack(states, axis=1)
    return (y_prompt, y_sample, new_state)
```

```python
import functools
import math

import numpy as np
import jax
import jax.numpy as jnp
from jax import lax
from jax.experimental import pallas as pl
from jax.experimental.pallas import tpu as pltpu

D_MODEL = 1024
DEPTH = 4
GRID_W = 64
D_RNN = 512
N_LRU_HEADS = 8
LRU_HEAD_DIM = 64
LRU_C = 8.0
CONV_WIDTH = 4
CONV_LEFT = 2
D_POOL = 256
D_FOURIER = 256
N_FOURIER_HEADS = 4
FOURIER_HEAD_DIM = 64
D_IN = 2 * D_RNN + D_POOL + D_FOURIER
D_FF = 4 * D_MODEL
N_MOD = 6
EPS = 1e-6

LANES = 128
SUBLANES = 8
TILE = 1024
CHUNK = TILE // SUBLANES
N_SLAB = D_RNN // LANES
SLAB_ROWS = TILE + SUBLANES
SCAN_GROUP = 4
SCAN_UNROLL = 4
HEADS_PER_BLOCK = 4
BLOCK = HEADS_PER_BLOCK * LRU_HEAD_DIM
FF_CHUNK = 1024
MODS_ROWS = 8
MODS_COLS = 2048
VMEM_LIMIT = 60 * 1024 * 1024

F32 = jnp.float32
BF16 = jnp.bfloat16


def _dot(a, b):
    return jnp.dot(a, b, preferred_element_type=F32)


def _row_iota(shape):
    return lax.broadcasted_iota(jnp.int32, shape, 0)


def _shift_rows(x, d):
    n = x.shape[0]
    return pltpu.roll(x, (-d) % n, axis=0)


def _mods_kernel(cond_ref, w_ref, b_ref, o_ref):
    s = cond_ref[...]
    s = s * jax.nn.sigmoid(s)
    o_ref[0] = _dot(s.astype(BF16), w_ref[0].astype(BF16)) + b_ref[0]


def _mods_call(cond, w_mod, b_mod):
    n_tiles = (N_MOD * D_MODEL) // MODS_COLS
    return pl.pallas_call(
        _mods_kernel,
        grid=(DEPTH, n_tiles),
        in_specs=[
            pl.BlockSpec((MODS_ROWS, D_MODEL), lambda l, j: (0, 0)),
            pl.BlockSpec((1, D_MODEL, MODS_COLS), lambda l, j: (l, 0, j)),
            pl.BlockSpec((1, 1, MODS_COLS), lambda l, j: (l, 0, j)),
        ],
        out_specs=pl.BlockSpec((1, MODS_ROWS, MODS_COLS), lambda l, j: (l, 0, j)),
        out_shape=jax.ShapeDtypeStruct((DEPTH, MODS_ROWS, N_MOD * D_MODEL), F32),
        compiler_params=pltpu.CompilerParams(dimension_semantics=("arbitrary", "arbitrary"),
                                             vmem_limit_bytes=VMEM_LIMIT),
        name="mods",
    )(cond, w_mod, b_mod.reshape(DEPTH, 1, N_MOD * D_MODEL))


def _window_sums(x, pos, length, unit):
    fwd = x
    bwd = jnp.where(pos >= 1, _shift_rows(x, -unit), 0.0)
    outs = [fwd + bwd]
    for k in (1, 2, 4):
        fwd = fwd + jnp.where(pos + k < length, _shift_rows(fwd, k * unit), 0.0)
        bwd = bwd + jnp.where(pos - k >= 0, _shift_rows(bwd, -k * unit), 0.0)
        outs.append(fwd + bwd)
    return outs


def _pool_axis(x, pos, length, unit):
    s1, s2, s4, s8 = _window_sums(x, pos, length, unit)
    lane = lax.broadcasted_iota(jnp.int32, x.shape, 1)
    sel = jnp.where(lane < 64, s1, jnp.where(lane < 128, s2, jnp.where(lane < 192, s4, s8)))
    half = jnp.where(lane < 64, 1, jnp.where(lane < 128, 2, jnp.where(lane < 192, 4, 8)))
    cnt = jnp.minimum(pos + half, length) - jnp.maximum(pos - half, 0)
    return sel, cnt.astype(F32)


def _mixer_kernel(x_ref, mods_ref, g1_ref, win_ref, cw_ref, cb_ref, wr_ref, wi_ref, bg_ref, lam_ref, h0_ref,
                  pw_ref, ps_ref, cdft_ref, fw_ref, dft_ref, wout_ref,
                  xo_ref, st_ref,
                  wg_ref, pwd_ref, u_ref, xc_ref, gate_ref, scan_ref, cat_ref, *, seq_len):
    n_seq = TILE // seq_len
    chunks_per_seq = seq_len // CHUNK
    mods = mods_ref[0, 0]

    @pl.when(pl.program_id(0) == 0)
    def _():
        wg_ref[...] = jnp.zeros_like(wg_ref)
        pwd_ref[...] = jnp.zeros_like(pwd_ref)
        for hd in range(HEADS_PER_BLOCK):
            rows = slice(hd * LRU_HEAD_DIM, (hd + 1) * LRU_HEAD_DIM)
            pwd_ref[rows, rows] = pw_ref[0, hd].astype(BF16)
            for d in range(2):
                for gate, w_ref in enumerate((wr_ref, wi_ref)):
                    for blk in range(2):
                        w = w_ref[0, d, blk * HEADS_PER_BLOCK + hd]
                        wg_ref[d * 4 + gate * 2 + blk, rows, rows] = (0.5 * w).astype(BF16)

    x = x_ref[...]
    ms = jnp.mean(x * x, axis=-1, keepdims=True)
    h = x * lax.rsqrt(ms + EPS) * g1_ref[0]
    h = h * (1.0 + mods[1:2]) + mods[0:1]
    u_ref[...] = _dot(h.astype(BF16), win_ref[0])

    xr = u_ref[:, 0:D_RNN]
    pos = _row_iota((TILE, D_RNN)) & (seq_len - 1)
    acc = cb_ref[0] + xr * cw_ref[0, CONV_LEFT:CONV_LEFT + 1, :]
    for k in range(CONV_WIDTH):
        d = k - CONV_LEFT
        if d == 0:
            continue
        valid = (pos + d >= 0) & (pos + d < seq_len)
        acc = acc + jnp.where(valid, _shift_rows(xr, d), 0.0) * cw_ref[0, k:k + 1, :]
    xc_ref[...] = acc

    xp = u_ref[:, 2 * D_RNN:2 * D_RNN + D_POOL]
    tok = _row_iota((TILE, D_POOL))
    if seq_len == TILE:
        n_rows = TILE // GRID_W
        col_sum, col_cnt = _pool_axis(xp, tok & (GRID_W - 1), GRID_W, 1)
        win_sum, row_cnt = _pool_axis(col_sum, lax.shift_right_logical(tok, GRID_W.bit_length() - 1),
                                      n_rows, GRID_W)
        cnt = row_cnt * col_cnt
    else:
        win_sum, cnt = _pool_axis(xp, tok & (seq_len - 1), seq_len, 1)
    pooled = win_sum / cnt - xp
    cat_ref[:, D_RNN:D_RNN + D_POOL] = (_dot(pooled.astype(BF16), pwd_ref[...]) * ps_ref[0]).astype(BF16)

    xf = u_ref[:, 2 * D_RNN + D_POOL:D_IN].astype(BF16)
    y1 = _dot(xf, cdft_ref[...]).astype(BF16)
    for q in range(n_seq):
        rows = slice(q * seq_len, (q + 1) * seq_len)
        four = _dot(dft_ref[0], y1[rows, 0:D_FOURIER]) + _dot(dft_ref[1], y1[rows, D_FOURIER:])
        cat_ref[rows, D_RNN + D_POOL:] = _dot(four.astype(BF16), fw_ref[0]).astype(BF16)

    xcb = xc_ref[...].astype(BF16)
    for d in range(2):
        for gate in range(2):
            for blk in range(2):
                pre = _dot(xcb[:, blk * BLOCK:(blk + 1) * BLOCK], wg_ref[d * 4 + gate * 2 + blk])
                gate_ref[gate * N_SLAB + 2 * blk, 0:TILE, :] = pre[:, 0:LANES]
                gate_ref[gate * N_SLAB + 2 * blk + 1, 0:TILE, :] = pre[:, LANES:2 * LANES]
        lam = lam_ref[0, d:d + 1, :]
        softplus_neg_lam = jnp.maximum(-lam, 0.0) + jnp.log1p(jnp.exp(-jnp.abs(lam)))
        half_rate = (-0.5 * LRU_C) * softplus_neg_lam
        hb_r = 0.5 * bg_ref[0, 2 * d:2 * d + 1, :]
        hb_i = 0.5 * bg_ref[0, 2 * d + 1:2 * d + 2, :]
        for c in range(SUBLANES):
            r0 = c * CHUNK
            for s in range(N_SLAB):
                lanes = slice(s * LANES, (s + 1) * LANES)
                t_r = jnp.tanh(gate_ref[s, pl.ds(r0, CHUNK), :] + hb_r[:, lanes])
                t_i = jnp.tanh(gate_ref[N_SLAB + s, pl.ds(r0, CHUNK), :] + hb_i[:, lanes])
                rate = half_rate[:, lanes]
                a = jnp.exp(rate + rate * t_r)
                m2 = 1.0 - a * a
                root = jnp.where(m2 > 0.0, m2 * lax.rsqrt(m2), 0.0)
                b = root * ((0.5 + 0.5 * t_i) * xc_ref[pl.ds(r0, CHUNK), lanes])
                scan_ref[2 * d, s, pl.ds(c, CHUNK, stride=SUBLANES), :] = a
                scan_ref[2 * d + 1, s, pl.ds(c, CHUNK, stride=SUBLANES), :] = b

    def load(arr, s, t):
        return scan_ref[arr, s, pl.ds(pl.multiple_of(t * SUBLANES, SUBLANES), SUBLANES), :]

    def compose(first, second):
        (a1, b1), (a2, b2) = first, second
        return a2 * a1, a2 * b1 + b2

    def group_steps(arr, s, t0, sign):
        return [(load(arr, s, t0 + sign * k), load(arr + 1, s, t0 + sign * k)) for k in range(SCAN_GROUP)]

    def pass1(i, carry):
        hf, pf, hb, pb = carry
        nhf, npf, nhb, npb = [], [], [], []
        for s in range(N_SLAB):
            f = group_steps(0, s, i * SCAN_GROUP, 1)
            af, bf = compose(compose(f[0], f[1]), compose(f[2], f[3]))
            nhf.append(af * hf[s] + bf)
            npf.append(af * pf[s])
            g = group_steps(2, s, CHUNK - 1 - i * SCAN_GROUP, -1)
            ab, bb = compose(compose(g[0], g[1]), compose(g[2], g[3]))
            nhb.append(ab * hb[s] + bb)
            npb.append(ab * pb[s])
        return tuple(nhf), tuple(npf), tuple(nhb), tuple(npb)

    zeros = tuple(jnp.zeros((SUBLANES, LANES), F32) for _ in range(N_SLAB))
    ones = tuple(jnp.ones((SUBLANES, LANES), F32) for _ in range(N_SLAB))
    ef, pf, eb, pb = lax.fori_loop(0, CHUNK // SCAN_GROUP, pass1, (zeros, ones, zeros, ones),
                                   unroll=SCAN_UNROLL)

    row = _row_iota((SUBLANES, LANES))
    seq_first = (row & (chunks_per_seq - 1)) == 0
    seq_last = (row & (chunks_per_seq - 1)) == chunks_per_seq - 1
    init_f, init_b = [], []
    for s in range(N_SLAB):
        h0f = jnp.broadcast_to(h0_ref[0, 0, 0:1, s * LANES:(s + 1) * LANES], (SUBLANES, LANES))
        h0b = jnp.broadcast_to(h0_ref[0, 0, 1:2, s * LANES:(s + 1) * LANES], (SUBLANES, LANES))
        cf, cb = h0f, h0b
        for _ in range(chunks_per_seq - 1):
            cf = jnp.where(seq_first, h0f, pltpu.roll(ef[s] + pf[s] * cf, 1, axis=0))
            cb = jnp.where(seq_last, h0b, pltpu.roll(eb[s] + pb[s] * cb, SUBLANES - 1, axis=0))
        init_f.append(cf)
        init_b.append(cb)

    def run_group(steps, h, out_slab, t0, sign):
        for k in range(0, SCAN_GROUP, 2):
            a0, b0 = steps[k]
            a01, b01 = compose(steps[k], steps[k + 1])
            h_even = a0 * h + b0
            h = a01 * h + b01
            for step, val in ((k, h_even), (k + 1, h)):
                t = t0 + sign * step
                gate_ref[out_slab, pl.ds(pl.multiple_of(t * SUBLANES, SUBLANES), SUBLANES), :] = val
        return h

    def pass2(i, carry):
        hf, hb = carry
        nhf, nhb = [], []
        for s in range(N_SLAB):
            tf0 = i * SCAN_GROUP
            tb0 = CHUNK - 1 - i * SCAN_GROUP
            nhf.append(run_group(group_steps(0, s, tf0, 1), hf[s], s, tf0, 1))
            nhb.append(run_group(group_steps(2, s, tb0, -1), hb[s], N_SLAB + s, tb0, -1))
        return tuple(nhf), tuple(nhb)

    lax.fori_loop(0, CHUNK // SCAN_GROUP, pass2, (tuple(init_f), tuple(init_b)), unroll=SCAN_UNROLL)

    if st_ref is not None:
        even = (_row_iota((SUBLANES, LANES)) & 1) == 0
        for s in range(N_SLAB):
            last_f = gate_ref[s, pl.ds((CHUNK - 1) * SUBLANES, SUBLANES), :]
            first_b = gate_ref[N_SLAB + s, pl.ds(0, SUBLANES), :]
            st_ref[0, :, s * LANES:(s + 1) * LANES] = jnp.where(
                even, pltpu.roll(last_f, SUBLANES - 1, axis=0), pltpu.roll(first_b, 1, axis=0))

    for c in range(SUBLANES):
        r0 = c * CHUNK
        for s in range(N_SLAB):
            hsum = (gate_ref[s, pl.ds(c, CHUNK, stride=SUBLANES), :]
                    + gate_ref[N_SLAB + s, pl.ds(c, CHUNK, stride=SUBLANES), :])
            xg = u_ref[pl.ds(r0, CHUNK), D_RNN + s * LANES:D_RNN + (s + 1) * LANES]
            cat_ref[pl.ds(r0, CHUNK), s * LANES:(s + 1) * LANES] = (hsum * jax.nn.gelu(xg)).astype(BF16)

    xo_ref[...] = x_ref[...] + mods[2:3] * _dot(cat_ref[...], wout_ref[0])


def _mixer_call(x, mods, layer, cond_row0, p, h0, h0_index, dft, seq_len, with_state):
    n_tiles = x.shape[0] // TILE
    n_seq = TILE // seq_len
    once = pl.Buffered(1)
    lyr = lambda *shape: pl.BlockSpec((1,) + shape, lambda g: (layer,) + (0,) * len(shape), pipeline_mode=once)
    in_specs = [
        pl.BlockSpec((TILE, D_MODEL), lambda g: (g, 0)),
        pl.BlockSpec((1, 1, N_MOD, D_MODEL), lambda g: (layer, cond_row0(g), 0, 0)),
        lyr(1, D_MODEL),
        lyr(D_MODEL, D_IN),
        lyr(CONV_WIDTH, D_RNN),
        lyr(1, D_RNN),
        lyr(2, N_LRU_HEADS, LRU_HEAD_DIM, LRU_HEAD_DIM),
        lyr(2, N_LRU_HEADS, LRU_HEAD_DIM, LRU_HEAD_DIM),
        lyr(4, D_RNN),
        lyr(2, D_RNN),
        pl.BlockSpec((1, 1, 2, D_RNN), h0_index),
        lyr(HEADS_PER_BLOCK, LRU_HEAD_DIM, LRU_HEAD_DIM),
        lyr(1, D_POOL),
        pl.BlockSpec((D_FOURIER, 2 * D_FOURIER), lambda g: (0, 0), pipeline_mode=once),
        lyr(D_FOURIER, D_FOURIER),
        pl.BlockSpec((2, seq_len, seq_len), lambda g: (0, 0, 0), pipeline_mode=once),
        lyr(D_MODEL, D_MODEL),
    ]
    out_specs = [pl.BlockSpec((TILE, D_MODEL), lambda g: (g, 0))]
    out_shape = [jax.ShapeDtypeStruct(x.shape, F32)]
    if with_state:
        out_specs.append(pl.BlockSpec((1, 2 * n_seq, D_RNN), lambda g: (g, 0, 0)))
        out_shape.append(jax.ShapeDtypeStruct((n_tiles, 2 * n_seq, D_RNN), F32))
        body = functools.partial(_mixer_kernel, seq_len=seq_len)
    else:
        def body(*refs):
            _mixer_kernel(*refs[:18], None, *refs[18:], seq_len=seq_len)
    return pl.pallas_call(
        body,
        grid=(n_tiles,),
        in_specs=in_specs,
        out_specs=out_specs,
        out_shape=out_shape,
        scratch_shapes=[
            pltpu.VMEM((8, BLOCK, BLOCK), BF16),
            pltpu.VMEM((D_POOL, D_POOL), BF16),
            pltpu.VMEM((TILE, D_IN), F32),
            pltpu.VMEM((TILE, D_RNN), F32),
            pltpu.VMEM((2 * N_SLAB, SLAB_ROWS, LANES), F32),
            pltpu.VMEM((4, N_SLAB, SLAB_ROWS, LANES), F32),
            pltpu.VMEM((TILE, D_MODEL), BF16),
        ],
        compiler_params=pltpu.CompilerParams(dimension_semantics=("arbitrary",),
                                             vmem_limit_bytes=VMEM_LIMIT),
        name="mixer_ctx" if with_state else "mixer_lat",
    )(x, mods, p['norm1_g'], p['w_in'], p['conv_w'], p['conv_b'], p['lru_wr'], p['lru_wi'], p['b_gate'], p['lam'], h0,
      p['pool_w'], p['pool_scale'], p['cdft'], p['fourier_w'], dft, p['w_out'])


def _mlp_kernel(x_ref, mods_ref, g2_ref, w1_ref, w2_ref, fg_ref, o_ref, h2_ref, acc_ref, *, final):
    j = pl.program_id(1)
    last = pl.num_programs(1) - 1
    mods = mods_ref[0, 0]

    def chunk_out(h2):
        hid = jnp.square(jnp.maximum(_dot(h2, w1_ref[0].astype(BF16)), 0.0))
        return _dot(hid.astype(BF16), w2_ref[0].astype(BF16))

    @pl.when(j == 0)
    def _():
        x = x_ref[...]
        ms = jnp.mean(x * x, axis=-1, keepdims=True)
        h = x * lax.rsqrt(ms + EPS) * g2_ref[0]
        h2 = (h * (1.0 + mods[4:5]) + mods[3:4]).astype(BF16)
        h2_ref[...] = h2
        acc_ref[...] = chunk_out(h2)

    @pl.when((j > 0) & (j < last))
    def _():
        acc_ref[...] += chunk_out(h2_ref[...])

    @pl.when(j == last)
    def _():
        y = x_ref[...] + mods[5:6] * (acc_ref[...] + chunk_out(h2_ref[...]))
        if final:
            ms = jnp.mean(y * y, axis=-1, keepdims=True)
            y = y * lax.rsqrt(ms + EPS) * fg_ref[...]
        o_ref[...] = y


def _mlp_call(x, mods, layer, cond_row0, p, final):
    n_tiles = x.shape[0] // TILE
    return pl.pallas_call(
        functools.partial(_mlp_kernel, final=final),
        grid=(n_tiles, D_FF // FF_CHUNK),
        in_specs=[
            pl.BlockSpec((TILE, D_MODEL), lambda g, j: (g, 0)),
            pl.BlockSpec((1, 1, N_MOD, D_MODEL), lambda g, j: (layer, cond_row0(g), 0, 0)),
            pl.BlockSpec((1, 1, D_MODEL), lambda g, j: (layer, 0, 0)),
            pl.BlockSpec((1, D_MODEL, FF_CHUNK), lambda g, j: (layer, 0, j)),
            pl.BlockSpec((1, FF_CHUNK, D_MODEL), lambda g, j: (layer, j, 0)),
            pl.BlockSpec((1, D_MODEL), lambda g, j: (0, 0)),
        ],
        out_specs=pl.BlockSpec((TILE, D_MODEL), lambda g, j: (g, 0)),
        out_shape=jax.ShapeDtypeStruct(x.shape, F32),
        scratch_shapes=[pltpu.VMEM((TILE, D_MODEL), BF16), pltpu.VMEM((TILE, D_MODEL), F32)],
        compiler_params=pltpu.CompilerParams(dimension_semantics=("arbitrary", "arbitrary"),
                                             vmem_limit_bytes=VMEM_LIMIT),
        name="mlp",
    )(x, mods, p['norm2_g'], p['mlp_w1'], p['mlp_w2'], p['final_g'])


def _dft_tables(n, scale):
    k = np.arange(n)
    ang = 2.0 * np.pi * ((k[:, None] * k[None, :]) % n) / n
    return np.cos(ang) * scale, np.sin(ang) * scale


def _seq_dft(seq_len):
    c, s = _dft_tables(seq_len, 1.0 / math.sqrt(seq_len))
    return jnp.asarray(np.stack([c, -s]), F32).astype(BF16)


def _channel_dft():
    c, s = _dft_tables(FOURIER_HEAD_DIM, 1.0 / math.sqrt(FOURIER_HEAD_DIM))
    eye = np.eye(N_FOURIER_HEADS)
    return jnp.asarray(np.concatenate([np.kron(eye, c), np.kron(eye, s)], axis=1), F32).astype(BF16)


def kernel(x_prompt, x_sample, state_rglru, c, c_ctx, norm1_g, norm2_g, final_g, w_mod, b_mod, w_in, conv_w, conv_b, lru_wr, lru_br, lru_wi, lru_bi, lru_lambda, pool_w, pool_scale, fourier_w, w_out, mlp_w1, mlp_w2):
    batch, seq, _ = x_prompt.shape
    dec_batch, dec_seq, _ = x_sample.shape
    assert TILE % seq == 0 and (batch * seq) % TILE == 0 and dec_seq == TILE

    p = {
        'norm1_g': norm1_g.reshape(DEPTH, 1, D_MODEL),
        'norm2_g': norm2_g.reshape(DEPTH, 1, D_MODEL),
        'final_g': final_g.reshape(1, D_MODEL),
        'w_in': w_in.astype(BF16),
        'conv_w': conv_w,
        'conv_b': conv_b.reshape(DEPTH, 1, D_RNN),
        'lru_wr': lru_wr,
        'lru_wi': lru_wi,
        'b_gate': jnp.stack([lru_br, lru_bi], axis=2).reshape(DEPTH, 4, D_RNN),
        'lam': lru_lambda,
        'pool_w': pool_w,
        'pool_scale': pool_scale.reshape(DEPTH, 1, D_POOL),
        'cdft': _channel_dft(),
        'fourier_w': fourier_w.astype(BF16),
        'w_out': w_out.astype(BF16),
        'mlp_w1': mlp_w1,
        'mlp_w2': mlp_w2,
    }

    cond = jnp.concatenate([c_ctx[None], c, jnp.zeros((MODS_ROWS - 1 - dec_batch, D_MODEL), F32)], axis=0)
    mods = _mods_call(cond, w_mod, b_mod).reshape(DEPTH, MODS_ROWS, N_MOD, D_MODEL)

    ctx_row = lambda g: 0
    lat_row = lambda g: 1 + g
    h0_ctx = jnp.zeros((1, 1, 2, D_RNN), F32)
    dft_ctx = _seq_dft(seq)
    dft_lat = _seq_dft(dec_seq)

    xc = x_prompt.reshape(batch * seq, D_MODEL)
    xs = x_sample.reshape(dec_batch * dec_seq, D_MODEL)
    states = []
    for l in range(DEPTH):
        final = l == DEPTH - 1
        xc, st = _mixer_call(xc, mods, l, ctx_row, p, h0_ctx, lambda g: (0, 0, 0, 0), dft_ctx, seq, True)
        states.append(st.reshape(batch, 2, D_RNN))
        xc = _mlp_call(xc, mods, l, ctx_row, p, final)
        (xs,) = _mixer_call(xs, mods, l, lat_row, p, state_rglru, lambda g, l=l: (g, l, 0, 0), dft_lat,
                            dec_seq, False)
        xs = _mlp_call(xs, mods, l, lat_row, p, final)

    y_prompt = xc.reshape(batch, seq, D_MODEL)
    y_sample = xs.reshape(dec_batch, dec_seq, D_MODEL)
    new_state = jnp.stack(states, axis=1)
    return (y_prompt, y_sample, new_state)
```

```python
import functools
import math

import numpy as np
import jax
import jax.numpy as jnp
from jax import lax
from jax.experimental import pallas as pl
from jax.experimental.pallas import tpu as pltpu

D_MODEL = 1024
DEPTH = 4
GRID_W = 64
D_RNN = 512
N_LRU_HEADS = 8
LRU_HEAD_DIM = 64
LRU_C = 8.0
CONV_WIDTH = 4
CONV_LEFT = 2
D_POOL = 256
D_FOURIER = 256
N_FOURIER_HEADS = 4
FOURIER_HEAD_DIM = 64
D_IN = 2 * D_RNN + D_POOL + D_FOURIER
D_FF = 4 * D_MODEL
N_MOD = 6
EPS = 1e-6

LANES = 128
SUBLANES = 8
TILE = 1024
CHUNK = TILE // SUBLANES
N_SLAB = D_RNN // LANES
SLAB_ROWS = TILE + SUBLANES
SCAN_GROUP = 4
SCAN_UNROLL = 4
HEADS_PER_BLOCK = 4
BLOCK = HEADS_PER_BLOCK * LRU_HEAD_DIM
PROJ_COLS = 512
FF_CHUNK = 1024
MODS_ROWS = 8
MODS_COLS = 2048
VMEM_LIMIT = 60 * 1024 * 1024

F32 = jnp.float32
BF16 = jnp.bfloat16


def _dot(a, b):
    return jnp.dot(a, b, preferred_element_type=F32)


def _row_iota(shape):
    return lax.broadcasted_iota(jnp.int32, shape, 0)


def _shift_rows(x, d):
    n = x.shape[0]
    return pltpu.roll(x, (-d) % n, axis=0)


def _mods_kernel(cond_ref, w_ref, b_ref, o_ref):
    s = cond_ref[...]
    s = s * jax.nn.sigmoid(s)
    o_ref[0] = _dot(s.astype(BF16), w_ref[0].astype(BF16)) + b_ref[0]


def _mods_call(cond, w_mod, b_mod):
    n_tiles = (N_MOD * D_MODEL) // MODS_COLS
    return pl.pallas_call(
        _mods_kernel,
        grid=(DEPTH, n_tiles),
        in_specs=[
            pl.BlockSpec((MODS_ROWS, D_MODEL), lambda l, j: (0, 0)),
            pl.BlockSpec((1, D_MODEL, MODS_COLS), lambda l, j: (l, 0, j)),
            pl.BlockSpec((1, 1, MODS_COLS), lambda l, j: (l, 0, j)),
        ],
        out_specs=pl.BlockSpec((1, MODS_ROWS, MODS_COLS), lambda l, j: (l, 0, j)),
        out_shape=jax.ShapeDtypeStruct((DEPTH, MODS_ROWS, N_MOD * D_MODEL), F32),
        compiler_params=pltpu.CompilerParams(dimension_semantics=("arbitrary", "arbitrary"),
                                             vmem_limit_bytes=VMEM_LIMIT),
        name="mods",
    )(cond, w_mod, b_mod.reshape(DEPTH, 1, N_MOD * D_MODEL))


def _window_sums(x, pos, length, unit):
    fwd = x
    bwd = jnp.where(pos >= 1, _shift_rows(x, -unit), 0.0)
    outs = [fwd + bwd]
    for k in (1, 2, 4):
        fwd = fwd + jnp.where(pos + k < length, _shift_rows(fwd, k * unit), 0.0)
        bwd = bwd + jnp.where(pos - k >= 0, _shift_rows(bwd, -k * unit), 0.0)
        outs.append(fwd + bwd)
    return outs


def _pool_axis(x, pos, length, unit):
    s1, s2, s4, s8 = _window_sums(x, pos, length, unit)
    lane = lax.broadcasted_iota(jnp.int32, x.shape, 1)
    sel = jnp.where(lane < 64, s1, jnp.where(lane < 128, s2, jnp.where(lane < 192, s4, s8)))
    half = jnp.where(lane < 64, 1, jnp.where(lane < 128, 2, jnp.where(lane < 192, 4, 8)))
    cnt = jnp.minimum(pos + half, length) - jnp.maximum(pos - half, 0)
    return sel, cnt.astype(F32)


def _mixer_kernel(x_ref, mods_ref, g1_ref, win_ref, cw_ref, cb_ref, wr_ref, wi_ref, bg_ref, lam_ref, h0_ref,
                  pw_ref, ps_ref, cdft_ref, fw_ref, dft_ref, wout_ref,
                  xo_ref, st_ref,
                  wg_ref, pwd_ref, u_ref, xc_ref, gate_ref, scan_ref, cat_ref, hn_ref, y1_ref, *, seq_len):
    n_seq = TILE // seq_len
    chunks_per_seq = seq_len // CHUNK
    mods = mods_ref[0, 0]

    @pl.when(pl.program_id(0) == 0)
    def _():
        wg_ref[...] = jnp.zeros_like(wg_ref)
        pwd_ref[...] = jnp.zeros_like(pwd_ref)
        for hd in range(HEADS_PER_BLOCK):
            rows = slice(hd * LRU_HEAD_DIM, (hd + 1) * LRU_HEAD_DIM)
            pwd_ref[rows, rows] = pw_ref[0, hd].astype(BF16)
            for d in range(2):
                for gate, w_ref in enumerate((wr_ref, wi_ref)):
                    for blk in range(2):
                        w = w_ref[0, d, blk * HEADS_PER_BLOCK + hd]
                        wg_ref[d * 4 + gate * 2 + blk, rows, rows] = (0.5 * w).astype(BF16)

    x = x_ref[...]
    ms = jnp.mean(x * x, axis=-1, keepdims=True)
    h = x * lax.rsqrt(ms + EPS) * g1_ref[0]
    hn_ref[...] = (h * (1.0 + mods[1:2]) + mods[0:1]).astype(BF16)

    def project(lo, hi):
        for c0 in range(lo, hi, PROJ_COLS):
            u_ref[:, c0:c0 + PROJ_COLS] = _dot(hn_ref[...], win_ref[0, :, c0:c0 + PROJ_COLS])

    project(0, D_RNN)
    project(2 * D_RNN, D_IN)

    xr = u_ref[:, 0:D_RNN]
    pos = _row_iota((TILE, D_RNN)) & (seq_len - 1)
    acc = cb_ref[0] + xr * cw_ref[0, CONV_LEFT:CONV_LEFT + 1, :]
    for k in range(CONV_WIDTH):
        d = k - CONV_LEFT
        if d == 0:
            continue
        valid = (pos + d >= 0) & (pos + d < seq_len)
        acc = acc + jnp.where(valid, _shift_rows(xr, d), 0.0) * cw_ref[0, k:k + 1, :]
    xc_ref[...] = acc


    def pooling_group():
        xp = u_ref[:, 2 * D_RNN:2 * D_RNN + D_POOL]
        tok = _row_iota((TILE, D_POOL))
        if seq_len == TILE:
            n_rows = TILE // GRID_W
            col_sum, col_cnt = _pool_axis(xp, tok & (GRID_W - 1), GRID_W, 1)
            win_sum, row_cnt = _pool_axis(col_sum, lax.shift_right_logical(tok, GRID_W.bit_length() - 1),
                                          n_rows, GRID_W)
            cnt = row_cnt * col_cnt
        else:
            win_sum, cnt = _pool_axis(xp, tok & (seq_len - 1), seq_len, 1)
        pooled = win_sum / cnt - xp
        cat_ref[:, D_RNN:D_RNN + D_POOL] = (_dot(pooled.astype(BF16), pwd_ref[...]) * ps_ref[0]).astype(BF16)

    def channel_dft():
        xf = u_ref[:, 2 * D_RNN + D_POOL:D_IN].astype(BF16)
        y1_ref[...] = _dot(xf, cdft_ref[...]).astype(BF16)

    def fourier_rows(q):
        rows = slice(q * seq_len, (q + 1) * seq_len)
        four = (_dot(dft_ref[0], y1_ref[rows, 0:D_FOURIER]) + _dot(dft_ref[1], y1_ref[rows, D_FOURIER:]))
        cat_ref[rows, D_RNN + D_POOL:] = _dot(four.astype(BF16), fw_ref[0]).astype(BF16)

    side_work = iter(
        [channel_dft, pooling_group, functools.partial(project, D_RNN, 2 * D_RNN)]
        + [functools.partial(fourier_rows, q) for q in range(n_seq)])

    xcb = xc_ref[...].astype(BF16)
    for d in range(2):
        for gate in range(2):
            for blk in range(2):
                pre = _dot(xcb[:, blk * BLOCK:(blk + 1) * BLOCK], wg_ref[d * 4 + gate * 2 + blk])
                gate_ref[gate * N_SLAB + 2 * blk, 0:TILE, :] = pre[:, 0:LANES]
                gate_ref[gate * N_SLAB + 2 * blk + 1, 0:TILE, :] = pre[:, LANES:2 * LANES]
        lam = lam_ref[0, d:d + 1, :]
        softplus_neg_lam = jnp.maximum(-lam, 0.0) + jnp.log1p(jnp.exp(-jnp.abs(lam)))
        half_rate = (-0.5 * LRU_C) * softplus_neg_lam
        hb_r = 0.5 * bg_ref[0, 2 * d:2 * d + 1, :]
        hb_i = 0.5 * bg_ref[0, 2 * d + 1:2 * d + 2, :]
        for c in range(SUBLANES):
            r0 = c * CHUNK
            for s in range(N_SLAB):
                lanes = slice(s * LANES, (s + 1) * LANES)
                t_r = jnp.tanh(gate_ref[s, pl.ds(r0, CHUNK), :] + hb_r[:, lanes])
                t_i = jnp.tanh(gate_ref[N_SLAB + s, pl.ds(r0, CHUNK), :] + hb_i[:, lanes])
                rate = half_rate[:, lanes]
                a = jnp.exp(rate + rate * t_r)
                m2 = 1.0 - a * a
                root = jnp.where(m2 > 0.0, m2 * lax.rsqrt(m2), 0.0)
                b = root * ((0.5 + 0.5 * t_i) * xc_ref[pl.ds(r0, CHUNK), lanes])
                scan_ref[2 * d, s, pl.ds(c, CHUNK, stride=SUBLANES), :] = a
                scan_ref[2 * d + 1, s, pl.ds(c, CHUNK, stride=SUBLANES), :] = b
            next(side_work, lambda: None)()
    for rest in side_work:
        rest()

    def load(arr, s, t):
        return scan_ref[arr, s, pl.ds(pl.multiple_of(t * SUBLANES, SUBLANES), SUBLANES), :]

    def compose(first, second):
        (a1, b1), (a2, b2) = first, second
        return a2 * a1, a2 * b1 + b2

    def group_steps(arr, s, t0, sign):
        return [(load(arr, s, t0 + sign * k), load(arr + 1, s, t0 + sign * k)) for k in range(SCAN_GROUP)]

    def pass1(i, carry):
        hf, pf, hb, pb = carry
        nhf, npf, nhb, npb = [], [], [], []
        for s in range(N_SLAB):
            f = group_steps(0, s, i * SCAN_GROUP, 1)
            af, bf = compose(compose(f[0], f[1]), compose(f[2], f[3]))
            nhf.append(af * hf[s] + bf)
            npf.append(af * pf[s])
            g = group_steps(2, s, CHUNK - 1 - i * SCAN_GROUP, -1)
            ab, bb = compose(compose(g[0], g[1]), compose(g[2], g[3]))
            nhb.append(ab * hb[s] + bb)
            npb.append(ab * pb[s])
        return tuple(nhf), tuple(npf), tuple(nhb), tuple(npb)

    zeros = tuple(jnp.zeros((SUBLANES, LANES), F32) for _ in range(N_SLAB))
    ones = tuple(jnp.ones((SUBLANES, LANES), F32) for _ in range(N_SLAB))
    ef, pf, eb, pb = lax.fori_loop(0, CHUNK // SCAN_GROUP, pass1, (zeros, ones, zeros, ones),
                                   unroll=SCAN_UNROLL)

    row = _row_iota((SUBLANES, LANES))
    seq_first = (row & (chunks_per_seq - 1)) == 0
    seq_last = (row & (chunks_per_seq - 1)) == chunks_per_seq - 1
    init_f, init_b = [], []
    for s in range(N_SLAB):
        h0f = jnp.broadcast_to(h0_ref[0, 0, 0:1, s * LANES:(s + 1) * LANES], (SUBLANES, LANES))
        h0b = jnp.broadcast_to(h0_ref[0, 0, 1:2, s * LANES:(s + 1) * LANES], (SUBLANES, LANES))
        cf, cb = h0f, h0b
        for _ in range(chunks_per_seq - 1):
            cf = jnp.where(seq_first, h0f, pltpu.roll(ef[s] + pf[s] * cf, 1, axis=0))
            cb = jnp.where(seq_last, h0b, pltpu.roll(eb[s] + pb[s] * cb, SUBLANES - 1, axis=0))
        init_f.append(cf)
        init_b.append(cb)

    def run_group(steps, h, out_slab, t0, sign):
        for k in range(0, SCAN_GROUP, 2):
            a0, b0 = steps[k]
            a01, b01 = compose(steps[k], steps[k + 1])
            h_even = a0 * h + b0
            h = a01 * h + b01
            for step, val in ((k, h_even), (k + 1, h)):
                t = t0 + sign * step
                gate_ref[out_slab, pl.ds(pl.multiple_of(t * SUBLANES, SUBLANES), SUBLANES), :] = val
        return h

    def pass2(i, carry):
        hf, hb = carry
        nhf, nhb = [], []
        for s in range(N_SLAB):
            tf0 = i * SCAN_GROUP
            tb0 = CHUNK - 1 - i * SCAN_GROUP
            nhf.append(run_group(group_steps(0, s, tf0, 1), hf[s], s, tf0, 1))
            nhb.append(run_group(group_steps(2, s, tb0, -1), hb[s], N_SLAB + s, tb0, -1))
        return tuple(nhf), tuple(nhb)

    lax.fori_loop(0, CHUNK // SCAN_GROUP, pass2, (tuple(init_f), tuple(init_b)), unroll=SCAN_UNROLL)

    if st_ref is not None:
        even = (_row_iota((SUBLANES, LANES)) & 1) == 0
        for s in range(N_SLAB):
            last_f = gate_ref[s, pl.ds((CHUNK - 1) * SUBLANES, SUBLANES), :]
            first_b = gate_ref[N_SLAB + s, pl.ds(0, SUBLANES), :]
            st_ref[0, :, s * LANES:(s + 1) * LANES] = jnp.where(
                even, pltpu.roll(last_f, SUBLANES - 1, axis=0), pltpu.roll(first_b, 1, axis=0))

    for c in range(SUBLANES):
        r0 = c * CHUNK
        for s in range(N_SLAB):
            hsum = (gate_ref[s, pl.ds(c, CHUNK, stride=SUBLANES), :]
                    + gate_ref[N_SLAB + s, pl.ds(c, CHUNK, stride=SUBLANES), :])
            xg = u_ref[pl.ds(r0, CHUNK), D_RNN + s * LANES:D_RNN + (s + 1) * LANES]
            cat_ref[pl.ds(r0, CHUNK), s * LANES:(s + 1) * LANES] = (hsum * jax.nn.gelu(xg)).astype(BF16)

    xo_ref[...] = x_ref[...] + mods[2:3] * _dot(cat_ref[...], wout_ref[0])


def _mixer_call(x, mods, layer, cond_row0, p, h0, h0_index, dft, seq_len, with_state):
    n_tiles = x.shape[0] // TILE
    n_seq = TILE // seq_len
    once = pl.Buffered(1)
    lyr = lambda *shape: pl.BlockSpec((1,) + shape, lambda g: (layer,) + (0,) * len(shape), pipeline_mode=once)
    in_specs = [
        pl.BlockSpec((TILE, D_MODEL), lambda g: (g, 0)),
        pl.BlockSpec((1, 1, N_MOD, D_MODEL), lambda g: (layer, cond_row0(g), 0, 0)),
        lyr(1, D_MODEL),
        lyr(D_MODEL, D_IN),
        lyr(CONV_WIDTH, D_RNN),
        lyr(1, D_RNN),
        lyr(2, N_LRU_HEADS, LRU_HEAD_DIM, LRU_HEAD_DIM),
        lyr(2, N_LRU_HEADS, LRU_HEAD_DIM, LRU_HEAD_DIM),
        lyr(4, D_RNN),
        lyr(2, D_RNN),
        pl.BlockSpec((1, 1, 2, D_RNN), h0_index),
        lyr(HEADS_PER_BLOCK, LRU_HEAD_DIM, LRU_HEAD_DIM),
        lyr(1, D_POOL),
        pl.BlockSpec((D_FOURIER, 2 * D_FOURIER), lambda g: (0, 0), pipeline_mode=once),
        lyr(D_FOURIER, D_FOURIER),
        pl.BlockSpec((2, seq_len, seq_len), lambda g: (0, 0, 0), pipeline_mode=once),
        lyr(D_MODEL, D_MODEL),
    ]
    out_specs = [pl.BlockSpec((TILE, D_MODEL), lambda g: (g, 0))]
    out_shape = [jax.ShapeDtypeStruct(x.shape, F32)]
    if with_state:
        out_specs.append(pl.BlockSpec((1, 2 * n_seq, D_RNN), lambda g: (g, 0, 0)))
        out_shape.append(jax.ShapeDtypeStruct((n_tiles, 2 * n_seq, D_RNN), F32))
        body = functools.partial(_mixer_kernel, seq_len=seq_len)
    else:
        def body(*refs):
            _mixer_kernel(*refs[:18], None, *refs[18:], seq_len=seq_len)
    return pl.pallas_call(
        body,
        grid=(n_tiles,),
        in_specs=in_specs,
        out_specs=out_specs,
        out_shape=out_shape,
        scratch_shapes=[
            pltpu.VMEM((8, BLOCK, BLOCK), BF16),
            pltpu.VMEM((D_POOL, D_POOL), BF16),
            pltpu.VMEM((TILE, D_IN), F32),
            pltpu.VMEM((TILE, D_RNN), F32),
            pltpu.VMEM((2 * N_SLAB, SLAB_ROWS, LANES), F32),
            pltpu.VMEM((4, N_SLAB, SLAB_ROWS, LANES), F32),
            pltpu.VMEM((TILE, D_MODEL), BF16),
            pltpu.VMEM((TILE, D_MODEL), BF16),
            pltpu.VMEM((TILE, 2 * D_FOURIER), BF16),
        ],
        compiler_params=pltpu.CompilerParams(dimension_semantics=("arbitrary",),
                                             vmem_limit_bytes=VMEM_LIMIT),
        name="mixer_ctx" if with_state else "mixer_lat",
    )(x, mods, p['norm1_g'], p['w_in'], p['conv_w'], p['conv_b'], p['lru_wr'], p['lru_wi'], p['b_gate'], p['lam'], h0,
      p['pool_w'], p['pool_scale'], p['cdft'], p['fourier_w'], dft, p['w_out'])


def _mlp_kernel(x_ref, mods_ref, g2_ref, w1_ref, w2_ref, fg_ref, o_ref, h2_ref, acc_ref, *, final):
    j = pl.program_id(1)
    last = pl.num_programs(1) - 1
    mods = mods_ref[0, 0]

    def chunk_out(h2):
        hid = jnp.square(jnp.maximum(_dot(h2, w1_ref[0].astype(BF16)), 0.0))
        return _dot(hid.astype(BF16), w2_ref[0].astype(BF16))

    @pl.when(j == 0)
    def _():
        x = x_ref[...]
        ms = jnp.mean(x * x, axis=-1, keepdims=True)
        h = x * lax.rsqrt(ms + EPS) * g2_ref[0]
        h2 = (h * (1.0 + mods[4:5]) + mods[3:4]).astype(BF16)
        h2_ref[...] = h2
        acc_ref[...] = chunk_out(h2)

    @pl.when((j > 0) & (j < last))
    def _():
        acc_ref[...] += chunk_out(h2_ref[...])

    @pl.when(j == last)
    def _():
        y = x_ref[...] + mods[5:6] * (acc_ref[...] + chunk_out(h2_ref[...]))
        if final:
            ms = jnp.mean(y * y, axis=-1, keepdims=True)
            y = y * lax.rsqrt(ms + EPS) * fg_ref[...]
        o_ref[...] = y


def _mlp_call(x, mods, layer, cond_row0, p, final):
    n_tiles = x.shape[0] // TILE
    return pl.pallas_call(
        functools.partial(_mlp_kernel, final=final),
        grid=(n_tiles, D_FF // FF_CHUNK),
        in_specs=[
            pl.BlockSpec((TILE, D_MODEL), lambda g, j: (g, 0)),
            pl.BlockSpec((1, 1, N_MOD, D_MODEL), lambda g, j: (layer, cond_row0(g), 0, 0)),
            pl.BlockSpec((1, 1, D_MODEL), lambda g, j: (layer, 0, 0)),
            pl.BlockSpec((1, D_MODEL, FF_CHUNK), lambda g, j: (layer, 0, j)),
            pl.BlockSpec((1, FF_CHUNK, D_MODEL), lambda g, j: (layer, j, 0)),
            pl.BlockSpec((1, D_MODEL), lambda g, j: (0, 0)),
        ],
        out_specs=pl.BlockSpec((TILE, D_MODEL), lambda g, j: (g, 0)),
        out_shape=jax.ShapeDtypeStruct(x.shape, F32),
        scratch_shapes=[pltpu.VMEM((TILE, D_MODEL), BF16), pltpu.VMEM((TILE, D_MODEL), F32)],
        compiler_params=pltpu.CompilerParams(dimension_semantics=("arbitrary", "arbitrary"),
                                             vmem_limit_bytes=VMEM_LIMIT),
        name="mlp",
    )(x, mods, p['norm2_g'], p['mlp_w1'], p['mlp_w2'], p['final_g'])


def _dft_tables(n, scale):
    k = np.arange(n)
    ang = 2.0 * np.pi * ((k[:, None] * k[None, :]) % n) / n
    return np.cos(ang) * scale, np.sin(ang) * scale


def _seq_dft(seq_len):
    c, s = _dft_tables(seq_len, 1.0 / math.sqrt(seq_len))
    return jnp.asarray(np.stack([c, -s]), F32).astype(BF16)


def _channel_dft():
    c, s = _dft_tables(FOURIER_HEAD_DIM, 1.0 / math.sqrt(FOURIER_HEAD_DIM))
    eye = np.eye(N_FOURIER_HEADS)
    return jnp.asarray(np.concatenate([np.kron(eye, c), np.kron(eye, s)], axis=1), F32).astype(BF16)


def kernel(x_prompt, x_sample, state_rglru, c, c_ctx, norm1_g, norm2_g, final_g, w_mod, b_mod, w_in, conv_w, conv_b, lru_wr, lru_br, lru_wi, lru_bi, lru_lambda, pool_w, pool_scale, fourier_w, w_out, mlp_w1, mlp_w2):
    batch, seq, _ = x_prompt.shape
    dec_batch, dec_seq, _ = x_sample.shape
    assert TILE % seq == 0 and (batch * seq) % TILE == 0 and dec_seq == TILE

    p = {
        'norm1_g': norm1_g.reshape(DEPTH, 1, D_MODEL),
        'norm2_g': norm2_g.reshape(DEPTH, 1, D_MODEL),
        'final_g': final_g.reshape(1, D_MODEL),
        'w_in': w_in.astype(BF16),
        'conv_w': conv_w,
        'conv_b': conv_b.reshape(DEPTH, 1, D_RNN),
        'lru_wr': lru_wr,
        'lru_wi': lru_wi,
        'b_gate': jnp.stack([lru_br, lru_bi], axis=2).reshape(DEPTH, 4, D_RNN),
        'lam': lru_lambda,
        'pool_w': pool_w,
        'pool_scale': pool_scale.reshape(DEPTH, 1, D_POOL),
        'cdft': _channel_dft(),
        'fourier_w': fourier_w.astype(BF16),
        'w_out': w_out.astype(BF16),
        'mlp_w1': mlp_w1,
        'mlp_w2': mlp_w2,
    }

    cond = jnp.concatenate([c_ctx[None], c, jnp.zeros((MODS_ROWS - 1 - dec_batch, D_MODEL), F32)], axis=0)
    mods = _mods_call(cond, w_mod, b_mod).reshape(DEPTH, MODS_ROWS, N_MOD, D_MODEL)

    ctx_row = lambda g: 0
    lat_row = lambda g: 1 + g
    h0_ctx = jnp.zeros((1, 1, 2, D_RNN), F32)
    dft_ctx = _seq_dft(seq)
    dft_lat = _seq_dft(dec_seq)

    xc = x_prompt.reshape(batch * seq, D_MODEL)
    xs = x_sample.reshape(dec_batch * dec_seq, D_MODEL)
    states = []
    for l in range(DEPTH):
        final = l == DEPTH - 1
        xc, st = _mixer_call(xc, mods, l, ctx_row, p, h0_ctx, lambda g: (0, 0, 0, 0), dft_ctx, seq, True)
        states.append(st.reshape(batch, 2, D_RNN))
        xc = _mlp_call(xc, mods, l, ctx_row, p, final)
        (xs,) = _mixer_call(xs, mods, l, lat_row, p, state_rglru, lambda g, l=l: (g, l, 0, 0), dft_lat,
                            dec_seq, False)
        xs = _mlp_call(xs, mods, l, lat_row, p, final)

    y_prompt = xc.reshape(batch, seq, D_MODEL)
    y_sample = xs.reshape(dec_batch, dec_seq, D_MODEL)
    new_state = jnp.stack(states, axis=1)
    return (y_prompt, y_sample, new_state)
```

```python
import functools
import math

import numpy as np
import jax
import jax.numpy as jnp
from jax import lax
from jax.experimental import pallas as pl
from jax.experimental.pallas import tpu as pltpu

D_MODEL = 1024
DEPTH = 4
GRID_W = 64
D_RNN = 512
N_LRU_HEADS = 8
LRU_HEAD_DIM = 64
LRU_C = 8.0
CONV_WIDTH = 4
CONV_LEFT = 2
D_POOL = 256
D_FOURIER = 256
N_FOURIER_HEADS = 4
FOURIER_HEAD_DIM = 64
D_IN = 2 * D_RNN + D_POOL + D_FOURIER
D_FF = 4 * D_MODEL
N_MOD = 6
EPS = 1e-6

LANES = 128
SUBLANES = 8
TILE = 1024
CHUNK = TILE // SUBLANES
N_SLAB = D_RNN // LANES
SLAB_ROWS = TILE + SUBLANES
SCAN_GROUP = 4
SCAN_UNROLL = 4
HEADS_PER_BLOCK = 4
BLOCK = HEADS_PER_BLOCK * LRU_HEAD_DIM
PROJ_COLS = 512
FF_CHUNK = 1024
MODS_ROWS = 8
MODS_COLS = 2048
VMEM_LIMIT = 60 * 1024 * 1024

F32 = jnp.float32
BF16 = jnp.bfloat16


def _dot(a, b):
    return jnp.dot(a, b, preferred_element_type=F32)


def _row_iota(shape):
    return lax.broadcasted_iota(jnp.int32, shape, 0)


def _shift_rows(x, d):
    n = x.shape[0]
    return pltpu.roll(x, (-d) % n, axis=0)


def _mods_kernel(cond_ref, w_ref, b_ref, o_ref):
    s = cond_ref[...]
    s = s * jax.nn.sigmoid(s)
    o_ref[0] = _dot(s.astype(BF16), w_ref[0].astype(BF16)) + b_ref[0]


def _mods_call(cond, w_mod, b_mod):
    n_tiles = (N_MOD * D_MODEL) // MODS_COLS
    return pl.pallas_call(
        _mods_kernel,
        grid=(DEPTH, n_tiles),
        in_specs=[
            pl.BlockSpec((MODS_ROWS, D_MODEL), lambda l, j: (0, 0)),
            pl.BlockSpec((1, D_MODEL, MODS_COLS), lambda l, j: (l, 0, j)),
            pl.BlockSpec((1, 1, MODS_COLS), lambda l, j: (l, 0, j)),
        ],
        out_specs=pl.BlockSpec((1, MODS_ROWS, MODS_COLS), lambda l, j: (l, 0, j)),
        out_shape=jax.ShapeDtypeStruct((DEPTH, MODS_ROWS, N_MOD * D_MODEL), F32),
        compiler_params=pltpu.CompilerParams(dimension_semantics=("arbitrary", "arbitrary"),
                                             vmem_limit_bytes=VMEM_LIMIT),
        name="mods",
    )(cond, w_mod, b_mod.reshape(DEPTH, 1, N_MOD * D_MODEL))


def _window_sums(x, pos, length, unit):
    fwd = x
    bwd = jnp.where(pos >= 1, _shift_rows(x, -unit), 0.0)
    outs = [fwd + bwd]
    for k in (1, 2, 4):
        fwd = fwd + jnp.where(pos + k < length, _shift_rows(fwd, k * unit), 0.0)
        bwd = bwd + jnp.where(pos - k >= 0, _shift_rows(bwd, -k * unit), 0.0)
        outs.append(fwd + bwd)
    return outs


def _pool_axis(x, pos, length, unit):
    s1, s2, s4, s8 = _window_sums(x, pos, length, unit)
    lane = lax.broadcasted_iota(jnp.int32, x.shape, 1)
    sel = jnp.where(lane < 64, s1, jnp.where(lane < 128, s2, jnp.where(lane < 192, s4, s8)))
    half = jnp.where(lane < 64, 1, jnp.where(lane < 128, 2, jnp.where(lane < 192, 4, 8)))
    cnt = jnp.minimum(pos + half, length) - jnp.maximum(pos - half, 0)
    return sel, cnt.astype(F32)


def _mixer_kernel(x_ref, mods_ref, g1_ref, win_ref, cw_ref, cb_ref, wr_ref, wi_ref, bg_ref, lam_ref, h0_ref,
                  pw_ref, ps_ref, cdft_ref, fw_ref, dft_ref, wout_ref,
                  xo_ref, st_ref,
                  wg_ref, pwd_ref, u_ref, xc_ref, gate_ref, scan_ref, cat_ref, hn_ref, y1_ref, *, seq_len):
    n_seq = TILE // seq_len
    chunks_per_seq = seq_len // CHUNK
    mods = mods_ref[0, 0]

    @pl.when(pl.program_id(0) == 0)
    def _():
        wg_ref[...] = jnp.zeros_like(wg_ref)
        pwd_ref[...] = jnp.zeros_like(pwd_ref)
        for hd in range(HEADS_PER_BLOCK):
            rows = slice(hd * LRU_HEAD_DIM, (hd + 1) * LRU_HEAD_DIM)
            pwd_ref[rows, rows] = pw_ref[0, hd].astype(BF16)
            for d in range(2):
                for gate, w_ref in enumerate((wr_ref, wi_ref)):
                    for blk in range(2):
                        w = w_ref[0, d, blk * HEADS_PER_BLOCK + hd]
                        cols = slice(gate * BLOCK + hd * LRU_HEAD_DIM, gate * BLOCK + (hd + 1) * LRU_HEAD_DIM)
                        wg_ref[d * 2 + blk, rows, cols] = (0.5 * w).astype(BF16)

    x = x_ref[...]
    ms = jnp.mean(x * x, axis=-1, keepdims=True)
    h = x * lax.rsqrt(ms + EPS) * g1_ref[0]
    hn_ref[...] = (h * (1.0 + mods[1:2]) + mods[0:1]).astype(BF16)

    def project(lo, hi):
        for c0 in range(lo, hi, PROJ_COLS):
            u_ref[:, c0:c0 + PROJ_COLS] = _dot(hn_ref[...], win_ref[0, :, c0:c0 + PROJ_COLS])

    project(0, D_RNN)
    project(2 * D_RNN, D_IN)

    xr = u_ref[:, 0:D_RNN]
    pos = _row_iota((TILE, D_RNN)) & (seq_len - 1)
    acc = cb_ref[0] + xr * cw_ref[0, CONV_LEFT:CONV_LEFT + 1, :]
    for k in range(CONV_WIDTH):
        d = k - CONV_LEFT
        if d == 0:
            continue
        valid = (pos + d >= 0) & (pos + d < seq_len)
        acc = acc + jnp.where(valid, _shift_rows(xr, d), 0.0) * cw_ref[0, k:k + 1, :]
    xc_ref[...] = acc


    def pooling_group():
        xp = u_ref[:, 2 * D_RNN:2 * D_RNN + D_POOL]
        tok = _row_iota((TILE, D_POOL))
        if seq_len == TILE:
            n_rows = TILE // GRID_W
            col_sum, col_cnt = _pool_axis(xp, tok & (GRID_W - 1), GRID_W, 1)
            win_sum, row_cnt = _pool_axis(col_sum, lax.shift_right_logical(tok, GRID_W.bit_length() - 1),
                                          n_rows, GRID_W)
            cnt = row_cnt * col_cnt
        else:
            win_sum, cnt = _pool_axis(xp, tok & (seq_len - 1), seq_len, 1)
        pooled = win_sum / cnt - xp
        cat_ref[:, D_RNN:D_RNN + D_POOL] = (_dot(pooled.astype(BF16), pwd_ref[...]) * ps_ref[0]).astype(BF16)

    def channel_dft():
        xf = u_ref[:, 2 * D_RNN + D_POOL:D_IN].astype(BF16)
        y1_ref[...] = _dot(xf, cdft_ref[...]).astype(BF16)

    seq_group = min(2, n_seq)

    def fourier_rows(q):
        rows = [slice((q + k) * seq_len, (q + k + 1) * seq_len) for k in range(seq_group)]
        y_cos = jnp.concatenate([y1_ref[r, 0:D_FOURIER] for r in rows], axis=1)
        y_sin = jnp.concatenate([y1_ref[r, D_FOURIER:] for r in rows], axis=1)
        four = (_dot(dft_ref[0], y_cos) + _dot(dft_ref[1], y_sin)).astype(BF16)
        for k, r in enumerate(rows):
            cat_ref[r, D_RNN + D_POOL:] = _dot(four[:, k * D_FOURIER:(k + 1) * D_FOURIER], fw_ref[0]).astype(BF16)

    side_work = iter(
        [channel_dft, pooling_group, functools.partial(project, D_RNN, 2 * D_RNN)]
        + [functools.partial(fourier_rows, q) for q in range(0, n_seq, seq_group)])

    xcb = xc_ref[...].astype(BF16)
    for d in range(2):
        for blk in range(2):
            pre = _dot(xcb[:, blk * BLOCK:(blk + 1) * BLOCK], wg_ref[d * 2 + blk])
            for gate in range(2):
                for half in range(2):
                    lanes = slice(gate * BLOCK + half * LANES, gate * BLOCK + (half + 1) * LANES)
                    gate_ref[gate * N_SLAB + 2 * blk + half, 0:TILE, :] = pre[:, lanes]
        lam = lam_ref[0, d:d + 1, :]
        softplus_neg_lam = jnp.maximum(-lam, 0.0) + jnp.log1p(jnp.exp(-jnp.abs(lam)))
        half_rate = (-0.5 * LRU_C) * softplus_neg_lam
        hb_r = 0.5 * bg_ref[0, 2 * d:2 * d + 1, :]
        hb_i = 0.5 * bg_ref[0, 2 * d + 1:2 * d + 2, :]
        for c in range(SUBLANES):
            r0 = c * CHUNK
            for s in range(N_SLAB):
                lanes = slice(s * LANES, (s + 1) * LANES)
                t_r = jnp.tanh(gate_ref[s, pl.ds(r0, CHUNK), :] + hb_r[:, lanes])
                t_i = jnp.tanh(gate_ref[N_SLAB + s, pl.ds(r0, CHUNK), :] + hb_i[:, lanes])
                rate = half_rate[:, lanes]
                a = jnp.exp(rate + rate * t_r)
                m2 = 1.0 - a * a
                root = jnp.where(m2 > 0.0, m2 * lax.rsqrt(m2), 0.0)
                b = root * ((0.5 + 0.5 * t_i) * xc_ref[pl.ds(r0, CHUNK), lanes])
                scan_ref[2 * d, s, pl.ds(c, CHUNK, stride=SUBLANES), :] = a
                scan_ref[2 * d + 1, s, pl.ds(c, CHUNK, stride=SUBLANES), :] = b
            next(side_work, lambda: None)()
    for rest in side_work:
        rest()

    def load(arr, s, t):
        return scan_ref[arr, s, pl.ds(pl.multiple_of(t * SUBLANES, SUBLANES), SUBLANES), :]

    def compose(first, second):
        (a1, b1), (a2, b2) = first, second
        return a2 * a1, a2 * b1 + b2

    def group_steps(arr, s, t0, sign):
        return [(load(arr, s, t0 + sign * k), load(arr + 1, s, t0 + sign * k)) for k in range(SCAN_GROUP)]

    def pass1(i, carry):
        hf, pf, hb, pb = carry
        nhf, npf, nhb, npb = [], [], [], []
        for s in range(N_SLAB):
            f = group_steps(0, s, i * SCAN_GROUP, 1)
            af, bf = compose(compose(f[0], f[1]), compose(f[2], f[3]))
            nhf.append(af * hf[s] + bf)
            npf.append(af * pf[s])
            g = group_steps(2, s, CHUNK - 1 - i * SCAN_GROUP, -1)
            ab, bb = compose(compose(g[0], g[1]), compose(g[2], g[3]))
            nhb.append(ab * hb[s] + bb)
            npb.append(ab * pb[s])
        return tuple(nhf), tuple(npf), tuple(nhb), tuple(npb)

    zeros = tuple(jnp.zeros((SUBLANES, LANES), F32) for _ in range(N_SLAB))
    ones = tuple(jnp.ones((SUBLANES, LANES), F32) for _ in range(N_SLAB))
    ef, pf, eb, pb = lax.fori_loop(0, CHUNK // SCAN_GROUP, pass1, (zeros, ones, zeros, ones),
                                   unroll=SCAN_UNROLL)

    row = _row_iota((SUBLANES, LANES))
    seq_first = (row & (chunks_per_seq - 1)) == 0
    seq_last = (row & (chunks_per_seq - 1)) == chunks_per_seq - 1
    init_f, init_b = [], []
    for s in range(N_SLAB):
        h0f = jnp.broadcast_to(h0_ref[0, 0, 0:1, s * LANES:(s + 1) * LANES], (SUBLANES, LANES))
        h0b = jnp.broadcast_to(h0_ref[0, 0, 1:2, s * LANES:(s + 1) * LANES], (SUBLANES, LANES))
        cf, cb = h0f, h0b
        for _ in range(chunks_per_seq - 1):
            cf = jnp.where(seq_first, h0f, pltpu.roll(ef[s] + pf[s] * cf, 1, axis=0))
            cb = jnp.where(seq_last, h0b, pltpu.roll(eb[s] + pb[s] * cb, SUBLANES - 1, axis=0))
        init_f.append(cf)
        init_b.append(cb)

    def run_group(steps, h, out_slab, t0, sign):
        for k in range(0, SCAN_GROUP, 2):
            a0, b0 = steps[k]
            a01, b01 = compose(steps[k], steps[k + 1])
            h_even = a0 * h + b0
            h = a01 * h + b01
            for step, val in ((k, h_even), (k + 1, h)):
                t = t0 + sign * step
                gate_ref[out_slab, pl.ds(pl.multiple_of(t * SUBLANES, SUBLANES), SUBLANES), :] = val
        return h

    def pass2(i, carry):
        hf, hb = carry
        nhf, nhb = [], []
        for s in range(N_SLAB):
            tf0 = i * SCAN_GROUP
            tb0 = CHUNK - 1 - i * SCAN_GROUP
            nhf.append(run_group(group_steps(0, s, tf0, 1), hf[s], s, tf0, 1))
            nhb.append(run_group(group_steps(2, s, tb0, -1), hb[s], N_SLAB + s, tb0, -1))
        return tuple(nhf), tuple(nhb)

    lax.fori_loop(0, CHUNK // SCAN_GROUP, pass2, (tuple(init_f), tuple(init_b)), unroll=SCAN_UNROLL)

    if st_ref is not None:
        even = (_row_iota((SUBLANES, LANES)) & 1) == 0
        for s in range(N_SLAB):
            last_f = gate_ref[s, pl.ds((CHUNK - 1) * SUBLANES, SUBLANES), :]
            first_b = gate_ref[N_SLAB + s, pl.ds(0, SUBLANES), :]
            st_ref[0, :, s * LANES:(s + 1) * LANES] = jnp.where(
                even, pltpu.roll(last_f, SUBLANES - 1, axis=0), pltpu.roll(first_b, 1, axis=0))

    for c in range(SUBLANES):
        r0 = c * CHUNK
        for s in range(N_SLAB):
            hsum = (gate_ref[s, pl.ds(c, CHUNK, stride=SUBLANES), :]
                    + gate_ref[N_SLAB + s, pl.ds(c, CHUNK, stride=SUBLANES), :])
            xg = u_ref[pl.ds(r0, CHUNK), D_RNN + s * LANES:D_RNN + (s + 1) * LANES]
            cat_ref[pl.ds(r0, CHUNK), s * LANES:(s + 1) * LANES] = (hsum * jax.nn.gelu(xg)).astype(BF16)

    xo_ref[...] = x_ref[...] + mods[2:3] * _dot(cat_ref[...], wout_ref[0])


def _mixer_call(x, mods, layer, cond_row0, p, h0, h0_index, dft, seq_len, with_state):
    n_tiles = x.shape[0] // TILE
    n_seq = TILE // seq_len
    once = pl.Buffered(1)
    lyr = lambda *shape: pl.BlockSpec((1,) + shape, lambda g: (layer,) + (0,) * len(shape), pipeline_mode=once)
    in_specs = [
        pl.BlockSpec((TILE, D_MODEL), lambda g: (g, 0)),
        pl.BlockSpec((1, 1, N_MOD, D_MODEL), lambda g: (layer, cond_row0(g), 0, 0)),
        lyr(1, D_MODEL),
        lyr(D_MODEL, D_IN),
        lyr(CONV_WIDTH, D_RNN),
        lyr(1, D_RNN),
        lyr(2, N_LRU_HEADS, LRU_HEAD_DIM, LRU_HEAD_DIM),
        lyr(2, N_LRU_HEADS, LRU_HEAD_DIM, LRU_HEAD_DIM),
        lyr(4, D_RNN),
        lyr(2, D_RNN),
        pl.BlockSpec((1, 1, 2, D_RNN), h0_index),
        lyr(HEADS_PER_BLOCK, LRU_HEAD_DIM, LRU_HEAD_DIM),
        lyr(1, D_POOL),
        pl.BlockSpec((D_FOURIER, 2 * D_FOURIER), lambda g: (0, 0), pipeline_mode=once),
        lyr(D_FOURIER, D_FOURIER),
        pl.BlockSpec((2, seq_len, seq_len), lambda g: (0, 0, 0), pipeline_mode=once),
        lyr(D_MODEL, D_MODEL),
    ]
    out_specs = [pl.BlockSpec((TILE, D_MODEL), lambda g: (g, 0))]
    out_shape = [jax.ShapeDtypeStruct(x.shape, F32)]
    if with_state:
        out_specs.append(pl.BlockSpec((1, 2 * n_seq, D_RNN), lambda g: (g, 0, 0)))
        out_shape.append(jax.ShapeDtypeStruct((n_tiles, 2 * n_seq, D_RNN), F32))
        body = functools.partial(_mixer_kernel, seq_len=seq_len)
    else:
        def body(*refs):
            _mixer_kernel(*refs[:18], None, *refs[18:], seq_len=seq_len)
    return pl.pallas_call(
        body,
        grid=(n_tiles,),
        in_specs=in_specs,
        out_specs=out_specs,
        out_shape=out_shape,
        scratch_shapes=[
            pltpu.VMEM((4, BLOCK, 2 * BLOCK), BF16),
            pltpu.VMEM((D_POOL, D_POOL), BF16),
            pltpu.VMEM((TILE, D_IN), F32),
            pltpu.VMEM((TILE, D_RNN), F32),
            pltpu.VMEM((2 * N_SLAB, SLAB_ROWS, LANES), F32),
            pltpu.VMEM((4, N_SLAB, SLAB_ROWS, LANES), F32),
            pltpu.VMEM((TILE, D_MODEL), BF16),
            pltpu.VMEM((TILE, D_MODEL), BF16),
            pltpu.VMEM((TILE, 2 * D_FOURIER), BF16),
        ],
        compiler_params=pltpu.CompilerParams(dimension_semantics=("arbitrary",),
                                             vmem_limit_bytes=VMEM_LIMIT),
        name="mixer_ctx" if with_state else "mixer_lat",
    )(x, mods, p['norm1_g'], p['w_in'], p['conv_w'], p['conv_b'], p['lru_wr'], p['lru_wi'], p['b_gate'], p['lam'], h0,
      p['pool_w'], p['pool_scale'], p['cdft'], p['fourier_w'], dft, p['w_out'])


def _mlp_kernel(x_ref, mods_ref, g2_ref, w1_ref, w2_ref, fg_ref, o_ref, h2_ref, acc_ref, *, final):
    j = pl.program_id(1)
    last = pl.num_programs(1) - 1
    mods = mods_ref[0, 0]

    def chunk_out(h2):
        hid = jnp.square(jnp.maximum(_dot(h2, w1_ref[0].astype(BF16)), 0.0))
        return _dot(hid.astype(BF16), w2_ref[0].astype(BF16))

    @pl.when(j == 0)
    def _():
        x = x_ref[...]
        ms = jnp.mean(x * x, axis=-1, keepdims=True)
        h = x * lax.rsqrt(ms + EPS) * g2_ref[0]
        h2 = (h * (1.0 + mods[4:5]) + mods[3:4]).astype(BF16)
        h2_ref[...] = h2
        acc_ref[...] = chunk_out(h2)

    @pl.when((j > 0) & (j < last))
    def _():
        acc_ref[...] += chunk_out(h2_ref[...])

    @pl.when(j == last)
    def _():
        y = x_ref[...] + mods[5:6] * (acc_ref[...] + chunk_out(h2_ref[...]))
        if final:
            ms = jnp.mean(y * y, axis=-1, keepdims=True)
            y = y * lax.rsqrt(ms + EPS) * fg_ref[...]
        o_ref[...] = y


def _mlp_call(x, mods, layer, cond_row0, p, final):
    n_tiles = x.shape[0] // TILE
    return pl.pallas_call(
        functools.partial(_mlp_kernel, final=final),
        grid=(n_tiles, D_FF // FF_CHUNK),
        in_specs=[
            pl.BlockSpec((TILE, D_MODEL), lambda g, j: (g, 0)),
            pl.BlockSpec((1, 1, N_MOD, D_MODEL), lambda g, j: (layer, cond_row0(g), 0, 0)),
            pl.BlockSpec((1, 1, D_MODEL), lambda g, j: (layer, 0, 0)),
            pl.BlockSpec((1, D_MODEL, FF_CHUNK), lambda g, j: (layer, 0, j)),
            pl.BlockSpec((1, FF_CHUNK, D_MODEL), lambda g, j: (layer, j, 0)),
            pl.BlockSpec((1, D_MODEL), lambda g, j: (0, 0)),
        ],
        out_specs=pl.BlockSpec((TILE, D_MODEL), lambda g, j: (g, 0)),
        out_shape=jax.ShapeDtypeStruct(x.shape, F32),
        scratch_shapes=[pltpu.VMEM((TILE, D_MODEL), BF16), pltpu.VMEM((TILE, D_MODEL), F32)],
        compiler_params=pltpu.CompilerParams(dimension_semantics=("arbitrary", "arbitrary"),
                                             vmem_limit_bytes=VMEM_LIMIT),
        name="mlp",
    )(x, mods, p['norm2_g'], p['mlp_w1'], p['mlp_w2'], p['final_g'])


def _dft_tables(n, scale):
    k = np.arange(n)
    ang = 2.0 * np.pi * ((k[:, None] * k[None, :]) % n) / n
    return np.cos(ang) * scale, np.sin(ang) * scale


def _seq_dft(seq_len):
    c, s = _dft_tables(seq_len, 1.0 / math.sqrt(seq_len))
    return jnp.asarray(np.stack([c, -s]), F32).astype(BF16)


def _channel_dft():
    c, s = _dft_tables(FOURIER_HEAD_DIM, 1.0 / math.sqrt(FOURIER_HEAD_DIM))
    eye = np.eye(N_FOURIER_HEADS)
    return jnp.asarray(np.concatenate([np.kron(eye, c), np.kron(eye, s)], axis=1), F32).astype(BF16)


def kernel(x_prompt, x_sample, state_rglru, c, c_ctx, norm1_g, norm2_g, final_g, w_mod, b_mod, w_in, conv_w, conv_b, lru_wr, lru_br, lru_wi, lru_bi, lru_lambda, pool_w, pool_scale, fourier_w, w_out, mlp_w1, mlp_w2):
    batch, seq, _ = x_prompt.shape
    dec_batch, dec_seq, _ = x_sample.shape
    assert TILE % seq == 0 and (batch * seq) % TILE == 0 and dec_seq == TILE

    p = {
        'norm1_g': norm1_g.reshape(DEPTH, 1, D_MODEL),
        'norm2_g': norm2_g.reshape(DEPTH, 1, D_MODEL),
        'final_g': final_g.reshape(1, D_MODEL),
        'w_in': w_in.astype(BF16),
        'conv_w': conv_w,
        'conv_b': conv_b.reshape(DEPTH, 1, D_RNN),
        'lru_wr': lru_wr,
        'lru_wi': lru_wi,
        'b_gate': jnp.stack([lru_br, lru_bi], axis=2).reshape(DEPTH, 4, D_RNN),
        'lam': lru_lambda,
        'pool_w': pool_w,
        'pool_scale': pool_scale.reshape(DEPTH, 1, D_POOL),
        'cdft': _channel_dft(),
        'fourier_w': fourier_w.astype(BF16),
        'w_out': w_out.astype(BF16),
        'mlp_w1': mlp_w1,
        'mlp_w2': mlp_w2,
    }

    cond = jnp.concatenate([c_ctx[None], c, jnp.zeros((MODS_ROWS - 1 - dec_batch, D_MODEL), F32)], axis=0)
    mods = _mods_call(cond, w_mod, b_mod).reshape(DEPTH, MODS_ROWS, N_MOD, D_MODEL)

    ctx_row = lambda g: 0
    lat_row = lambda g: 1 + g
    h0_ctx = jnp.zeros((1, 1, 2, D_RNN), F32)
    dft_ctx = _seq_dft(seq)
    dft_lat = _seq_dft(dec_seq)

    xc = x_prompt.reshape(batch * seq, D_MODEL)
    xs = x_sample.reshape(dec_batch * dec_seq, D_MODEL)
    states = []
    for l in range(DEPTH):
        final = l == DEPTH - 1
        xc, st = _mixer_call(xc, mods, l, ctx_row, p, h0_ctx, lambda g: (0, 0, 0, 0), dft_ctx, seq, True)
        states.append(st.reshape(batch, 2, D_RNN))
        xc = _mlp_call(xc, mods, l, ctx_row, p, final)
        (xs,) = _mixer_call(xs, mods, l, lat_row, p, state_rglru, lambda g, l=l: (g, l, 0, 0), dft_lat,
                            dec_seq, False)
        xs = _mlp_call(xs, mods, l, lat_row, p, final)

    y_prompt = xc.reshape(batch, seq, D_MODEL)
    y_sample = xs.reshape(dec_batch, dec_seq, D_MODEL)
    new_state = jnp.stack(states, axis=1)
    return (y_prompt, y_sample, new_state)
```

```python
import functools
import math

import numpy as np
import jax
import jax.numpy as jnp
from jax import lax
from jax.experimental import pallas as pl
from jax.experimental.pallas import tpu as pltpu

D_MODEL = 1024
DEPTH = 4
GRID_W = 64
D_RNN = 512
N_LRU_HEADS = 8
LRU_HEAD_DIM = 64
LRU_C = 8.0
CONV_WIDTH = 4
CONV_LEFT = 2
D_POOL = 256
D_FOURIER = 256
N_FOURIER_HEADS = 4
FOURIER_HEAD_DIM = 64
D_IN = 2 * D_RNN + D_POOL + D_FOURIER
D_FF = 4 * D_MODEL
N_MOD = 6
EPS = 1e-6

LANES = 128
SUBLANES = 8
TILE = 1024
CHUNK = TILE // SUBLANES
N_SLAB = D_RNN // LANES
SLAB_ROWS = TILE + SUBLANES
SCAN_GROUP = 4
SCAN_UNROLL = 4
HEADS_PER_BLOCK = 4
BLOCK = HEADS_PER_BLOCK * LRU_HEAD_DIM
PROJ_COLS = 512
FF_CHUNK = 1024
MODS_ROWS = 8
MODS_COLS = 2048
VMEM_LIMIT = 60 * 1024 * 1024

F32 = jnp.float32
BF16 = jnp.bfloat16


def _dot(a, b):
    return jnp.dot(a, b, preferred_element_type=F32)


def _row_iota(shape):
    return lax.broadcasted_iota(jnp.int32, shape, 0)


def _shift_rows(x, d):
    n = x.shape[0]
    return pltpu.roll(x, (-d) % n, axis=0)


def _mods_kernel(cond_ref, w_ref, b_ref, o_ref):
    s = cond_ref[...]
    s = s * jax.nn.sigmoid(s)
    o_ref[0] = _dot(s.astype(BF16), w_ref[0].astype(BF16)) + b_ref[0]


def _mods_call(cond, w_mod, b_mod):
    n_tiles = (N_MOD * D_MODEL) // MODS_COLS
    return pl.pallas_call(
        _mods_kernel,
        grid=(DEPTH, n_tiles),
        in_specs=[
            pl.BlockSpec((MODS_ROWS, D_MODEL), lambda l, j: (0, 0)),
            pl.BlockSpec((1, D_MODEL, MODS_COLS), lambda l, j: (l, 0, j)),
            pl.BlockSpec((1, 1, MODS_COLS), lambda l, j: (l, 0, j)),
        ],
        out_specs=pl.BlockSpec((1, MODS_ROWS, MODS_COLS), lambda l, j: (l, 0, j)),
        out_shape=jax.ShapeDtypeStruct((DEPTH, MODS_ROWS, N_MOD * D_MODEL), F32),
        compiler_params=pltpu.CompilerParams(dimension_semantics=("arbitrary", "arbitrary"),
                                             vmem_limit_bytes=VMEM_LIMIT),
        name="mods",
    )(cond, w_mod, b_mod.reshape(DEPTH, 1, N_MOD * D_MODEL))


def _window_sums(x, pos, length, unit):
    fwd = x
    bwd = jnp.where(pos >= 1, _shift_rows(x, -unit), 0.0)
    outs = [fwd + bwd]
    for k in (1, 2, 4):
        fwd = fwd + jnp.where(pos + k < length, _shift_rows(fwd, k * unit), 0.0)
        bwd = bwd + jnp.where(pos - k >= 0, _shift_rows(bwd, -k * unit), 0.0)
        outs.append(fwd + bwd)
    return outs


def _pool_axis(x, pos, length, unit):
    s1, s2, s4, s8 = _window_sums(x, pos, length, unit)
    lane = lax.broadcasted_iota(jnp.int32, x.shape, 1)
    sel = jnp.where(lane < 64, s1, jnp.where(lane < 128, s2, jnp.where(lane < 192, s4, s8)))
    half = jnp.where(lane < 64, 1, jnp.where(lane < 128, 2, jnp.where(lane < 192, 4, 8)))
    cnt = jnp.minimum(pos + half, length) - jnp.maximum(pos - half, 0)
    return sel, cnt.astype(F32)


def _mixer_kernel(x_ref, mods_ref, g1_ref, win_ref, cw_ref, cb_ref, wr_ref, wi_ref, bg_ref, lam_ref, h0_ref,
                  pw_ref, ps_ref, cdft_ref, fw_ref, dft_ref, wout_ref,
                  xo_ref, st_ref,
                  wg_ref, pwd_ref, u_ref, xc_ref, gate_ref, scan_ref, cat_ref, hn_ref, y1_ref, *, seq_len):
    n_seq = TILE // seq_len
    chunks_per_seq = seq_len // CHUNK
    mods = mods_ref[0, 0]

    @pl.when(pl.program_id(0) == 0)
    def _():
        wg_ref[...] = jnp.zeros_like(wg_ref)
        pwd_ref[...] = jnp.zeros_like(pwd_ref)
        for hd in range(HEADS_PER_BLOCK):
            rows = slice(hd * LRU_HEAD_DIM, (hd + 1) * LRU_HEAD_DIM)
            pwd_ref[rows, rows] = pw_ref[0, hd].astype(BF16)
            for d in range(2):
                for gate, w_ref in enumerate((wr_ref, wi_ref)):
                    for blk in range(2):
                        w = w_ref[0, d, blk * HEADS_PER_BLOCK + hd]
                        cols = slice(gate * BLOCK + hd * LRU_HEAD_DIM, gate * BLOCK + (hd + 1) * LRU_HEAD_DIM)
                        wg_ref[d * 2 + blk, rows, cols] = (0.5 * w).astype(BF16)

    def project(lo, hi, rows=slice(0, TILE)):
        for c0 in range(lo, hi, PROJ_COLS):
            u_ref[rows, c0:c0 + PROJ_COLS] = _dot(hn_ref[rows, :], win_ref[0, :, c0:c0 + PROJ_COLS])

    for r0 in range(0, TILE, TILE // 2):
        rows = slice(r0, r0 + TILE // 2)
        x = x_ref[rows, :]
        ms = jnp.mean(x * x, axis=-1, keepdims=True)
        h = x * lax.rsqrt(ms + EPS) * g1_ref[0]
        hn_ref[rows, :] = (h * (1.0 + mods[1:2]) + mods[0:1]).astype(BF16)
        project(0, D_RNN, rows)
    project(2 * D_RNN, D_IN)

    xr = u_ref[:, 0:D_RNN]
    pos = _row_iota((TILE, D_RNN)) & (seq_len - 1)
    acc = cb_ref[0] + xr * cw_ref[0, CONV_LEFT:CONV_LEFT + 1, :]
    for k in range(CONV_WIDTH):
        d = k - CONV_LEFT
        if d == 0:
            continue
        valid = (pos + d >= 0) & (pos + d < seq_len)
        acc = acc + jnp.where(valid, _shift_rows(xr, d), 0.0) * cw_ref[0, k:k + 1, :]
    xc_ref[...] = acc


    def pooling_group():
        xp = u_ref[:, 2 * D_RNN:2 * D_RNN + D_POOL]
        tok = _row_iota((TILE, D_POOL))
        if seq_len == TILE:
            n_rows = TILE // GRID_W
            col_sum, col_cnt = _pool_axis(xp, tok & (GRID_W - 1), GRID_W, 1)
            win_sum, row_cnt = _pool_axis(col_sum, lax.shift_right_logical(tok, GRID_W.bit_length() - 1),
                                          n_rows, GRID_W)
            cnt = row_cnt * col_cnt
        else:
            win_sum, cnt = _pool_axis(xp, tok & (seq_len - 1), seq_len, 1)
        pooled = win_sum / cnt - xp
        cat_ref[:, D_RNN:D_RNN + D_POOL] = (_dot(pooled.astype(BF16), pwd_ref[...]) * ps_ref[0]).astype(BF16)

    def channel_dft():
        xf = u_ref[:, 2 * D_RNN + D_POOL:D_IN].astype(BF16)
        y1_ref[...] = _dot(xf, cdft_ref[...]).astype(BF16)

    seq_group = min(2, n_seq)

    def fourier_rows(q):
        rows = [slice((q + k) * seq_len, (q + k + 1) * seq_len) for k in range(seq_group)]
        y_cos = jnp.concatenate([y1_ref[r, 0:D_FOURIER] for r in rows], axis=1)
        y_sin = jnp.concatenate([y1_ref[r, D_FOURIER:] for r in rows], axis=1)
        four = (_dot(dft_ref[0], y_cos) + _dot(dft_ref[1], y_sin)).astype(BF16)
        for k, r in enumerate(rows):
            cat_ref[r, D_RNN + D_POOL:] = _dot(four[:, k * D_FOURIER:(k + 1) * D_FOURIER], fw_ref[0]).astype(BF16)

    side_work = iter(
        [channel_dft, pooling_group, functools.partial(project, D_RNN, 2 * D_RNN)]
        + [functools.partial(fourier_rows, q) for q in range(0, n_seq, seq_group)])

    xcb = xc_ref[...].astype(BF16)
    for d in range(2):
        for blk in range(2):
            pre = _dot(xcb[:, blk * BLOCK:(blk + 1) * BLOCK], wg_ref[d * 2 + blk])
            for gate in range(2):
                for half in range(2):
                    lanes = slice(gate * BLOCK + half * LANES, gate * BLOCK + (half + 1) * LANES)
                    gate_ref[gate * N_SLAB + 2 * blk + half, 0:TILE, :] = pre[:, lanes]
        lam = lam_ref[0, d:d + 1, :]
        softplus_neg_lam = jnp.maximum(-lam, 0.0) + jnp.log1p(jnp.exp(-jnp.abs(lam)))
        half_rate = (-0.5 * LRU_C) * softplus_neg_lam
        hb_r = 0.5 * bg_ref[0, 2 * d:2 * d + 1, :]
        hb_i = 0.5 * bg_ref[0, 2 * d + 1:2 * d + 2, :]
        for c in range(SUBLANES):
            r0 = c * CHUNK
            for s in range(N_SLAB):
                lanes = slice(s * LANES, (s + 1) * LANES)
                t_r = jnp.tanh(gate_ref[s, pl.ds(r0, CHUNK), :] + hb_r[:, lanes])
                t_i = jnp.tanh(gate_ref[N_SLAB + s, pl.ds(r0, CHUNK), :] + hb_i[:, lanes])
                rate = half_rate[:, lanes]
                a = jnp.exp(rate + rate * t_r)
                m2 = 1.0 - a * a
                root = jnp.where(m2 > 0.0, m2 * lax.rsqrt(m2), 0.0)
                b = root * ((0.5 + 0.5 * t_i) * xc_ref[pl.ds(r0, CHUNK), lanes])
                scan_ref[2 * d, s, pl.ds(c, CHUNK, stride=SUBLANES), :] = a
                scan_ref[2 * d + 1, s, pl.ds(c, CHUNK, stride=SUBLANES), :] = b
            next(side_work, lambda: None)()
    for rest in side_work:
        rest()

    def load(arr, s, t):
        return scan_ref[arr, s, pl.ds(pl.multiple_of(t * SUBLANES, SUBLANES), SUBLANES), :]

    def compose(first, second):
        (a1, b1), (a2, b2) = first, second
        return a2 * a1, a2 * b1 + b2

    def group_steps(arr, s, t0, sign):
        return [(load(arr, s, t0 + sign * k), load(arr + 1, s, t0 + sign * k)) for k in range(SCAN_GROUP)]

    def pass1(i, carry):
        hf, pf, hb, pb = carry
        nhf, npf, nhb, npb = [], [], [], []
        for s in range(N_SLAB):
            f = group_steps(0, s, i * SCAN_GROUP, 1)
            af, bf = compose(compose(f[0], f[1]), compose(f[2], f[3]))
            nhf.append(af * hf[s] + bf)
            npf.append(af * pf[s])
            g = group_steps(2, s, CHUNK - 1 - i * SCAN_GROUP, -1)
            ab, bb = compose(compose(g[0], g[1]), compose(g[2], g[3]))
            nhb.append(ab * hb[s] + bb)
            npb.append(ab * pb[s])
        return tuple(nhf), tuple(npf), tuple(nhb), tuple(npb)

    zeros = tuple(jnp.zeros((SUBLANES, LANES), F32) for _ in range(N_SLAB))
    ones = tuple(jnp.ones((SUBLANES, LANES), F32) for _ in range(N_SLAB))
    ef, pf, eb, pb = lax.fori_loop(0, CHUNK // SCAN_GROUP, pass1, (zeros, ones, zeros, ones),
                                   unroll=SCAN_UNROLL)

    row = _row_iota((SUBLANES, LANES))
    seq_first = (row & (chunks_per_seq - 1)) == 0
    seq_last = (row & (chunks_per_seq - 1)) == chunks_per_seq - 1
    init_f, init_b = [], []
    for s in range(N_SLAB):
        h0f = jnp.broadcast_to(h0_ref[0, 0, 0:1, s * LANES:(s + 1) * LANES], (SUBLANES, LANES))
        h0b = jnp.broadcast_to(h0_ref[0, 0, 1:2, s * LANES:(s + 1) * LANES], (SUBLANES, LANES))
        cf, cb = h0f, h0b
        for _ in range(chunks_per_seq - 1):
            cf = jnp.where(seq_first, h0f, pltpu.roll(ef[s] + pf[s] * cf, 1, axis=0))
            cb = jnp.where(seq_last, h0b, pltpu.roll(eb[s] + pb[s] * cb, SUBLANES - 1, axis=0))
        init_f.append(cf)
        init_b.append(cb)

    def run_group(steps, h, out_slab, t0, sign):
        for k in range(0, SCAN_GROUP, 2):
            a0, b0 = steps[k]
            a01, b01 = compose(steps[k], steps[k + 1])
            h_even = a0 * h + b0
            h = a01 * h + b01
            for step, val in ((k, h_even), (k + 1, h)):
                t = t0 + sign * step
                gate_ref[out_slab, pl.ds(pl.multiple_of(t * SUBLANES, SUBLANES), SUBLANES), :] = val
        return h

    def pass2(i, carry):
        hf, hb = carry
        nhf, nhb = [], []
        for s in range(N_SLAB):
            tf0 = i * SCAN_GROUP
            tb0 = CHUNK - 1 - i * SCAN_GROUP
            nhf.append(run_group(group_steps(0, s, tf0, 1), hf[s], s, tf0, 1))
            nhb.append(run_group(group_steps(2, s, tb0, -1), hb[s], N_SLAB + s, tb0, -1))
        return tuple(nhf), tuple(nhb)

    lax.fori_loop(0, CHUNK // SCAN_GROUP, pass2, (tuple(init_f), tuple(init_b)), unroll=SCAN_UNROLL)

    if st_ref is not None:
        even = (_row_iota((SUBLANES, LANES)) & 1) == 0
        for s in range(N_SLAB):
            last_f = gate_ref[s, pl.ds((CHUNK - 1) * SUBLANES, SUBLANES), :]
            first_b = gate_ref[N_SLAB + s, pl.ds(0, SUBLANES), :]
            st_ref[0, :, s * LANES:(s + 1) * LANES] = jnp.where(
                even, pltpu.roll(last_f, SUBLANES - 1, axis=0), pltpu.roll(first_b, 1, axis=0))

    for c in range(SUBLANES):
        r0 = c * CHUNK
        for s in range(N_SLAB):
            hsum = (gate_ref[s, pl.ds(c, CHUNK, stride=SUBLANES), :]
                    + gate_ref[N_SLAB + s, pl.ds(c, CHUNK, stride=SUBLANES), :])
            xg = u_ref[pl.ds(r0, CHUNK), D_RNN + s * LANES:D_RNN + (s + 1) * LANES]
            cat_ref[pl.ds(r0, CHUNK), s * LANES:(s + 1) * LANES] = (hsum * jax.nn.gelu(xg)).astype(BF16)

    xo_ref[...] = x_ref[...] + mods[2:3] * _dot(cat_ref[...], wout_ref[0])


def _mixer_call(x, mods, layer, cond_row0, p, h0, h0_index, dft, seq_len, with_state):
    n_tiles = x.shape[0] // TILE
    n_seq = TILE // seq_len
    once = pl.Buffered(1)
    lyr = lambda *shape: pl.BlockSpec((1,) + shape, lambda g: (layer,) + (0,) * len(shape), pipeline_mode=once)
    in_specs = [
        pl.BlockSpec((TILE, D_MODEL), lambda g: (g, 0)),
        pl.BlockSpec((1, 1, N_MOD, D_MODEL), lambda g: (layer, cond_row0(g), 0, 0)),
        lyr(1, D_MODEL),
        lyr(D_MODEL, D_IN),
        lyr(CONV_WIDTH, D_RNN),
        lyr(1, D_RNN),
        lyr(2, N_LRU_HEADS, LRU_HEAD_DIM, LRU_HEAD_DIM),
        lyr(2, N_LRU_HEADS, LRU_HEAD_DIM, LRU_HEAD_DIM),
        lyr(4, D_RNN),
        lyr(2, D_RNN),
        pl.BlockSpec((1, 1, 2, D_RNN), h0_index),
        lyr(HEADS_PER_BLOCK, LRU_HEAD_DIM, LRU_HEAD_DIM),
        lyr(1, D_POOL),
        pl.BlockSpec((D_FOURIER, 2 * D_FOURIER), lambda g: (0, 0), pipeline_mode=once),
        lyr(D_FOURIER, D_FOURIER),
        pl.BlockSpec((2, seq_len, seq_len), lambda g: (0, 0, 0), pipeline_mode=once),
        lyr(D_MODEL, D_MODEL),
    ]
    out_specs = [pl.BlockSpec((TILE, D_MODEL), lambda g: (g, 0))]
    out_shape = [jax.ShapeDtypeStruct(x.shape, F32)]
    if with_state:
        out_specs.append(pl.BlockSpec((1, 2 * n_seq, D_RNN), lambda g: (g, 0, 0)))
        out_shape.append(jax.ShapeDtypeStruct((n_tiles, 2 * n_seq, D_RNN), F32))
        body = functools.partial(_mixer_kernel, seq_len=seq_len)
    else:
        def body(*refs):
            _mixer_kernel(*refs[:18], None, *refs[18:], seq_len=seq_len)
    return pl.pallas_call(
        body,
        grid=(n_tiles,),
        in_specs=in_specs,
        out_specs=out_specs,
        out_shape=out_shape,
        scratch_shapes=[
            pltpu.VMEM((4, BLOCK, 2 * BLOCK), BF16),
            pltpu.VMEM((D_POOL, D_POOL), BF16),
            pltpu.VMEM((TILE, D_IN), F32),
            pltpu.VMEM((TILE, D_RNN), F32),
            pltpu.VMEM((2 * N_SLAB, SLAB_ROWS, LANES), F32),
            pltpu.VMEM((4, N_SLAB, SLAB_ROWS, LANES), F32),
            pltpu.VMEM((TILE, D_MODEL), BF16),
            pltpu.VMEM((TILE, D_MODEL), BF16),
            pltpu.VMEM((TILE, 2 * D_FOURIER), BF16),
        ],
        compiler_params=pltpu.CompilerParams(dimension_semantics=("arbitrary",),
                                             vmem_limit_bytes=VMEM_LIMIT),
        name="mixer_ctx" if with_state else "mixer_lat",
    )(x, mods, p['norm1_g'], p['w_in'], p['conv_w'], p['conv_b'], p['lru_wr'], p['lru_wi'], p['b_gate'], p['lam'], h0,
      p['pool_w'], p['pool_scale'], p['cdft'], p['fourier_w'], dft, p['w_out'])


def _mlp_kernel(x_ref, mods_ref, g2_ref, w1_ref, w2_ref, fg_ref, o_ref, h2_ref, acc_ref, *, final):
    j = pl.program_id(1)
    last = pl.num_programs(1) - 1
    mods = mods_ref[0, 0]

    def hidden(h2):
        return jnp.square(jnp.maximum(_dot(h2, w1_ref[0].astype(BF16)), 0.0)).astype(BF16)

    def chunk_out(h2):
        return _dot(hidden(h2), w2_ref[0].astype(BF16))

    @pl.when(j == 0)
    def _():
        for r0 in range(0, TILE, TILE // 2):
            rows = slice(r0, r0 + TILE // 2)
            x = x_ref[rows, :]
            ms = jnp.mean(x * x, axis=-1, keepdims=True)
            h = x * lax.rsqrt(ms + EPS) * g2_ref[0]
            h2 = (h * (1.0 + mods[4:5]) + mods[3:4]).astype(BF16)
            h2_ref[rows, :] = h2
            acc_ref[rows, :] = chunk_out(h2)

    @pl.when((j > 0) & (j < last))
    def _():
        acc_ref[...] += chunk_out(h2_ref[...])

    @pl.when(j == last)
    def _():
        hid = hidden(h2_ref[...])
        for c0 in range(0, D_MODEL, PROJ_COLS):
            cols = slice(c0, c0 + PROJ_COLS)
            ff = acc_ref[:, cols] + _dot(hid, w2_ref[0, :, cols].astype(BF16))
            o_ref[:, cols] = x_ref[:, cols] + mods[5:6, cols] * ff
        if final:
            y = o_ref[...]
            ms = jnp.mean(y * y, axis=-1, keepdims=True)
            o_ref[...] = y * lax.rsqrt(ms + EPS) * fg_ref[...]


def _mlp_call(x, mods, layer, cond_row0, p, final):
    n_tiles = x.shape[0] // TILE
    return pl.pallas_call(
        functools.partial(_mlp_kernel, final=final),
        grid=(n_tiles, D_FF // FF_CHUNK),
        in_specs=[
            pl.BlockSpec((TILE, D_MODEL), lambda g, j: (g, 0)),
            pl.BlockSpec((1, 1, N_MOD, D_MODEL), lambda g, j: (layer, cond_row0(g), 0, 0)),
            pl.BlockSpec((1, 1, D_MODEL), lambda g, j: (layer, 0, 0)),
            pl.BlockSpec((1, D_MODEL, FF_CHUNK), lambda g, j: (layer, 0, j)),
            pl.BlockSpec((1, FF_CHUNK, D_MODEL), lambda g, j: (layer, j, 0)),
            pl.BlockSpec((1, D_MODEL), lambda g, j: (0, 0)),
        ],
        out_specs=pl.BlockSpec((TILE, D_MODEL), lambda g, j: (g, 0)),
        out_shape=jax.ShapeDtypeStruct(x.shape, F32),
        scratch_shapes=[pltpu.VMEM((TILE, D_MODEL), BF16), pltpu.VMEM((TILE, D_MODEL), F32)],
        compiler_params=pltpu.CompilerParams(dimension_semantics=("arbitrary", "arbitrary"),
                                             vmem_limit_bytes=VMEM_LIMIT),
        name="mlp",
    )(x, mods, p['norm2_g'], p['mlp_w1'], p['mlp_w2'], p['final_g'])


def _dft_tables(n, scale):
    k = np.arange(n)
    ang = 2.0 * np.pi * ((k[:, None] * k[None, :]) % n) / n
    return np.cos(ang) * scale, np.sin(ang) * scale


def _seq_dft(seq_len):
    c, s = _dft_tables(seq_len, 1.0 / math.sqrt(seq_len))
    return jnp.asarray(np.stack([c, -s]), F32).astype(BF16)


def _channel_dft():
    c, s = _dft_tables(FOURIER_HEAD_DIM, 1.0 / math.sqrt(FOURIER_HEAD_DIM))
    eye = np.eye(N_FOURIER_HEADS)
    return jnp.asarray(np.concatenate([np.kron(eye, c), np.kron(eye, s)], axis=1), F32).astype(BF16)


def kernel(x_prompt, x_sample, state_rglru, c, c_ctx, norm1_g, norm2_g, final_g, w_mod, b_mod, w_in, conv_w, conv_b, lru_wr, lru_br, lru_wi, lru_bi, lru_lambda, pool_w, pool_scale, fourier_w, w_out, mlp_w1, mlp_w2):
    batch, seq, _ = x_prompt.shape
    dec_batch, dec_seq, _ = x_sample.shape
    assert TILE % seq == 0 and (batch * seq) % TILE == 0 and dec_seq == TILE

    p = {
        'norm1_g': norm1_g.reshape(DEPTH, 1, D_MODEL),
        'norm2_g': norm2_g.reshape(DEPTH, 1, D_MODEL),
        'final_g': final_g.reshape(1, D_MODEL),
        'w_in': w_in.astype(BF16),
        'conv_w': conv_w,
        'conv_b': conv_b.reshape(DEPTH, 1, D_RNN),
        'lru_wr': lru_wr,
        'lru_wi': lru_wi,
        'b_gate': jnp.stack([lru_br, lru_bi], axis=2).reshape(DEPTH, 4, D_RNN),
        'lam': lru_lambda,
        'pool_w': pool_w,
        'pool_scale': pool_scale.reshape(DEPTH, 1, D_POOL),
        'cdft': _channel_dft(),
        'fourier_w': fourier_w.astype(BF16),
        'w_out': w_out.astype(BF16),
        'mlp_w1': mlp_w1,
        'mlp_w2': mlp_w2,
    }

    cond = jnp.concatenate([c_ctx[None], c, jnp.zeros((MODS_ROWS - 1 - dec_batch, D_MODEL), F32)], axis=0)
    mods = _mods_call(cond, w_mod, b_mod).reshape(DEPTH, MODS_ROWS, N_MOD, D_MODEL)

    ctx_row = lambda g: 0
    lat_row = lambda g: 1 + g
    h0_ctx = jnp.zeros((1, 1, 2, D_RNN), F32)
    dft_ctx = _seq_dft(seq)
    dft_lat = _seq_dft(dec_seq)

    xc = x_prompt.reshape(batch * seq, D_MODEL)
    xs = x_sample.reshape(dec_batch * dec_seq, D_MODEL)
    states = []
    for l in range(DEPTH):
        final = l == DEPTH - 1
        xc, st = _mixer_call(xc, mods, l, ctx_row, p, h0_ctx, lambda g: (0, 0, 0, 0), dft_ctx, seq, True)
        states.append(st.reshape(batch, 2, D_RNN))
        xc = _mlp_call(xc, mods, l, ctx_row, p, final)
        (xs,) = _mixer_call(xs, mods, l, lat_row, p, state_rglru, lambda g, l=l: (g, l, 0, 0), dft_lat,
                            dec_seq, False)
        xs = _mlp_call(xs, mods, l, lat_row, p, final)

    y_prompt = xc.reshape(batch, seq, D_MODEL)
    y_sample = xs.reshape(dec_batch, dec_seq, D_MODEL)
    new_state = jnp.stack(states, axis=1)
    return (y_prompt, y_sample, new_state)
```

```python
import functools
import math

import numpy as np
import jax
import jax.numpy as jnp
from jax import lax
from jax.experimental import pallas as pl
from jax.experimental.pallas import tpu as pltpu

D_MODEL = 1024
DEPTH = 4
GRID_W = 64
D_RNN = 512
N_LRU_HEADS = 8
LRU_HEAD_DIM = 64
LRU_C = 8.0
CONV_WIDTH = 4
CONV_LEFT = 2
D_POOL = 256
POOL_WINDOWS = (2, 4, 8, 16)
POOL_GROUP_DIM = D_POOL // len(POOL_WINDOWS)
D_FOURIER = 256
N_FOURIER_HEADS = 4
FOURIER_HEAD_DIM = 64
D_IN = 2 * D_RNN + D_POOL + D_FOURIER
D_FF = 4 * D_MODEL
N_MOD = 6
EPS = 1e-6

LANES = 128
SUBLANES = 8
TILE = 1024
CHUNK = TILE // SUBLANES
N_SLAB = D_RNN // LANES
SLAB_ROWS = TILE + SUBLANES
SCAN_GROUP = 4
SCAN_UNROLL = 4
HEADS_PER_BLOCK = 4
BLOCK = HEADS_PER_BLOCK * LRU_HEAD_DIM
N_MXU = 2
MXU_TILE = 256
PROJ_COLS = N_MXU * MXU_TILE
FF_CHUNK = 1024
MODS_ROWS = 8
MODS_COLS = 2048
VMEM_LIMIT = 60 * 1024 * 1024

F32 = jnp.float32
BF16 = jnp.bfloat16


def _dot(a, b):
    return jnp.dot(a, b, preferred_element_type=F32)


def _row_iota(shape):
    return lax.broadcasted_iota(jnp.int32, shape, 0)


def _shift_rows(x, d):
    n = x.shape[0]
    return pltpu.roll(x, (-d) % n, axis=0)


def _mods_kernel(cond_ref, w_ref, b_ref, o_ref):
    s = cond_ref[...]
    s = s * jax.nn.sigmoid(s)
    o_ref[0] = _dot(s.astype(BF16), w_ref[0].astype(BF16)) + b_ref[0]


def _mods_call(cond, w_mod, b_mod):
    n_tiles = (N_MOD * D_MODEL) // MODS_COLS
    return pl.pallas_call(
        _mods_kernel,
        grid=(DEPTH, n_tiles),
        in_specs=[
            pl.BlockSpec((MODS_ROWS, D_MODEL), lambda l, j: (0, 0)),
            pl.BlockSpec((1, D_MODEL, MODS_COLS), lambda l, j: (l, 0, j)),
            pl.BlockSpec((1, 1, MODS_COLS), lambda l, j: (l, 0, j)),
        ],
        out_specs=pl.BlockSpec((1, MODS_ROWS, MODS_COLS), lambda l, j: (l, 0, j)),
        out_shape=jax.ShapeDtypeStruct((DEPTH, MODS_ROWS, N_MOD * D_MODEL), F32),
        compiler_params=pltpu.CompilerParams(dimension_semantics=("arbitrary", "arbitrary"),
                                             vmem_limit_bytes=VMEM_LIMIT),
        name="mods",
    )(cond, w_mod, b_mod.reshape(DEPTH, 1, N_MOD * D_MODEL))


def _window_sums(x, pos, length, unit, max_half):
    fwd = x
    bwd = jnp.where(pos >= 1, _shift_rows(x, -unit), 0.0)
    sums = {1: fwd + bwd}
    k = 1
    while k < max_half:
        fwd = fwd + jnp.where(pos + k < length, _shift_rows(fwd, k * unit), 0.0)
        bwd = bwd + jnp.where(pos - k >= 0, _shift_rows(bwd, -k * unit), 0.0)
        k *= 2
        sums[k] = fwd + bwd
    return sums


def _pool_axis(x, pos, length, unit, halves):
    sums = _window_sums(x, pos, length, unit, max(halves))
    low = lax.broadcasted_iota(jnp.int32, x.shape, 1) < POOL_GROUP_DIM
    sel = jnp.where(low, sums[halves[0]], sums[halves[1]])
    half = jnp.where(low, halves[0], halves[1])
    cnt = jnp.minimum(pos + half, length) - jnp.maximum(pos - half, 0)
    return sel, cnt.astype(F32)


def _mixer_kernel(x_ref, mods_ref, g1_ref, win_ref, cw_ref, cb_ref, wr_ref, wi_ref, bg_ref, lam_ref, h0_ref,
                  pw_ref, ps_ref, cdft_ref, fw_ref, dft_ref, wout_ref,
                  xo_ref, st_ref,
                  wg_ref, pwd_ref, u_ref, xc_ref, gate_ref, scan_ref, cat_ref, hn_ref, y1_ref, pool_ref, *, seq_len):
    n_seq = TILE // seq_len
    chunks_per_seq = seq_len // CHUNK
    mods = mods_ref[0, 0]

    @pl.when(pl.program_id(0) == 0)
    def _():
        wg_ref[...] = jnp.zeros_like(wg_ref)
        pwd_ref[...] = jnp.zeros_like(pwd_ref)
        for hd in range(HEADS_PER_BLOCK):
            rows = slice(hd * LRU_HEAD_DIM, (hd + 1) * LRU_HEAD_DIM)
            pwd_ref[rows, rows] = pw_ref[0, hd].astype(BF16)
            for d in range(2):
                for gate, w_ref in enumerate((wr_ref, wi_ref)):
                    for blk in range(2):
                        w = w_ref[0, d, blk * HEADS_PER_BLOCK + hd]
                        cols = slice(gate * BLOCK + hd * LRU_HEAD_DIM, gate * BLOCK + (hd + 1) * LRU_HEAD_DIM)
                        wg_ref[d * 2 + blk, rows, cols] = (0.5 * w).astype(BF16)

    def project(lo, hi, rows=slice(0, TILE)):
        for c0 in range(lo, hi, PROJ_COLS):
            u_ref[rows, c0:c0 + PROJ_COLS] = _dot(hn_ref[rows, :], win_ref[0, :, c0:c0 + PROJ_COLS])

    for r0 in range(0, TILE, TILE // 2):
        rows = slice(r0, r0 + TILE // 2)
        x = x_ref[rows, :]
        ms = jnp.mean(x * x, axis=-1, keepdims=True)
        h = x * lax.rsqrt(ms + EPS) * g1_ref[0]
        hn_ref[rows, :] = (h * (1.0 + mods[1:2]) + mods[0:1]).astype(BF16)
        project(0, D_RNN, rows)
    project(2 * D_RNN, D_IN)

    pos = _row_iota((TILE, LANES)) & (seq_len - 1)
    taps = [(k, k - CONV_LEFT) for k in range(CONV_WIDTH) if k != CONV_LEFT]
    valid = {d: (pos + d >= 0) & (pos + d < seq_len) for _, d in taps}
    for s in range(N_SLAB):
        lanes = slice(s * LANES, (s + 1) * LANES)
        xr = u_ref[:, lanes]
        acc = cb_ref[0, :, lanes] + xr * cw_ref[0, CONV_LEFT:CONV_LEFT + 1, lanes]
        for k, d in taps:
            acc = acc + jnp.where(valid[d], _shift_rows(xr, d), 0.0) * cw_ref[0, k:k + 1, lanes]
        xc_ref[:, lanes] = acc


    def pooled_slab(k):
        halves = tuple(w // 2 for w in POOL_WINDOWS[2 * k:2 * k + 2])
        xp = u_ref[:, 2 * D_RNN + k * LANES:2 * D_RNN + (k + 1) * LANES]
        tok = _row_iota((TILE, LANES))
        if seq_len == TILE:
            n_rows = TILE // GRID_W
            col_sum, col_cnt = _pool_axis(xp, tok & (GRID_W - 1), GRID_W, 1, halves)
            win_sum, row_cnt = _pool_axis(col_sum, lax.shift_right_logical(tok, GRID_W.bit_length() - 1),
                                          n_rows, GRID_W, halves)
            cnt = row_cnt * col_cnt
        else:
            win_sum, cnt = _pool_axis(xp, tok & (seq_len - 1), seq_len, 1, halves)
        pool_ref[:, k * LANES:(k + 1) * LANES] = (win_sum / cnt - xp).astype(BF16)

    def pooling_group():
        for k in range(D_POOL // LANES):
            pooled_slab(k)
        cat_ref[:, D_RNN:D_RNN + D_POOL] = (_dot(pool_ref[...], pwd_ref[...]) * ps_ref[0]).astype(BF16)

    def channel_dft():
        xf = u_ref[:, 2 * D_RNN + D_POOL:D_IN].astype(BF16)
        y1_ref[...] = _dot(xf, cdft_ref[...]).astype(BF16)

    seq_group = min(N_MXU, n_seq)

    def fourier_rows(q):
        rows = [slice((q + k) * seq_len, (q + k + 1) * seq_len) for k in range(seq_group)]
        y_cos = jnp.concatenate([y1_ref[r, 0:D_FOURIER] for r in rows], axis=1)
        y_sin = jnp.concatenate([y1_ref[r, D_FOURIER:] for r in rows], axis=1)
        four = (_dot(dft_ref[0], y_cos) + _dot(dft_ref[1], y_sin)).astype(BF16)
        for k, r in enumerate(rows):
            cat_ref[r, D_RNN + D_POOL:] = _dot(four[:, k * D_FOURIER:(k + 1) * D_FOURIER], fw_ref[0]).astype(BF16)

    side_work = iter(
        [channel_dft, pooling_group, functools.partial(project, D_RNN, 2 * D_RNN)]
        + [functools.partial(fourier_rows, q) for q in range(0, n_seq, seq_group)])

    xcb = xc_ref[...].astype(BF16)
    for d in range(2):
        for blk in range(2):
            pre = _dot(xcb[:, blk * BLOCK:(blk + 1) * BLOCK], wg_ref[d * 2 + blk])
            for gate in range(2):
                for half in range(2):
                    lanes = slice(gate * BLOCK + half * LANES, gate * BLOCK + (half + 1) * LANES)
                    gate_ref[gate * N_SLAB + 2 * blk + half, 0:TILE, :] = pre[:, lanes]
        lam = lam_ref[0, d:d + 1, :]
        softplus_neg_lam = jnp.maximum(-lam, 0.0) + jnp.log1p(jnp.exp(-jnp.abs(lam)))
        half_rate = (-0.5 * LRU_C) * softplus_neg_lam
        hb_r = 0.5 * bg_ref[0, 2 * d:2 * d + 1, :]
        hb_i = 0.5 * bg_ref[0, 2 * d + 1:2 * d + 2, :]
        for c in range(SUBLANES):
            r0 = c * CHUNK
            for s in range(N_SLAB):
                lanes = slice(s * LANES, (s + 1) * LANES)
                t_r = jnp.tanh(gate_ref[s, pl.ds(r0, CHUNK), :] + hb_r[:, lanes])
                t_i = jnp.tanh(gate_ref[N_SLAB + s, pl.ds(r0, CHUNK), :] + hb_i[:, lanes])
                rate = half_rate[:, lanes]
                a = jnp.exp(rate + rate * t_r)
                m2 = 1.0 - a * a
                root = jnp.where(m2 > 0.0, m2 * lax.rsqrt(m2), 0.0)
                b = root * ((0.5 + 0.5 * t_i) * xc_ref[pl.ds(r0, CHUNK), lanes])
                scan_ref[2 * d, s, pl.ds(c, CHUNK, stride=SUBLANES), :] = a
                scan_ref[2 * d + 1, s, pl.ds(c, CHUNK, stride=SUBLANES), :] = b
            next(side_work, lambda: None)()
    for rest in side_work:
        rest()

    def load(arr, s, t):
        return scan_ref[arr, s, pl.ds(pl.multiple_of(t * SUBLANES, SUBLANES), SUBLANES), :]

    def compose(first, second):
        (a1, b1), (a2, b2) = first, second
        return a2 * a1, a2 * b1 + b2

    def group_steps(arr, s, t0, sign):
        return [(load(arr, s, t0 + sign * k), load(arr + 1, s, t0 + sign * k)) for k in range(SCAN_GROUP)]

    def pass1(i, carry):
        hf, pf, hb, pb = carry
        nhf, npf, nhb, npb = [], [], [], []
        for s in range(N_SLAB):
            f = group_steps(0, s, i * SCAN_GROUP, 1)
            af, bf = compose(compose(f[0], f[1]), compose(f[2], f[3]))
            nhf.append(af * hf[s] + bf)
            npf.append(af * pf[s])
            g = group_steps(2, s, CHUNK - 1 - i * SCAN_GROUP, -1)
            ab, bb = compose(compose(g[0], g[1]), compose(g[2], g[3]))
            nhb.append(ab * hb[s] + bb)
            npb.append(ab * pb[s])
        return tuple(nhf), tuple(npf), tuple(nhb), tuple(npb)

    zeros = tuple(jnp.zeros((SUBLANES, LANES), F32) for _ in range(N_SLAB))
    ones = tuple(jnp.ones((SUBLANES, LANES), F32) for _ in range(N_SLAB))
    ef, pf, eb, pb = lax.fori_loop(0, CHUNK // SCAN_GROUP, pass1, (zeros, ones, zeros, ones),
                                   unroll=SCAN_UNROLL)

    row = _row_iota((SUBLANES, LANES))
    seq_first = (row & (chunks_per_seq - 1)) == 0
    seq_last = (row & (chunks_per_seq - 1)) == chunks_per_seq - 1
    init_f, init_b = [], []
    for s in range(N_SLAB):
        h0f = jnp.broadcast_to(h0_ref[0, 0, 0:1, s * LANES:(s + 1) * LANES], (SUBLANES, LANES))
        h0b = jnp.broadcast_to(h0_ref[0, 0, 1:2, s * LANES:(s + 1) * LANES], (SUBLANES, LANES))
        cf, cb = h0f, h0b
        for _ in range(chunks_per_seq - 1):
            cf = jnp.where(seq_first, h0f, pltpu.roll(ef[s] + pf[s] * cf, 1, axis=0))
            cb = jnp.where(seq_last, h0b, pltpu.roll(eb[s] + pb[s] * cb, SUBLANES - 1, axis=0))
        init_f.append(cf)
        init_b.append(cb)

    def run_group(steps, h, out_slab, t0, sign):
        for k in range(0, SCAN_GROUP, 2):
            a0, b0 = steps[k]
            a01, b01 = compose(steps[k], steps[k + 1])
            h_even = a0 * h + b0
            h = a01 * h + b01
            for step, val in ((k, h_even), (k + 1, h)):
                t = t0 + sign * step
                gate_ref[out_slab, pl.ds(pl.multiple_of(t * SUBLANES, SUBLANES), SUBLANES), :] = val
        return h

    def pass2(i, carry):
        hf, hb = carry
        nhf, nhb = [], []
        for s in range(N_SLAB):
            tf0 = i * SCAN_GROUP
            tb0 = CHUNK - 1 - i * SCAN_GROUP
            nhf.append(run_group(group_steps(0, s, tf0, 1), hf[s], s, tf0, 1))
            nhb.append(run_group(group_steps(2, s, tb0, -1), hb[s], N_SLAB + s, tb0, -1))
        return tuple(nhf), tuple(nhb)

    lax.fori_loop(0, CHUNK // SCAN_GROUP, pass2, (tuple(init_f), tuple(init_b)), unroll=SCAN_UNROLL)

    if st_ref is not None:
        even = (_row_iota((SUBLANES, LANES)) & 1) == 0
        for s in range(N_SLAB):
            last_f = gate_ref[s, pl.ds((CHUNK - 1) * SUBLANES, SUBLANES), :]
            first_b = gate_ref[N_SLAB + s, pl.ds(0, SUBLANES), :]
            st_ref[0, :, s * LANES:(s + 1) * LANES] = jnp.where(
                even, pltpu.roll(last_f, SUBLANES - 1, axis=0), pltpu.roll(first_b, 1, axis=0))

    for c in range(SUBLANES):
        r0 = c * CHUNK
        for s in range(N_SLAB):
            hsum = (gate_ref[s, pl.ds(c, CHUNK, stride=SUBLANES), :]
                    + gate_ref[N_SLAB + s, pl.ds(c, CHUNK, stride=SUBLANES), :])
            xg = u_ref[pl.ds(r0, CHUNK), D_RNN + s * LANES:D_RNN + (s + 1) * LANES]
            cat_ref[pl.ds(r0, CHUNK), s * LANES:(s + 1) * LANES] = (hsum * jax.nn.gelu(xg)).astype(BF16)

    xo_ref[...] = x_ref[...] + mods[2:3] * _dot(cat_ref[...], wout_ref[0])


def _mixer_call(x, mods, layer, cond_row0, p, h0, h0_index, dft, seq_len, with_state):
    n_tiles = x.shape[0] // TILE
    n_seq = TILE // seq_len
    once = pl.Buffered(1)
    lyr = lambda *shape: pl.BlockSpec((1,) + shape, lambda g: (layer,) + (0,) * len(shape), pipeline_mode=once)
    in_specs = [
        pl.BlockSpec((TILE, D_MODEL), lambda g: (g, 0)),
        pl.BlockSpec((1, 1, N_MOD, D_MODEL), lambda g: (layer, cond_row0(g), 0, 0)),
        lyr(1, D_MODEL),
        lyr(D_MODEL, D_IN),
        lyr(CONV_WIDTH, D_RNN),
        lyr(1, D_RNN),
        lyr(2, N_LRU_HEADS, LRU_HEAD_DIM, LRU_HEAD_DIM),
        lyr(2, N_LRU_HEADS, LRU_HEAD_DIM, LRU_HEAD_DIM),
        lyr(4, D_RNN),
        lyr(2, D_RNN),
        pl.BlockSpec((1, 1, 2, D_RNN), h0_index),
        lyr(HEADS_PER_BLOCK, LRU_HEAD_DIM, LRU_HEAD_DIM),
        lyr(1, D_POOL),
        pl.BlockSpec((D_FOURIER, 2 * D_FOURIER), lambda g: (0, 0), pipeline_mode=once),
        lyr(D_FOURIER, D_FOURIER),
        pl.BlockSpec((2, seq_len, seq_len), lambda g: (0, 0, 0), pipeline_mode=once),
        lyr(D_MODEL, D_MODEL),
    ]
    out_specs = [pl.BlockSpec((TILE, D_MODEL), lambda g: (g, 0))]
    out_shape = [jax.ShapeDtypeStruct(x.shape, F32)]
    if with_state:
        out_specs.append(pl.BlockSpec((1, 2 * n_seq, D_RNN), lambda g: (g, 0, 0)))
        out_shape.append(jax.ShapeDtypeStruct((n_tiles, 2 * n_seq, D_RNN), F32))
        body = functools.partial(_mixer_kernel, seq_len=seq_len)
    else:
        def body(*refs):
            _mixer_kernel(*refs[:18], None, *refs[18:], seq_len=seq_len)
    return pl.pallas_call(
        body,
        grid=(n_tiles,),
        in_specs=in_specs,
        out_specs=out_specs,
        out_shape=out_shape,
        scratch_shapes=[
            pltpu.VMEM((4, BLOCK, 2 * BLOCK), BF16),
            pltpu.VMEM((D_POOL, D_POOL), BF16),
            pltpu.VMEM((TILE, D_IN), F32),
            pltpu.VMEM((TILE, D_RNN), F32),
            pltpu.VMEM((2 * N_SLAB, SLAB_ROWS, LANES), F32),
            pltpu.VMEM((4, N_SLAB, SLAB_ROWS, LANES), F32),
            pltpu.VMEM((TILE, D_MODEL), BF16),
            pltpu.VMEM((TILE, D_MODEL), BF16),
            pltpu.VMEM((TILE, 2 * D_FOURIER), BF16),
            pltpu.VMEM((TILE, D_POOL), BF16),
        ],
        compiler_params=pltpu.CompilerParams(dimension_semantics=("arbitrary",),
                                             vmem_limit_bytes=VMEM_LIMIT),
        name="mixer_ctx" if with_state else "mixer_lat",
    )(x, mods, p['norm1_g'], p['w_in'], p['conv_w'], p['conv_b'], p['lru_wr'], p['lru_wi'], p['b_gate'], p['lam'], h0,
      p['pool_w'], p['pool_scale'], p['cdft'], p['fourier_w'], dft, p['w_out'])


def _mlp_kernel(x_ref, mods_ref, g2_ref, w1_ref, w2_ref, fg_ref, o_ref, h2_ref, acc_ref, *, final):
    j = pl.program_id(1)
    last = pl.num_programs(1) - 1
    mods = mods_ref[0, 0]

    def hidden(h2):
        return jnp.square(jnp.maximum(_dot(h2, w1_ref[0].astype(BF16)), 0.0)).astype(BF16)

    def chunk_out(h2):
        return _dot(hidden(h2), w2_ref[0].astype(BF16))

    @pl.when(j == 0)
    def _():
        for r0 in range(0, TILE, TILE // 2):
            rows = slice(r0, r0 + TILE // 2)
            x = x_ref[rows, :]
            ms = jnp.mean(x * x, axis=-1, keepdims=True)
            h = x * lax.rsqrt(ms + EPS) * g2_ref[0]
            h2 = (h * (1.0 + mods[4:5]) + mods[3:4]).astype(BF16)
            h2_ref[rows, :] = h2
            acc_ref[rows, :] = chunk_out(h2)

    @pl.when((j > 0) & (j < last))
    def _():
        acc_ref[...] += chunk_out(h2_ref[...])

    @pl.when(j == last)
    def _():
        hid = hidden(h2_ref[...])
        for c0 in range(0, D_MODEL, PROJ_COLS):
            cols = slice(c0, c0 + PROJ_COLS)
            ff = acc_ref[:, cols] + _dot(hid, w2_ref[0, :, cols].astype(BF16))
            o_ref[:, cols] = x_ref[:, cols] + mods[5:6, cols] * ff
        if final:
            y = o_ref[...]
            ms = jnp.mean(y * y, axis=-1, keepdims=True)
            o_ref[...] = y * lax.rsqrt(ms + EPS) * fg_ref[...]


def _mlp_call(x, mods, layer, cond_row0, p, final):
    n_tiles = x.shape[0] // TILE
    return pl.pallas_call(
        functools.partial(_mlp_kernel, final=final),
        grid=(n_tiles, D_FF // FF_CHUNK),
        in_specs=[
            pl.BlockSpec((TILE, D_MODEL), lambda g, j: (g, 0)),
            pl.BlockSpec((1, 1, N_MOD, D_MODEL), lambda g, j: (layer, cond_row0(g), 0, 0)),
            pl.BlockSpec((1, 1, D_MODEL), lambda g, j: (layer, 0, 0)),
            pl.BlockSpec((1, D_MODEL, FF_CHUNK), lambda g, j: (layer, 0, j)),
            pl.BlockSpec((1, FF_CHUNK, D_MODEL), lambda g, j: (layer, j, 0)),
            pl.BlockSpec((1, D_MODEL), lambda g, j: (0, 0)),
        ],
        out_specs=pl.BlockSpec((TILE, D_MODEL), lambda g, j: (g, 0)),
        out_shape=jax.ShapeDtypeStruct(x.shape, F32),
        scratch_shapes=[pltpu.VMEM((TILE, D_MODEL), BF16), pltpu.VMEM((TILE, D_MODEL), F32)],
        compiler_params=pltpu.CompilerParams(dimension_semantics=("arbitrary", "arbitrary"),
                                             vmem_limit_bytes=VMEM_LIMIT),
        name="mlp",
    )(x, mods, p['norm2_g'], p['mlp_w1'], p['mlp_w2'], p['final_g'])


def _dft_tables(n, scale):
    k = np.arange(n)
    ang = 2.0 * np.pi * ((k[:, None] * k[None, :]) % n) / n
    return np.cos(ang) * scale, np.sin(ang) * scale


def _seq_dft(seq_len):
    c, s = _dft_tables(seq_len, 1.0 / math.sqrt(seq_len))
    return jnp.asarray(np.stack([c, -s]), F32).astype(BF16)


def _channel_dft():
    c, s = _dft_tables(FOURIER_HEAD_DIM, 1.0 / math.sqrt(FOURIER_HEAD_DIM))
    eye = np.eye(N_FOURIER_HEADS)
    return jnp.asarray(np.concatenate([np.kron(eye, c), np.kron(eye, s)], axis=1), F32).astype(BF16)


def kernel(x_prompt, x_sample, state_rglru, c, c_ctx, norm1_g, norm2_g, final_g, w_mod, b_mod, w_in, conv_w, conv_b, lru_wr, lru_br, lru_wi, lru_bi, lru_lambda, pool_w, pool_scale, fourier_w, w_out, mlp_w1, mlp_w2):
    batch, seq, _ = x_prompt.shape
    dec_batch, dec_seq, _ = x_sample.shape
    assert TILE % seq == 0 and (batch * seq) % TILE == 0 and dec_seq == TILE

    p = {
        'norm1_g': norm1_g.reshape(DEPTH, 1, D_MODEL),
        'norm2_g': norm2_g.reshape(DEPTH, 1, D_MODEL),
        'final_g': final_g.reshape(1, D_MODEL),
        'w_in': w_in.astype(BF16),
        'conv_w': conv_w,
        'conv_b': conv_b.reshape(DEPTH, 1, D_RNN),
        'lru_wr': lru_wr,
        'lru_wi': lru_wi,
        'b_gate': jnp.stack([lru_br, lru_bi], axis=2).reshape(DEPTH, 4, D_RNN),
        'lam': lru_lambda,
        'pool_w': pool_w,
        'pool_scale': pool_scale.reshape(DEPTH, 1, D_POOL),
        'cdft': _channel_dft(),
        'fourier_w': fourier_w.astype(BF16),
        'w_out': w_out.astype(BF16),
        'mlp_w1': mlp_w1,
        'mlp_w2': mlp_w2,
    }

    cond = jnp.concatenate([c_ctx[None], c, jnp.zeros((MODS_ROWS - 1 - dec_batch, D_MODEL), F32)], axis=0)
    mods = _mods_call(cond, w_mod, b_mod).reshape(DEPTH, MODS_ROWS, N_MOD, D_MODEL)

    ctx_row = lambda g: 0
    lat_row = lambda g: 1 + g
    h0_ctx = jnp.zeros((1, 1, 2, D_RNN), F32)
    dft_ctx = _seq_dft(seq)
    dft_lat = _seq_dft(dec_seq)

    xc = x_prompt.reshape(batch * seq, D_MODEL)
    xs = x_sample.reshape(dec_batch * dec_seq, D_MODEL)
    states = []
    for l in range(DEPTH):
        final = l == DEPTH - 1
        xc, st = _mixer_call(xc, mods, l, ctx_row, p, h0_ctx, lambda g: (0, 0, 0, 0), dft_ctx, seq, True)
        states.append(st.reshape(batch, 2, D_RNN))
        xc = _mlp_call(xc, mods, l, ctx_row, p, final)
        (xs,) = _mixer_call(xs, mods, l, lat_row, p, state_rglru, lambda g, l=l: (g, l, 0, 0), dft_lat,
                            dec_seq, False)
        xs = _mlp_call(xs, mods, l, lat_row, p, final)

    y_prompt = xc.reshape(batch, seq, D_MODEL)
    y_sample = xs.reshape(dec_batch, dec_seq, D_MODEL)
    new_state = jnp.stack(states, axis=1)
    return (y_prompt, y_sample, new_state)
```

```python
import functools
import math

import numpy as np
import jax
import jax.numpy as jnp
from jax import lax
from jax.experimental import pallas as pl
from jax.experimental.pallas import tpu as pltpu

D_MODEL = 1024
DEPTH = 4
GRID_W = 64
D_RNN = 512
N_LRU_HEADS = 8
LRU_HEAD_DIM = 64
LRU_C = 8.0
CONV_WIDTH = 4
CONV_LEFT = 2
D_POOL = 256
POOL_WINDOWS = (2, 4, 8, 16)
POOL_GROUP_DIM = D_POOL // len(POOL_WINDOWS)
D_FOURIER = 256
N_FOURIER_HEADS = 4
FOURIER_HEAD_DIM = 64
D_IN = 2 * D_RNN + D_POOL + D_FOURIER
D_FF = 4 * D_MODEL
N_MOD = 6
EPS = 1e-6

LANES = 128
SUBLANES = 8
TILE = 1024
CHUNK = TILE // SUBLANES
N_SLAB = D_RNN // LANES
SLAB_ROWS = TILE + SUBLANES
SCAN_GROUP = 4
SCAN_UNROLL = 4
HEADS_PER_BLOCK = 4
BLOCK = HEADS_PER_BLOCK * LRU_HEAD_DIM
N_MXU = 2
MXU_TILE = 256
PROJ_COLS = N_MXU * MXU_TILE
FF_CHUNK = 1024
MODS_ROWS = 8
MODS_COLS = 2048
VMEM_LIMIT = 60 * 1024 * 1024

F32 = jnp.float32
BF16 = jnp.bfloat16


def _dot(a, b):
    return jnp.dot(a, b, preferred_element_type=F32)


def _row_iota(shape):
    return lax.broadcasted_iota(jnp.int32, shape, 0)


def _shift_rows(x, d):
    n = x.shape[0]
    return pltpu.roll(x, (-d) % n, axis=0)


def _mods_kernel(cond_ref, w_ref, b_ref, o_ref):
    s = cond_ref[...]
    s = s * jax.nn.sigmoid(s)
    o_ref[0] = _dot(s.astype(BF16), w_ref[0].astype(BF16)) + b_ref[0]


def _mods_call(cond, w_mod, b_mod):
    n_tiles = (N_MOD * D_MODEL) // MODS_COLS
    return pl.pallas_call(
        _mods_kernel,
        grid=(DEPTH, n_tiles),
        in_specs=[
            pl.BlockSpec((MODS_ROWS, D_MODEL), lambda l, j: (0, 0)),
            pl.BlockSpec((1, D_MODEL, MODS_COLS), lambda l, j: (l, 0, j)),
            pl.BlockSpec((1, 1, MODS_COLS), lambda l, j: (l, 0, j)),
        ],
        out_specs=pl.BlockSpec((1, MODS_ROWS, MODS_COLS), lambda l, j: (l, 0, j)),
        out_shape=jax.ShapeDtypeStruct((DEPTH, MODS_ROWS, N_MOD * D_MODEL), F32),
        compiler_params=pltpu.CompilerParams(dimension_semantics=("arbitrary", "arbitrary"),
                                             vmem_limit_bytes=VMEM_LIMIT),
        name="mods",
    )(cond, w_mod, b_mod.reshape(DEPTH, 1, N_MOD * D_MODEL))


def _window_sums(x, pos, length, unit, max_half):
    fwd = x
    bwd = jnp.where(pos >= 1, _shift_rows(x, -unit), 0.0)
    sums = {1: fwd + bwd}
    k = 1
    while k < max_half:
        fwd = fwd + jnp.where(pos + k < length, _shift_rows(fwd, k * unit), 0.0)
        bwd = bwd + jnp.where(pos - k >= 0, _shift_rows(bwd, -k * unit), 0.0)
        k *= 2
        sums[k] = fwd + bwd
    return sums


def _pool_axis(x, pos, length, unit, halves):
    sums = _window_sums(x, pos, length, unit, max(halves))
    low = lax.broadcasted_iota(jnp.int32, x.shape, 1) < POOL_GROUP_DIM
    sel = jnp.where(low, sums[halves[0]], sums[halves[1]])
    half = jnp.where(low, halves[0], halves[1])
    cnt = jnp.minimum(pos + half, length) - jnp.maximum(pos - half, 0)
    return sel, cnt.astype(F32)


def _mixer_kernel(x_ref, mods_ref, g1_ref, win_ref, cw_ref, cb_ref, wr_ref, wi_ref, bg_ref, lam_ref, h0_ref,
                  pw_ref, ps_ref, cdft_ref, fw_ref, dft_ref, wout_ref,
                  xo_ref, st_ref,
                  wg_ref, pwd_ref, u_ref, xc_ref, gate_ref, scan_ref, cat_ref, hn_ref, y1_ref, pool_ref, *, seq_len):
    n_seq = TILE // seq_len
    chunks_per_seq = seq_len // CHUNK
    mods = mods_ref[0, 0]

    @pl.when(pl.program_id(0) == 0)
    def _():
        wg_ref[...] = jnp.zeros_like(wg_ref)
        pwd_ref[...] = jnp.zeros_like(pwd_ref)
        for hd in range(HEADS_PER_BLOCK):
            rows = slice(hd * LRU_HEAD_DIM, (hd + 1) * LRU_HEAD_DIM)
            pwd_ref[rows, rows] = pw_ref[0, hd].astype(BF16)
            for d in range(2):
                for gate, w_ref in enumerate((wr_ref, wi_ref)):
                    for blk in range(2):
                        w = w_ref[0, d, blk * HEADS_PER_BLOCK + hd]
                        cols = slice(gate * BLOCK + hd * LRU_HEAD_DIM, gate * BLOCK + (hd + 1) * LRU_HEAD_DIM)
                        wg_ref[d * 2 + blk, rows, cols] = (0.5 * w).astype(BF16)

    def project(lo, hi, rows=slice(0, TILE)):
        for c0 in range(lo, hi, PROJ_COLS):
            u_ref[rows, c0:c0 + PROJ_COLS] = _dot(hn_ref[rows, :], win_ref[0, :, c0:c0 + PROJ_COLS])

    for r0 in range(0, TILE, TILE // 2):
        rows = slice(r0, r0 + TILE // 2)
        x = x_ref[rows, :]
        ms = jnp.mean(x * x, axis=-1, keepdims=True)
        h = x * lax.rsqrt(ms + EPS) * g1_ref[0]
        hn_ref[rows, :] = (h * (1.0 + mods[1:2]) + mods[0:1]).astype(BF16)
        project(0, D_RNN, rows)
    project(2 * D_RNN, D_IN)

    pos = _row_iota((TILE, LANES)) & (seq_len - 1)
    taps = [(k, k - CONV_LEFT) for k in range(CONV_WIDTH) if k != CONV_LEFT]
    valid = {d: (pos + d >= 0) & (pos + d < seq_len) for _, d in taps}
    for s in range(N_SLAB):
        lanes = slice(s * LANES, (s + 1) * LANES)
        xr = u_ref[:, lanes]
        acc = cb_ref[0, :, lanes] + xr * cw_ref[0, CONV_LEFT:CONV_LEFT + 1, lanes]
        for k, d in taps:
            acc = acc + jnp.where(valid[d], _shift_rows(xr, d), 0.0) * cw_ref[0, k:k + 1, lanes]
        xc_ref[:, lanes] = acc


    def pooled_slab(k):
        halves = tuple(w // 2 for w in POOL_WINDOWS[2 * k:2 * k + 2])
        xp = u_ref[:, 2 * D_RNN + k * LANES:2 * D_RNN + (k + 1) * LANES]
        tok = _row_iota((TILE, LANES))
        if seq_len == TILE:
            n_rows = TILE // GRID_W
            col_sum, col_cnt = _pool_axis(xp, tok & (GRID_W - 1), GRID_W, 1, halves)
            win_sum, row_cnt = _pool_axis(col_sum, lax.shift_right_logical(tok, GRID_W.bit_length() - 1),
                                          n_rows, GRID_W, halves)
            cnt = row_cnt * col_cnt
        else:
            win_sum, cnt = _pool_axis(xp, tok & (seq_len - 1), seq_len, 1, halves)
        pool_ref[:, k * LANES:(k + 1) * LANES] = (win_sum / cnt - xp).astype(BF16)

    def pooling_group():
        for k in range(D_POOL // LANES):
            pooled_slab(k)
        cat_ref[:, D_RNN:D_RNN + D_POOL] = (_dot(pool_ref[...], pwd_ref[...]) * ps_ref[0]).astype(BF16)

    def channel_dft():
        xf = u_ref[:, 2 * D_RNN + D_POOL:D_IN].astype(BF16)
        y1_ref[...] = _dot(xf, cdft_ref[...]).astype(BF16)

    seq_group = min(N_MXU, n_seq)

    def fourier_rows(q):
        rows = [slice((q + k) * seq_len, (q + k + 1) * seq_len) for k in range(seq_group)]
        y_cos = jnp.concatenate([y1_ref[r, 0:D_FOURIER] for r in rows], axis=1)
        y_sin = jnp.concatenate([y1_ref[r, D_FOURIER:] for r in rows], axis=1)
        four = (_dot(dft_ref[0], y_cos) + _dot(dft_ref[1], y_sin)).astype(BF16)
        for k, r in enumerate(rows):
            cat_ref[r, D_RNN + D_POOL:] = _dot(four[:, k * D_FOURIER:(k + 1) * D_FOURIER], fw_ref[0]).astype(BF16)

    side_work = iter(
        [channel_dft, pooling_group, functools.partial(project, D_RNN, 2 * D_RNN)]
        + [functools.partial(fourier_rows, q) for q in range(0, n_seq, seq_group)])

    xcb = xc_ref[...].astype(BF16)
    for d in range(2):
        for blk in range(2):
            pre = _dot(xcb[:, blk * BLOCK:(blk + 1) * BLOCK], wg_ref[d * 2 + blk])
            for gate in range(2):
                for half in range(2):
                    lanes = slice(gate * BLOCK + half * LANES, gate * BLOCK + (half + 1) * LANES)
                    gate_ref[gate * N_SLAB + 2 * blk + half, 0:TILE, :] = pre[:, lanes]
        lam = lam_ref[0, d:d + 1, :]
        softplus_neg_lam = jnp.maximum(-lam, 0.0) + jnp.log1p(jnp.exp(-jnp.abs(lam)))
        half_rate = (-0.5 * LRU_C) * softplus_neg_lam
        hb_r = 0.5 * bg_ref[0, 2 * d:2 * d + 1, :]
        hb_i = 0.5 * bg_ref[0, 2 * d + 1:2 * d + 2, :]
        for c in range(SUBLANES):
            r0 = c * CHUNK
            for s in range(N_SLAB):
                lanes = slice(s * LANES, (s + 1) * LANES)
                t_r = jnp.tanh(gate_ref[s, pl.ds(r0, CHUNK), :] + hb_r[:, lanes])
                t_i = jnp.tanh(gate_ref[N_SLAB + s, pl.ds(r0, CHUNK), :] + hb_i[:, lanes])
                rate = half_rate[:, lanes]
                a = jnp.exp(rate + rate * t_r)
                m2 = 1.0 - a * a
                root = jnp.where(m2 > 0.0, m2 * lax.rsqrt(m2), 0.0)
                b = root * ((0.5 + 0.5 * t_i) * xc_ref[pl.ds(r0, CHUNK), lanes])
                scan_ref[2 * d, s, pl.ds(c, CHUNK, stride=SUBLANES), :] = a
                scan_ref[2 * d + 1, s, pl.ds(c, CHUNK, stride=SUBLANES), :] = b
            next(side_work, lambda: None)()
    for rest in side_work:
        rest()

    def load(arr, s, t):
        return scan_ref[arr, s, pl.ds(pl.multiple_of(t * SUBLANES, SUBLANES), SUBLANES), :]

    def compose(first, second):
        (a1, b1), (a2, b2) = first, second
        return a2 * a1, a2 * b1 + b2

    def group_steps(arr, s, t0, sign):
        return [(load(arr, s, t0 + sign * k), load(arr + 1, s, t0 + sign * k)) for k in range(SCAN_GROUP)]

    def pass1(i, carry):
        hf, pf, hb, pb = carry
        nhf, npf, nhb, npb = [], [], [], []
        for s in range(N_SLAB):
            f = group_steps(0, s, i * SCAN_GROUP, 1)
            af, bf = compose(compose(f[0], f[1]), compose(f[2], f[3]))
            nhf.append(af * hf[s] + bf)
            npf.append(af * pf[s])
            g = group_steps(2, s, CHUNK - 1 - i * SCAN_GROUP, -1)
            ab, bb = compose(compose(g[0], g[1]), compose(g[2], g[3]))
            nhb.append(ab * hb[s] + bb)
            npb.append(ab * pb[s])
        return tuple(nhf), tuple(npf), tuple(nhb), tuple(npb)

    zeros = tuple(jnp.zeros((SUBLANES, LANES), F32) for _ in range(N_SLAB))
    ones = tuple(jnp.ones((SUBLANES, LANES), F32) for _ in range(N_SLAB))
    ef, pf, eb, pb = lax.fori_loop(0, CHUNK // SCAN_GROUP, pass1, (zeros, ones, zeros, ones),
                                   unroll=SCAN_UNROLL)

    row = _row_iota((SUBLANES, LANES))
    seq_first = (row & (chunks_per_seq - 1)) == 0
    seq_last = (row & (chunks_per_seq - 1)) == chunks_per_seq - 1
    init_f, init_b = [], []
    for s in range(N_SLAB):
        h0f = jnp.broadcast_to(h0_ref[0, 0, 0:1, s * LANES:(s + 1) * LANES], (SUBLANES, LANES))
        h0b = jnp.broadcast_to(h0_ref[0, 0, 1:2, s * LANES:(s + 1) * LANES], (SUBLANES, LANES))
        cf, cb = h0f, h0b
        for _ in range(chunks_per_seq - 1):
            cf = jnp.where(seq_first, h0f, pltpu.roll(ef[s] + pf[s] * cf, 1, axis=0))
            cb = jnp.where(seq_last, h0b, pltpu.roll(eb[s] + pb[s] * cb, SUBLANES - 1, axis=0))
        init_f.append(cf)
        init_b.append(cb)

    def run_group(steps, h, out_slab, t0, sign):
        for k in range(0, SCAN_GROUP, 2):
            a0, b0 = steps[k]
            a01, b01 = compose(steps[k], steps[k + 1])
            h_even = a0 * h + b0
            h = a01 * h + b01
            for step, val in ((k, h_even), (k + 1, h)):
                t = t0 + sign * step
                gate_ref[out_slab, pl.ds(pl.multiple_of(t * SUBLANES, SUBLANES), SUBLANES), :] = val
        return h

    def pass2(i, carry):
        hf, hb = carry
        nhf, nhb = [], []
        for s in range(N_SLAB):
            tf0 = i * SCAN_GROUP
            tb0 = CHUNK - 1 - i * SCAN_GROUP
            nhf.append(run_group(group_steps(0, s, tf0, 1), hf[s], s, tf0, 1))
            nhb.append(run_group(group_steps(2, s, tb0, -1), hb[s], N_SLAB + s, tb0, -1))
        return tuple(nhf), tuple(nhb)

    lax.fori_loop(0, CHUNK // SCAN_GROUP, pass2, (tuple(init_f), tuple(init_b)), unroll=SCAN_UNROLL)

    if st_ref is not None:
        even = (_row_iota((SUBLANES, LANES)) & 1) == 0
        for s in range(N_SLAB):
            last_f = gate_ref[s, pl.ds((CHUNK - 1) * SUBLANES, SUBLANES), :]
            first_b = gate_ref[N_SLAB + s, pl.ds(0, SUBLANES), :]
            st_ref[0, :, s * LANES:(s + 1) * LANES] = jnp.where(
                even, pltpu.roll(last_f, SUBLANES - 1, axis=0), pltpu.roll(first_b, 1, axis=0))

    for c in range(SUBLANES):
        r0 = c * CHUNK
        for s in range(N_SLAB):
            hsum = (gate_ref[s, pl.ds(c, CHUNK, stride=SUBLANES), :]
                    + gate_ref[N_SLAB + s, pl.ds(c, CHUNK, stride=SUBLANES), :])
            xg = u_ref[pl.ds(r0, CHUNK), D_RNN + s * LANES:D_RNN + (s + 1) * LANES]
            cat_ref[pl.ds(r0, CHUNK), s * LANES:(s + 1) * LANES] = (hsum * jax.nn.gelu(xg)).astype(BF16)

    xo_ref[...] = x_ref[...] + mods[2:3] * _dot(cat_ref[...], wout_ref[0])


def _mixer_call(x, x_block0, n_tiles, mods, layer, cond_row0, p, h0, h0_index, dft, seq_len, with_state,
                out_rows, out_block0, fill=None):
    n_seq = TILE // seq_len
    once = pl.Buffered(1)
    lyr = lambda *shape: pl.BlockSpec((1,) + shape, lambda g: (layer,) + (0,) * len(shape), pipeline_mode=once)
    in_specs = [
        pl.BlockSpec((TILE, D_MODEL), lambda g: (g + x_block0, 0)),
        pl.BlockSpec((1, 1, N_MOD, D_MODEL), lambda g: (layer, cond_row0(g), 0, 0)),
        lyr(1, D_MODEL),
        lyr(D_MODEL, D_IN),
        lyr(CONV_WIDTH, D_RNN),
        lyr(1, D_RNN),
        lyr(2, N_LRU_HEADS, LRU_HEAD_DIM, LRU_HEAD_DIM),
        lyr(2, N_LRU_HEADS, LRU_HEAD_DIM, LRU_HEAD_DIM),
        lyr(4, D_RNN),
        lyr(2, D_RNN),
        pl.BlockSpec((1, 1, 2, D_RNN), h0_index),
        lyr(HEADS_PER_BLOCK, LRU_HEAD_DIM, LRU_HEAD_DIM),
        lyr(1, D_POOL),
        pl.BlockSpec((D_FOURIER, 2 * D_FOURIER), lambda g: (0, 0), pipeline_mode=once),
        lyr(D_FOURIER, D_FOURIER),
        pl.BlockSpec((2, seq_len, seq_len), lambda g: (0, 0, 0), pipeline_mode=once),
        lyr(D_MODEL, D_MODEL),
    ]
    n_in = len(in_specs)
    args = [x, mods, p['norm1_g'], p['w_in'], p['conv_w'], p['conv_b'], p['lru_wr'], p['lru_wi'], p['b_gate'],
            p['lam'], h0, p['pool_w'], p['pool_scale'], p['cdft'], p['fourier_w'], dft, p['w_out']]
    aliases = {}
    if fill is not None:
        in_specs.append(pl.BlockSpec(memory_space=pl.ANY))
        args.append(fill)
        aliases = {n_in: 0}
    out_specs = [pl.BlockSpec((TILE, D_MODEL), lambda g: (g + out_block0, 0))]
    out_shape = [jax.ShapeDtypeStruct((out_rows, D_MODEL), F32)]
    if with_state:
        out_specs.append(pl.BlockSpec((1, 2 * n_seq, D_RNN), lambda g: (g, 0, 0)))
        out_shape.append(jax.ShapeDtypeStruct((n_tiles, 2 * n_seq, D_RNN), F32))

    def body(*refs):
        ins, rest = refs[:n_in], refs[len(in_specs):]
        if with_state:
            _mixer_kernel(*ins, *rest, seq_len=seq_len)
        else:
            _mixer_kernel(*ins, rest[0], None, *rest[1:], seq_len=seq_len)

    return pl.pallas_call(
        body,
        grid=(n_tiles,),
        in_specs=in_specs,
        out_specs=out_specs,
        out_shape=out_shape,
        input_output_aliases=aliases,
        scratch_shapes=[
            pltpu.VMEM((4, BLOCK, 2 * BLOCK), BF16),
            pltpu.VMEM((D_POOL, D_POOL), BF16),
            pltpu.VMEM((TILE, D_IN), F32),
            pltpu.VMEM((TILE, D_RNN), F32),
            pltpu.VMEM((2 * N_SLAB, SLAB_ROWS, LANES), F32),
            pltpu.VMEM((4, N_SLAB, SLAB_ROWS, LANES), F32),
            pltpu.VMEM((TILE, D_MODEL), BF16),
            pltpu.VMEM((TILE, D_MODEL), BF16),
            pltpu.VMEM((TILE, 2 * D_FOURIER), BF16),
            pltpu.VMEM((TILE, D_POOL), BF16),
        ],
        compiler_params=pltpu.CompilerParams(dimension_semantics=("arbitrary",),
                                             vmem_limit_bytes=VMEM_LIMIT),
        name="mixer_ctx" if with_state else "mixer_lat",
    )(*args)


def _mlp_kernel(x_ref, mods_ref, g2_ref, w1_ref, w2_ref, fg_ref, o_ref, h2_ref, acc_ref, *, final):
    j = pl.program_id(1)
    last = pl.num_programs(1) - 1
    mods = mods_ref[0, 0]

    def hidden(h2):
        return jnp.square(jnp.maximum(_dot(h2, w1_ref[0].astype(BF16)), 0.0)).astype(BF16)

    def chunk_out(h2):
        return _dot(hidden(h2), w2_ref[0].astype(BF16))

    @pl.when(j == 0)
    def _():
        for r0 in range(0, TILE, TILE // 2):
            rows = slice(r0, r0 + TILE // 2)
            x = x_ref[rows, :]
            ms = jnp.mean(x * x, axis=-1, keepdims=True)
            h = x * lax.rsqrt(ms + EPS) * g2_ref[0]
            h2 = (h * (1.0 + mods[4:5]) + mods[3:4]).astype(BF16)
            h2_ref[rows, :] = h2
            acc_ref[rows, :] = chunk_out(h2)

    @pl.when((j > 0) & (j < last))
    def _():
        acc_ref[...] += chunk_out(h2_ref[...])

    @pl.when(j == last)
    def _():
        hid = hidden(h2_ref[...])
        for c0 in range(0, D_MODEL, PROJ_COLS):
            cols = slice(c0, c0 + PROJ_COLS)
            ff = acc_ref[:, cols] + _dot(hid, w2_ref[0, :, cols].astype(BF16))
            o_ref[:, cols] = x_ref[:, cols] + mods[5:6, cols] * ff
        if final:
            y = o_ref[...]
            ms = jnp.mean(y * y, axis=-1, keepdims=True)
            o_ref[...] = y * lax.rsqrt(ms + EPS) * fg_ref[...]


def _mlp_call(x, x_block0, n_tiles, mods, layer, cond_row0, p, final):
    return pl.pallas_call(
        functools.partial(_mlp_kernel, final=final),
        grid=(n_tiles, D_FF // FF_CHUNK),
        in_specs=[
            pl.BlockSpec((TILE, D_MODEL), lambda g, j: (g + x_block0, 0)),
            pl.BlockSpec((1, 1, N_MOD, D_MODEL), lambda g, j: (layer, cond_row0(g), 0, 0)),
            pl.BlockSpec((1, 1, D_MODEL), lambda g, j: (layer, 0, 0)),
            pl.BlockSpec((1, D_MODEL, FF_CHUNK), lambda g, j: (layer, 0, j)),
            pl.BlockSpec((1, FF_CHUNK, D_MODEL), lambda g, j: (layer, j, 0)),
            pl.BlockSpec((1, D_MODEL), lambda g, j: (0, 0)),
        ],
        out_specs=pl.BlockSpec((TILE, D_MODEL), lambda g, j: (g, 0)),
        out_shape=jax.ShapeDtypeStruct((n_tiles * TILE, D_MODEL), F32),
        scratch_shapes=[pltpu.VMEM((TILE, D_MODEL), BF16), pltpu.VMEM((TILE, D_MODEL), F32)],
        compiler_params=pltpu.CompilerParams(dimension_semantics=("arbitrary", "arbitrary"),
                                             vmem_limit_bytes=VMEM_LIMIT),
        name="mlp",
    )(x, mods, p['norm2_g'], p['mlp_w1'], p['mlp_w2'], p['final_g'])


def _dft_tables(n, scale):
    k = np.arange(n)
    ang = 2.0 * np.pi * ((k[:, None] * k[None, :]) % n) / n
    return np.cos(ang) * scale, np.sin(ang) * scale


def _seq_dft(seq_len):
    c, s = _dft_tables(seq_len, 1.0 / math.sqrt(seq_len))
    return jnp.asarray(np.stack([c, -s]), F32).astype(BF16)


def _channel_dft():
    c, s = _dft_tables(FOURIER_HEAD_DIM, 1.0 / math.sqrt(FOURIER_HEAD_DIM))
    eye = np.eye(N_FOURIER_HEADS)
    return jnp.asarray(np.concatenate([np.kron(eye, c), np.kron(eye, s)], axis=1), F32).astype(BF16)


def kernel(x_prompt, x_sample, state_rglru, c, c_ctx, norm1_g, norm2_g, final_g, w_mod, b_mod, w_in, conv_w, conv_b, lru_wr, lru_br, lru_wi, lru_bi, lru_lambda, pool_w, pool_scale, fourier_w, w_out, mlp_w1, mlp_w2):
    batch, seq, _ = x_prompt.shape
    dec_batch, dec_seq, _ = x_sample.shape
    assert TILE % seq == 0 and (batch * seq) % TILE == 0 and dec_seq == TILE

    p = {
        'norm1_g': norm1_g.reshape(DEPTH, 1, D_MODEL),
        'norm2_g': norm2_g.reshape(DEPTH, 1, D_MODEL),
        'final_g': final_g.reshape(1, D_MODEL),
        'w_in': w_in.astype(BF16),
        'conv_w': conv_w,
        'conv_b': conv_b.reshape(DEPTH, 1, D_RNN),
        'lru_wr': lru_wr,
        'lru_wi': lru_wi,
        'b_gate': jnp.stack([lru_br, lru_bi], axis=2).reshape(DEPTH, 4, D_RNN),
        'lam': lru_lambda,
        'pool_w': pool_w,
        'pool_scale': pool_scale.reshape(DEPTH, 1, D_POOL),
        'cdft': _channel_dft(),
        'fourier_w': fourier_w.astype(BF16),
        'w_out': w_out.astype(BF16),
        'mlp_w1': mlp_w1,
        'mlp_w2': mlp_w2,
    }

    cond = jnp.concatenate([c_ctx[None], c, jnp.zeros((MODS_ROWS - 1 - dec_batch, D_MODEL), F32)], axis=0)
    mods = _mods_call(cond, w_mod, b_mod).reshape(DEPTH, MODS_ROWS, N_MOD, D_MODEL)

    ctx_row = lambda g: 0
    lat_row = lambda g: 1 + g
    h0_ctx = jnp.zeros((1, 1, 2, D_RNN), F32)
    dft_ctx = _seq_dft(seq)
    dft_lat = _seq_dft(dec_seq)

    n_ctx = batch * seq // TILE
    n_lat = dec_batch * dec_seq // TILE
    rows_all = (n_ctx + n_lat) * TILE
    all_row = lambda g: jnp.maximum(g - (n_ctx - 1), 0)
    x_ctx, ctx_block0 = x_prompt.reshape(batch * seq, D_MODEL), 0
    x_lat, lat_block0 = x_sample.reshape(dec_batch * dec_seq, D_MODEL), 0
    states = []
    for l in range(DEPTH):
        xa, st = _mixer_call(x_ctx, ctx_block0, n_ctx, mods, l, ctx_row, p, h0_ctx, lambda g: (0, 0, 0, 0), dft_ctx,
                             seq, True, rows_all, 0)
        states.append(st.reshape(batch, 2, D_RNN))
        (xa,) = _mixer_call(x_lat, lat_block0, n_lat, mods, l, lat_row, p, state_rglru,
                            lambda g, l=l: (g, l, 0, 0), dft_lat, dec_seq, False, rows_all, n_ctx, fill=xa)
        if l < DEPTH - 1:
            xa = _mlp_call(xa, 0, n_ctx + n_lat, mods, l, all_row, p, False)
            x_ctx, ctx_block0, x_lat, lat_block0 = xa, 0, xa, n_ctx
        else:
            xc = _mlp_call(xa, 0, n_ctx, mods, l, ctx_row, p, True)
            xs = _mlp_call(xa, n_ctx, n_lat, mods, l, lat_row, p, True)

    y_prompt = xc.reshape(batch, seq, D_MODEL)
    y_sample = xs.reshape(dec_batch, dec_seq, D_MODEL)
    new_state = jnp.stack(states, axis=1)
    return (y_prompt, y_sample, new_state)
```

```python
import functools
import math

import numpy as np
import jax
import jax.numpy as jnp
from jax import lax
from jax.experimental import pallas as pl
from jax.experimental.pallas import tpu as pltpu

D_MODEL = 1024
DEPTH = 4
GRID_W = 64
D_RNN = 512
N_LRU_HEADS = 8
LRU_HEAD_DIM = 64
LRU_C = 8.0
CONV_WIDTH = 4
CONV_LEFT = 2
D_POOL = 256
POOL_WINDOWS = (2, 4, 8, 16)
POOL_GROUP_DIM = D_POOL // len(POOL_WINDOWS)
D_FOURIER = 256
N_FOURIER_HEADS = 4
FOURIER_HEAD_DIM = 64
D_IN = 2 * D_RNN + D_POOL + D_FOURIER
D_FF = 4 * D_MODEL
N_MOD = 6
EPS = 1e-6

LANES = 128
SUBLANES = 8
TILE = 1024
CHUNK = TILE // SUBLANES
N_SLAB = D_RNN // LANES
SLAB_ROWS = TILE + SUBLANES
SCAN_GROUP = 4
SCAN_UNROLL = 4
HEADS_PER_BLOCK = 4
BLOCK = HEADS_PER_BLOCK * LRU_HEAD_DIM
N_MXU = 2
MXU_TILE = 256
PROJ_COLS = N_MXU * MXU_TILE
FF_CHUNK = 1024
MODS_ROWS = 8
MODS_COLS = 2048
VMEM_LIMIT = 60 * 1024 * 1024

F32 = jnp.float32
BF16 = jnp.bfloat16


def _dot(a, b):
    return jnp.dot(a, b, preferred_element_type=F32)


def _row_iota(shape):
    return lax.broadcasted_iota(jnp.int32, shape, 0)


def _shift_rows(x, d):
    n = x.shape[0]
    return pltpu.roll(x, (-d) % n, axis=0)


def _mods_kernel(cond_ref, w_ref, b_ref, o_ref):
    s = cond_ref[...]
    s = s * jax.nn.sigmoid(s)
    o_ref[0] = _dot(s.astype(BF16), w_ref[0].astype(BF16)) + b_ref[0]


def _mods_call(cond, w_mod, b_mod):
    n_tiles = (N_MOD * D_MODEL) // MODS_COLS
    return pl.pallas_call(
        _mods_kernel,
        grid=(DEPTH, n_tiles),
        in_specs=[
            pl.BlockSpec((MODS_ROWS, D_MODEL), lambda l, j: (0, 0)),
            pl.BlockSpec((1, D_MODEL, MODS_COLS), lambda l, j: (l, 0, j)),
            pl.BlockSpec((1, 1, MODS_COLS), lambda l, j: (l, 0, j)),
        ],
        out_specs=pl.BlockSpec((1, MODS_ROWS, MODS_COLS), lambda l, j: (l, 0, j)),
        out_shape=jax.ShapeDtypeStruct((DEPTH, MODS_ROWS, N_MOD * D_MODEL), F32),
        compiler_params=pltpu.CompilerParams(dimension_semantics=("arbitrary", "arbitrary"),
                                             vmem_limit_bytes=VMEM_LIMIT),
        name="mods",
    )(cond, w_mod, b_mod.reshape(DEPTH, 1, N_MOD * D_MODEL))


def _window_sums(x, pos, length, unit, max_half):
    fwd = x
    bwd = jnp.where(pos >= 1, _shift_rows(x, -unit), 0.0)
    sums = {1: fwd + bwd}
    k = 1
    while k < max_half:
        fwd = fwd + jnp.where(pos + k < length, _shift_rows(fwd, k * unit), 0.0)
        bwd = bwd + jnp.where(pos - k >= 0, _shift_rows(bwd, -k * unit), 0.0)
        k *= 2
        sums[k] = fwd + bwd
    return sums


def _pool_axis(x, pos, length, unit, halves):
    sums = _window_sums(x, pos, length, unit, max(halves))
    low = lax.broadcasted_iota(jnp.int32, x.shape, 1) < POOL_GROUP_DIM
    sel = jnp.where(low, sums[halves[0]], sums[halves[1]])
    half = jnp.where(low, halves[0], halves[1])
    cnt = jnp.minimum(pos + half, length) - jnp.maximum(pos - half, 0)
    return sel, cnt.astype(F32)


def _mixer_kernel(x_ref, mods_ref, g1_ref, win_ref, cw_ref, cb_ref, wr_ref, wi_ref, bg_ref, lam_ref, h0_ref,
                  pw_ref, ps_ref, cdft_ref, fw_ref, dft_ref, wout_ref,
                  xo_ref, st_ref,
                  wg_ref, pwd_ref, u_ref, xc_ref, gate_ref, scan_ref, cat_ref, hn_ref, y1_ref, pool_ref, *, seq_len):
    n_seq = TILE // seq_len
    chunks_per_seq = seq_len // CHUNK
    mods = mods_ref[0, 0]

    @pl.when(pl.program_id(0) == 0)
    def _():
        wg_ref[...] = jnp.zeros_like(wg_ref)
        pwd_ref[...] = jnp.zeros_like(pwd_ref)
        for hd in range(HEADS_PER_BLOCK):
            rows = slice(hd * LRU_HEAD_DIM, (hd + 1) * LRU_HEAD_DIM)
            pwd_ref[rows, rows] = pw_ref[0, hd].astype(BF16)
            for d in range(2):
                for gate, w_ref in enumerate((wr_ref, wi_ref)):
                    for blk in range(2):
                        w = w_ref[0, d, blk * HEADS_PER_BLOCK + hd]
                        cols = slice(gate * BLOCK + hd * LRU_HEAD_DIM, gate * BLOCK + (hd + 1) * LRU_HEAD_DIM)
                        wg_ref[d * 2 + blk, rows, cols] = (0.5 * w).astype(BF16)

    def project(lo, hi, rows=slice(0, TILE)):
        for c0 in range(lo, hi, PROJ_COLS):
            u_ref[rows, c0:c0 + PROJ_COLS] = _dot(hn_ref[rows, :], win_ref[0, :, c0:c0 + PROJ_COLS])

    for r0 in range(0, TILE, TILE // 2):
        rows = slice(r0, r0 + TILE // 2)
        x = x_ref[rows, :]
        ms = jnp.mean(x * x, axis=-1, keepdims=True)
        h = x * lax.rsqrt(ms + EPS) * g1_ref[0]
        hn_ref[rows, :] = (h * (1.0 + mods[1:2]) + mods[0:1]).astype(BF16)
        project(0, D_RNN, rows)
    project(2 * D_RNN, D_IN)

    pos = _row_iota((TILE, LANES)) & (seq_len - 1)
    taps = [(k, k - CONV_LEFT) for k in range(CONV_WIDTH) if k != CONV_LEFT]
    valid = {d: (pos + d >= 0) & (pos + d < seq_len) for _, d in taps}
    for s in range(N_SLAB):
        lanes = slice(s * LANES, (s + 1) * LANES)
        xr = u_ref[:, lanes]
        acc = cb_ref[0, :, lanes] + xr * cw_ref[0, CONV_LEFT:CONV_LEFT + 1, lanes]
        for k, d in taps:
            acc = acc + jnp.where(valid[d], _shift_rows(xr, d), 0.0) * cw_ref[0, k:k + 1, lanes]
        xc_ref[:, lanes] = acc


    def pooled_slab(k):
        halves = tuple(w // 2 for w in POOL_WINDOWS[2 * k:2 * k + 2])
        xp = u_ref[:, 2 * D_RNN + k * LANES:2 * D_RNN + (k + 1) * LANES]
        tok = _row_iota((TILE, LANES))
        if seq_len == TILE:
            n_rows = TILE // GRID_W
            col_sum, col_cnt = _pool_axis(xp, tok & (GRID_W - 1), GRID_W, 1, halves)
            win_sum, row_cnt = _pool_axis(col_sum, lax.shift_right_logical(tok, GRID_W.bit_length() - 1),
                                          n_rows, GRID_W, halves)
            cnt = row_cnt * col_cnt
        else:
            win_sum, cnt = _pool_axis(xp, tok & (seq_len - 1), seq_len, 1, halves)
        pool_ref[:, k * LANES:(k + 1) * LANES] = (win_sum / cnt - xp).astype(BF16)

    def pooling_group():
        for k in range(D_POOL // LANES):
            pooled_slab(k)
        cat_ref[:, D_RNN:D_RNN + D_POOL] = (_dot(pool_ref[...], pwd_ref[...]) * ps_ref[0]).astype(BF16)

    def channel_dft():
        xf = u_ref[:, 2 * D_RNN + D_POOL:D_IN].astype(BF16)
        y1_ref[...] = _dot(xf, cdft_ref[...]).astype(BF16)

    seq_group = min(N_MXU, n_seq)

    def fourier_rows(q):
        rows = [slice((q + k) * seq_len, (q + k + 1) * seq_len) for k in range(seq_group)]
        y_cos = jnp.concatenate([y1_ref[r, 0:D_FOURIER] for r in rows], axis=1)
        y_sin = jnp.concatenate([y1_ref[r, D_FOURIER:] for r in rows], axis=1)
        four = (_dot(dft_ref[0], y_cos) + _dot(dft_ref[1], y_sin)).astype(BF16)
        for k, r in enumerate(rows):
            cat_ref[r, D_RNN + D_POOL:] = _dot(four[:, k * D_FOURIER:(k + 1) * D_FOURIER], fw_ref[0]).astype(BF16)

    side_work = iter(
        [channel_dft, pooling_group, functools.partial(project, D_RNN, 2 * D_RNN)]
        + [functools.partial(fourier_rows, q) for q in range(0, n_seq, seq_group)])

    xcb = xc_ref[...].astype(BF16)
    for d in range(2):
        for blk in range(2):
            pre = _dot(xcb[:, blk * BLOCK:(blk + 1) * BLOCK], wg_ref[d * 2 + blk])
            for gate in range(2):
                for half in range(2):
                    lanes = slice(gate * BLOCK + half * LANES, gate * BLOCK + (half + 1) * LANES)
                    gate_ref[gate * N_SLAB + 2 * blk + half, 0:TILE, :] = pre[:, lanes]
        lam = lam_ref[0, d:d + 1, :]
        softplus_neg_lam = jnp.maximum(-lam, 0.0) + jnp.log1p(jnp.exp(-jnp.abs(lam)))
        half_rate = (-0.5 * LRU_C) * softplus_neg_lam
        hb_r = 0.5 * bg_ref[0, 2 * d:2 * d + 1, :]
        hb_i = 0.5 * bg_ref[0, 2 * d + 1:2 * d + 2, :]
        for c in range(SUBLANES):
            r0 = c * CHUNK
            for s in range(N_SLAB):
                lanes = slice(s * LANES, (s + 1) * LANES)
                t_r = jnp.tanh(gate_ref[s, pl.ds(r0, CHUNK), :] + hb_r[:, lanes])
                t_i = jnp.tanh(gate_ref[N_SLAB + s, pl.ds(r0, CHUNK), :] + hb_i[:, lanes])
                rate = half_rate[:, lanes]
                a = jnp.exp(rate + rate * t_r)
                m2 = 1.0 - a * a
                root = jnp.where(m2 > 0.0, m2 * lax.rsqrt(m2), 0.0)
                b = root * ((0.5 + 0.5 * t_i) * xc_ref[pl.ds(r0, CHUNK), lanes])
                scan_ref[2 * d, s, pl.ds(c, CHUNK, stride=SUBLANES), :] = a
                scan_ref[2 * d + 1, s, pl.ds(c, CHUNK, stride=SUBLANES), :] = b
            next(side_work, lambda: None)()
    for rest in side_work:
        rest()

    def load(arr, s, t):
        return scan_ref[arr, s, pl.ds(pl.multiple_of(t * SUBLANES, SUBLANES), SUBLANES), :]

    def compose(first, second):
        (a1, b1), (a2, b2) = first, second
        return a2 * a1, a2 * b1 + b2

    def group_steps(arr, s, t0, sign):
        return [(load(arr, s, t0 + sign * k), load(arr + 1, s, t0 + sign * k)) for k in range(SCAN_GROUP)]

    def pass1(i, carry):
        hf, pf, hb, pb = carry
        nhf, npf, nhb, npb = [], [], [], []
        for s in range(N_SLAB):
            f = group_steps(0, s, i * SCAN_GROUP, 1)
            af, bf = compose(compose(f[0], f[1]), compose(f[2], f[3]))
            nhf.append(af * hf[s] + bf)
            npf.append(af * pf[s])
            g = group_steps(2, s, CHUNK - 1 - i * SCAN_GROUP, -1)
            ab, bb = compose(compose(g[0], g[1]), compose(g[2], g[3]))
            nhb.append(ab * hb[s] + bb)
            npb.append(ab * pb[s])
        return tuple(nhf), tuple(npf), tuple(nhb), tuple(npb)

    zeros = tuple(jnp.zeros((SUBLANES, LANES), F32) for _ in range(N_SLAB))
    ones = tuple(jnp.ones((SUBLANES, LANES), F32) for _ in range(N_SLAB))
    ef, pf, eb, pb = lax.fori_loop(0, CHUNK // SCAN_GROUP, pass1, (zeros, ones, zeros, ones),
                                   unroll=SCAN_UNROLL)

    row = _row_iota((SUBLANES, LANES))
    seq_first = (row & (chunks_per_seq - 1)) == 0
    seq_last = (row & (chunks_per_seq - 1)) == chunks_per_seq - 1
    init_f, init_b = [], []
    for s in range(N_SLAB):
        h0f = jnp.broadcast_to(h0_ref[0, 0, 0:1, s * LANES:(s + 1) * LANES], (SUBLANES, LANES))
        h0b = jnp.broadcast_to(h0_ref[0, 0, 1:2, s * LANES:(s + 1) * LANES], (SUBLANES, LANES))
        cf, cb = h0f, h0b
        for _ in range(chunks_per_seq - 1):
            cf = jnp.where(seq_first, h0f, pltpu.roll(ef[s] + pf[s] * cf, 1, axis=0))
            cb = jnp.where(seq_last, h0b, pltpu.roll(eb[s] + pb[s] * cb, SUBLANES - 1, axis=0))
        init_f.append(cf)
        init_b.append(cb)

    def run_group(steps, h, out_slab, t0, sign):
        for k in range(0, SCAN_GROUP, 2):
            a0, b0 = steps[k]
            a01, b01 = compose(steps[k], steps[k + 1])
            h_even = a0 * h + b0
            h = a01 * h + b01
            for step, val in ((k, h_even), (k + 1, h)):
                t = t0 + sign * step
                gate_ref[out_slab, pl.ds(pl.multiple_of(t * SUBLANES, SUBLANES), SUBLANES), :] = val
        return h

    def pass2(i, carry):
        hf, hb = carry
        nhf, nhb = [], []
        for s in range(N_SLAB):
            tf0 = i * SCAN_GROUP
            tb0 = CHUNK - 1 - i * SCAN_GROUP
            nhf.append(run_group(group_steps(0, s, tf0, 1), hf[s], s, tf0, 1))
            nhb.append(run_group(group_steps(2, s, tb0, -1), hb[s], N_SLAB + s, tb0, -1))
        return tuple(nhf), tuple(nhb)

    lax.fori_loop(0, CHUNK // SCAN_GROUP, pass2, (tuple(init_f), tuple(init_b)), unroll=SCAN_UNROLL)

    if st_ref is not None:
        even = (_row_iota((SUBLANES, LANES)) & 1) == 0
        for s in range(N_SLAB):
            last_f = gate_ref[s, pl.ds((CHUNK - 1) * SUBLANES, SUBLANES), :]
            first_b = gate_ref[N_SLAB + s, pl.ds(0, SUBLANES), :]
            st_ref[0, :, s * LANES:(s + 1) * LANES] = jnp.where(
                even, pltpu.roll(last_f, SUBLANES - 1, axis=0), pltpu.roll(first_b, 1, axis=0))

    for c in range(SUBLANES):
        r0 = c * CHUNK
        for s in range(N_SLAB):
            hsum = (gate_ref[s, pl.ds(c, CHUNK, stride=SUBLANES), :]
                    + gate_ref[N_SLAB + s, pl.ds(c, CHUNK, stride=SUBLANES), :])
            xg = u_ref[pl.ds(r0, CHUNK), D_RNN + s * LANES:D_RNN + (s + 1) * LANES]
            cat_ref[pl.ds(r0, CHUNK), s * LANES:(s + 1) * LANES] = (hsum * jax.nn.gelu(xg)).astype(BF16)

    xo_ref[...] = x_ref[...] + mods[2:3] * _dot(cat_ref[...], wout_ref[0])


def _mixer_call(x, x_block0, n_tiles, mods, layer, cond_row0, p, h0, h0_index, dft, seq_len, with_state,
                out_rows, out_block0, fill=None):
    n_seq = TILE // seq_len
    once = pl.Buffered(1)
    lyr = lambda *shape: pl.BlockSpec((1,) + shape, lambda g: (layer,) + (0,) * len(shape), pipeline_mode=once)
    in_specs = [
        pl.BlockSpec((TILE, D_MODEL), lambda g: (g + x_block0, 0)),
        pl.BlockSpec((1, 1, N_MOD, D_MODEL), lambda g: (layer, cond_row0(g), 0, 0)),
        lyr(1, D_MODEL),
        lyr(D_MODEL, D_IN),
        lyr(CONV_WIDTH, D_RNN),
        lyr(1, D_RNN),
        lyr(2, N_LRU_HEADS, LRU_HEAD_DIM, LRU_HEAD_DIM),
        lyr(2, N_LRU_HEADS, LRU_HEAD_DIM, LRU_HEAD_DIM),
        lyr(4, D_RNN),
        lyr(2, D_RNN),
        pl.BlockSpec((1, 1, 2, D_RNN), h0_index),
        lyr(HEADS_PER_BLOCK, LRU_HEAD_DIM, LRU_HEAD_DIM),
        lyr(1, D_POOL),
        pl.BlockSpec((D_FOURIER, 2 * D_FOURIER), lambda g: (0, 0), pipeline_mode=once),
        lyr(D_FOURIER, D_FOURIER),
        pl.BlockSpec((2, seq_len, seq_len), lambda g: (0, 0, 0), pipeline_mode=once),
        lyr(D_MODEL, D_MODEL),
    ]
    n_in = len(in_specs)
    args = [x, mods, p['norm1_g'], p['w_in'], p['conv_w'], p['conv_b'], p['lru_wr'], p['lru_wi'], p['b_gate'],
            p['lam'], h0, p['pool_w'], p['pool_scale'], p['cdft'], p['fourier_w'], dft, p['w_out']]
    aliases = {}
    if fill is not None:
        in_specs.append(pl.BlockSpec(memory_space=pl.ANY))
        args.append(fill)
        aliases = {n_in: 0}
    out_specs = [pl.BlockSpec((TILE, D_MODEL), lambda g: (g + out_block0, 0))]
    out_shape = [jax.ShapeDtypeStruct((out_rows, D_MODEL), F32)]
    if with_state:
        out_specs.append(pl.BlockSpec((1, 2 * n_seq, D_RNN), lambda g: (g, 0, 0)))
        out_shape.append(jax.ShapeDtypeStruct((n_tiles, 2 * n_seq, D_RNN), F32))

    def body(*refs):
        ins, rest = refs[:n_in], refs[len(in_specs):]
        if with_state:
            _mixer_kernel(*ins, *rest, seq_len=seq_len)
        else:
            _mixer_kernel(*ins, rest[0], None, *rest[1:], seq_len=seq_len)

    return pl.pallas_call(
        body,
        grid=(n_tiles,),
        in_specs=in_specs,
        out_specs=out_specs,
        out_shape=out_shape,
        input_output_aliases=aliases,
        scratch_shapes=[
            pltpu.VMEM((4, BLOCK, 2 * BLOCK), BF16),
            pltpu.VMEM((D_POOL, D_POOL), BF16),
            pltpu.VMEM((TILE, D_IN), F32),
            pltpu.VMEM((TILE, D_RNN), F32),
            pltpu.VMEM((2 * N_SLAB, SLAB_ROWS, LANES), F32),
            pltpu.VMEM((4, N_SLAB, SLAB_ROWS, LANES), F32),
            pltpu.VMEM((TILE, D_MODEL), BF16),
            pltpu.VMEM((TILE, D_MODEL), BF16),
            pltpu.VMEM((TILE, 2 * D_FOURIER), BF16),
            pltpu.VMEM((TILE, D_POOL), BF16),
        ],
        compiler_params=pltpu.CompilerParams(dimension_semantics=("arbitrary",),
                                             vmem_limit_bytes=VMEM_LIMIT),
        name="mixer_ctx" if with_state else "mixer_lat",
    )(*args)


def _mlp_kernel(x_ref, mods_ref, g2_ref, w1_ref, w2_ref, fg_ref, *rest, n_first):
    if n_first:
        first_ref, second_ref, h2_ref, acc_ref, o_ref = rest
    else:
        o_ref, h2_ref, acc_ref = rest
    j = pl.program_id(1)
    last = pl.num_programs(1) - 1
    mods = mods_ref[0, 0]

    def hidden(h2):
        return jnp.square(jnp.maximum(_dot(h2, w1_ref[0].astype(BF16)), 0.0)).astype(BF16)

    def chunk_out(h2):
        return _dot(hidden(h2), w2_ref[0].astype(BF16))

    @pl.when(j == 0)
    def _():
        for r0 in range(0, TILE, TILE // 2):
            rows = slice(r0, r0 + TILE // 2)
            x = x_ref[rows, :]
            ms = jnp.mean(x * x, axis=-1, keepdims=True)
            h = x * lax.rsqrt(ms + EPS) * g2_ref[0]
            h2 = (h * (1.0 + mods[4:5]) + mods[3:4]).astype(BF16)
            h2_ref[rows, :] = h2
            acc_ref[rows, :] = chunk_out(h2)

    @pl.when((j > 0) & (j < last))
    def _():
        acc_ref[...] += chunk_out(h2_ref[...])

    @pl.when(j == last)
    def _():
        hid = hidden(h2_ref[...])
        for c0 in range(0, D_MODEL, PROJ_COLS):
            cols = slice(c0, c0 + PROJ_COLS)
            ff = acc_ref[:, cols] + _dot(hid, w2_ref[0, :, cols].astype(BF16))
            o_ref[:, cols] = x_ref[:, cols] + mods[5:6, cols] * ff
        if n_first:
            y = o_ref[...]
            ms = jnp.mean(y * y, axis=-1, keepdims=True)
            y = y * lax.rsqrt(ms + EPS) * fg_ref[...]
            tile = pl.program_id(0)

            @pl.when(tile < n_first)
            def _():
                first_ref[...] = y

            @pl.when(tile >= n_first)
            def _():
                second_ref[...] = y


def _mlp_call(x, mods, layer, cond_row0, p, n_first=0):
    n_tiles = x.shape[0] // TILE
    if n_first:
        out_specs = [pl.BlockSpec((TILE, D_MODEL), lambda g, j: (jnp.minimum(g, n_first - 1), 0)),
                     pl.BlockSpec((TILE, D_MODEL), lambda g, j: (jnp.maximum(g - n_first, 0), 0))]
        out_shape = [jax.ShapeDtypeStruct((n_first * TILE, D_MODEL), F32),
                     jax.ShapeDtypeStruct(((n_tiles - n_first) * TILE, D_MODEL), F32)]
        extra_scratch = [pltpu.VMEM((TILE, D_MODEL), F32)]
    else:
        out_specs = pl.BlockSpec((TILE, D_MODEL), lambda g, j: (g, 0))
        out_shape = jax.ShapeDtypeStruct(x.shape, F32)
        extra_scratch = []
    return pl.pallas_call(
        functools.partial(_mlp_kernel, n_first=n_first),
        grid=(n_tiles, D_FF // FF_CHUNK),
        in_specs=[
            pl.BlockSpec((TILE, D_MODEL), lambda g, j: (g, 0)),
            pl.BlockSpec((1, 1, N_MOD, D_MODEL), lambda g, j: (layer, cond_row0(g), 0, 0)),
            pl.BlockSpec((1, 1, D_MODEL), lambda g, j: (layer, 0, 0)),
            pl.BlockSpec((1, D_MODEL, FF_CHUNK), lambda g, j: (layer, 0, j)),
            pl.BlockSpec((1, FF_CHUNK, D_MODEL), lambda g, j: (layer, j, 0)),
            pl.BlockSpec((1, D_MODEL), lambda g, j: (0, 0)),
        ],
        out_specs=out_specs,
        out_shape=out_shape,
        scratch_shapes=[pltpu.VMEM((TILE, D_MODEL), BF16), pltpu.VMEM((TILE, D_MODEL), F32)] + extra_scratch,
        compiler_params=pltpu.CompilerParams(dimension_semantics=("arbitrary", "arbitrary"),
                                             vmem_limit_bytes=VMEM_LIMIT),
        name="mlp",
    )(x, mods, p['norm2_g'], p['mlp_w1'], p['mlp_w2'], p['final_g'])


def _dft_tables(n, scale):
    k = np.arange(n)
    ang = 2.0 * np.pi * ((k[:, None] * k[None, :]) % n) / n
    return np.cos(ang) * scale, np.sin(ang) * scale


def _seq_dft(seq_len):
    c, s = _dft_tables(seq_len, 1.0 / math.sqrt(seq_len))
    return jnp.asarray(np.stack([c, -s]), F32).astype(BF16)


def _channel_dft():
    c, s = _dft_tables(FOURIER_HEAD_DIM, 1.0 / math.sqrt(FOURIER_HEAD_DIM))
    eye = np.eye(N_FOURIER_HEADS)
    return jnp.asarray(np.concatenate([np.kron(eye, c), np.kron(eye, s)], axis=1), F32).astype(BF16)


def kernel(x_prompt, x_sample, state_rglru, c, c_ctx, norm1_g, norm2_g, final_g, w_mod, b_mod, w_in, conv_w, conv_b, lru_wr, lru_br, lru_wi, lru_bi, lru_lambda, pool_w, pool_scale, fourier_w, w_out, mlp_w1, mlp_w2):
    batch, seq, _ = x_prompt.shape
    dec_batch, dec_seq, _ = x_sample.shape
    assert TILE % seq == 0 and (batch * seq) % TILE == 0 and dec_seq == TILE

    p = {
        'norm1_g': norm1_g.reshape(DEPTH, 1, D_MODEL),
        'norm2_g': norm2_g.reshape(DEPTH, 1, D_MODEL),
        'final_g': final_g.reshape(1, D_MODEL),
        'w_in': w_in.astype(BF16),
        'conv_w': conv_w,
        'conv_b': conv_b.reshape(DEPTH, 1, D_RNN),
        'lru_wr': lru_wr,
        'lru_wi': lru_wi,
        'b_gate': jnp.stack([lru_br, lru_bi], axis=2).reshape(DEPTH, 4, D_RNN),
        'lam': lru_lambda,
        'pool_w': pool_w,
        'pool_scale': pool_scale.reshape(DEPTH, 1, D_POOL),
        'cdft': _channel_dft(),
        'fourier_w': fourier_w.astype(BF16),
        'w_out': w_out.astype(BF16),
        'mlp_w1': mlp_w1,
        'mlp_w2': mlp_w2,
    }

    cond = jnp.concatenate([c_ctx[None], c, jnp.zeros((MODS_ROWS - 1 - dec_batch, D_MODEL), F32)], axis=0)
    mods = _mods_call(cond, w_mod, b_mod).reshape(DEPTH, MODS_ROWS, N_MOD, D_MODEL)

    ctx_row = lambda g: 0
    lat_row = lambda g: 1 + g
    h0_ctx = jnp.zeros((1, 1, 2, D_RNN), F32)
    dft_ctx = _seq_dft(seq)
    dft_lat = _seq_dft(dec_seq)

    n_ctx = batch * seq // TILE
    n_lat = dec_batch * dec_seq // TILE
    rows_all = (n_ctx + n_lat) * TILE
    all_row = lambda g: jnp.maximum(g - (n_ctx - 1), 0)
    x_ctx, ctx_block0 = x_prompt.reshape(batch * seq, D_MODEL), 0
    x_lat, lat_block0 = x_sample.reshape(dec_batch * dec_seq, D_MODEL), 0
    states = []
    for l in range(DEPTH):
        xa, st = _mixer_call(x_ctx, ctx_block0, n_ctx, mods, l, ctx_row, p, h0_ctx, lambda g: (0, 0, 0, 0), dft_ctx,
                             seq, True, rows_all, 0)
        states.append(st.reshape(batch, 2, D_RNN))
        (xa,) = _mixer_call(x_lat, lat_block0, n_lat, mods, l, lat_row, p, state_rglru,
                            lambda g, l=l: (g, l, 0, 0), dft_lat, dec_seq, False, rows_all, n_ctx, fill=xa)
        if l < DEPTH - 1:
            xa = _mlp_call(xa, mods, l, all_row, p)
            x_ctx, ctx_block0, x_lat, lat_block0 = xa, 0, xa, n_ctx
        else:
            xc, xs = _mlp_call(xa, mods, l, all_row, p, n_first=n_ctx)

    y_prompt = xc.reshape(batch, seq, D_MODEL)
    y_sample = xs.reshape(dec_batch, dec_seq, D_MODEL)
    new_state = jnp.stack(states, axis=1)
    return (y_prompt, y_sample, new_state)
```

```python
import functools
import math

import numpy as np
import jax
import jax.numpy as jnp
from jax import lax
from jax.experimental import pallas as pl
from jax.experimental.pallas import tpu as pltpu

D_MODEL = 1024
DEPTH = 4
GRID_W = 64
D_RNN = 512
N_LRU_HEADS = 8
LRU_HEAD_DIM = 64
LRU_C = 8.0
CONV_WIDTH = 4
CONV_LEFT = 2
D_POOL = 256
POOL_WINDOWS = (2, 4, 8, 16)
POOL_GROUP_DIM = D_POOL // len(POOL_WINDOWS)
D_FOURIER = 256
N_FOURIER_HEADS = 4
FOURIER_HEAD_DIM = 64
D_IN = 2 * D_RNN + D_POOL + D_FOURIER
D_FF = 4 * D_MODEL
N_MOD = 6
EPS = 1e-6

LANES = 128
SUBLANES = 8
TILE = 1024
CHUNK = TILE // SUBLANES
N_SLAB = D_RNN // LANES
SLAB_ROWS = TILE + SUBLANES
SCAN_GROUP = 4
SCAN_UNROLL = 4
HEADS_PER_BLOCK = 4
BLOCK = HEADS_PER_BLOCK * LRU_HEAD_DIM
N_MXU = 2
MXU_TILE = 256
PROJ_COLS = N_MXU * MXU_TILE
FF_CHUNK = 1024
MODS_ROWS = 8
MODS_COLS = 2048
VMEM_LIMIT = 60 * 1024 * 1024

F32 = jnp.float32
BF16 = jnp.bfloat16


def _dot(a, b):
    return jnp.dot(a, b, preferred_element_type=F32)


def _row_iota(shape):
    return lax.broadcasted_iota(jnp.int32, shape, 0)


def _shift_rows(x, d):
    n = x.shape[0]
    return pltpu.roll(x, (-d) % n, axis=0)


def _mods_kernel(cond_ref, w_ref, b_ref, o_ref):
    s = cond_ref[...]
    s = s * jax.nn.sigmoid(s)
    o_ref[0] = _dot(s.astype(BF16), w_ref[0].astype(BF16)) + b_ref[0]


def _mods_call(cond, w_mod, b_mod):
    n_tiles = (N_MOD * D_MODEL) // MODS_COLS
    return pl.pallas_call(
        _mods_kernel,
        grid=(DEPTH, n_tiles),
        in_specs=[
            pl.BlockSpec((MODS_ROWS, D_MODEL), lambda l, j: (0, 0)),
            pl.BlockSpec((1, D_MODEL, MODS_COLS), lambda l, j: (l, 0, j)),
            pl.BlockSpec((1, 1, MODS_COLS), lambda l, j: (l, 0, j)),
        ],
        out_specs=pl.BlockSpec((1, MODS_ROWS, MODS_COLS), lambda l, j: (l, 0, j)),
        out_shape=jax.ShapeDtypeStruct((DEPTH, MODS_ROWS, N_MOD * D_MODEL), F32),
        compiler_params=pltpu.CompilerParams(dimension_semantics=("arbitrary", "arbitrary"),
                                             vmem_limit_bytes=VMEM_LIMIT),
        name="mods",
    )(cond, w_mod, b_mod.reshape(DEPTH, 1, N_MOD * D_MODEL))


def _window_sums(x, pos, length, unit, max_half):
    fwd = x
    bwd = jnp.where(pos >= 1, _shift_rows(x, -unit), 0.0)
    sums = {1: fwd + bwd}
    k = 1
    while k < max_half:
        fwd = fwd + jnp.where(pos + k < length, _shift_rows(fwd, k * unit), 0.0)
        bwd = bwd + jnp.where(pos - k >= 0, _shift_rows(bwd, -k * unit), 0.0)
        k *= 2
        sums[k] = fwd + bwd
    return sums


def _pool_axis(x, pos, length, unit, halves):
    sums = _window_sums(x, pos, length, unit, max(halves))
    low = lax.broadcasted_iota(jnp.int32, x.shape, 1) < POOL_GROUP_DIM
    sel = jnp.where(low, sums[halves[0]], sums[halves[1]])
    half = jnp.where(low, halves[0], halves[1])
    cnt = jnp.minimum(pos + half, length) - jnp.maximum(pos - half, 0)
    return sel, cnt.astype(F32)


def _mixer_kernel(x_ref, mods_ref, g1_ref, win_ref, cw_ref, cb_ref, wr_ref, wi_ref, bg_ref, lam_ref, h0_ref,
                  pw_ref, ps_ref, cdft_ref, fw_ref, dft_ref, wout_ref,
                  xo_ref, st_ref,
                  wg_ref, pwd_ref, u_ref, xc_ref, gate_ref, scan_ref, cat_ref, hn_ref, y1_ref, pool_ref, *, seq_len):
    n_seq = TILE // seq_len
    chunks_per_seq = seq_len // CHUNK
    mods = mods_ref[0, 0]

    @pl.when(pl.program_id(0) == 0)
    def _():
        wg_ref[...] = jnp.zeros_like(wg_ref)
        pwd_ref[...] = jnp.zeros_like(pwd_ref)
        for hd in range(HEADS_PER_BLOCK):
            rows = slice(hd * LRU_HEAD_DIM, (hd + 1) * LRU_HEAD_DIM)
            pwd_ref[rows, rows] = pw_ref[0, hd].astype(BF16)
            for d in range(2):
                for gate, w_ref in enumerate((wr_ref, wi_ref)):
                    for blk in range(2):
                        w = w_ref[0, d, blk * HEADS_PER_BLOCK + hd]
                        cols = slice(gate * BLOCK + hd * LRU_HEAD_DIM, gate * BLOCK + (hd + 1) * LRU_HEAD_DIM)
                        wg_ref[d * 2 + blk, rows, cols] = (0.5 * w).astype(BF16)

    def project(lo, hi, rows=slice(0, TILE)):
        for c0 in range(lo, hi, PROJ_COLS):
            u_ref[rows, c0:c0 + PROJ_COLS] = _dot(hn_ref[rows, :], win_ref[0, :, c0:c0 + PROJ_COLS])

    for r0 in range(0, TILE, TILE // 2):
        rows = slice(r0, r0 + TILE // 2)
        x = x_ref[rows, :]
        ms = jnp.mean(x * x, axis=-1, keepdims=True)
        h = x * lax.rsqrt(ms + EPS) * g1_ref[0]
        hn_ref[rows, :] = (h * (1.0 + mods[1:2]) + mods[0:1]).astype(BF16)
        project(0, D_RNN, rows)
    project(2 * D_RNN, D_IN)

    pos = _row_iota((TILE, LANES)) & (seq_len - 1)
    taps = [(k, k - CONV_LEFT) for k in range(CONV_WIDTH) if k != CONV_LEFT]
    valid = {d: (pos + d >= 0) & (pos + d < seq_len) for _, d in taps}
    for s in range(N_SLAB):
        lanes = slice(s * LANES, (s + 1) * LANES)
        xr = u_ref[:, lanes]
        acc = cb_ref[0, :, lanes] + xr * cw_ref[0, CONV_LEFT:CONV_LEFT + 1, lanes]
        for k, d in taps:
            acc = acc + jnp.where(valid[d], _shift_rows(xr, d), 0.0) * cw_ref[0, k:k + 1, lanes]
        xc_ref[:, lanes] = acc


    def pooled_slab(k):
        halves = tuple(w // 2 for w in POOL_WINDOWS[2 * k:2 * k + 2])
        xp = u_ref[:, 2 * D_RNN + k * LANES:2 * D_RNN + (k + 1) * LANES]
        tok = _row_iota((TILE, LANES))
        if seq_len == TILE:
            n_rows = TILE // GRID_W
            col_sum, col_cnt = _pool_axis(xp, tok & (GRID_W - 1), GRID_W, 1, halves)
            win_sum, row_cnt = _pool_axis(col_sum, lax.shift_right_logical(tok, GRID_W.bit_length() - 1),
                                          n_rows, GRID_W, halves)
            cnt = row_cnt * col_cnt
        else:
            win_sum, cnt = _pool_axis(xp, tok & (seq_len - 1), seq_len, 1, halves)
        pool_ref[:, k * LANES:(k + 1) * LANES] = (win_sum / cnt - xp).astype(BF16)

    def pooling_group():
        for k in range(D_POOL // LANES):
            pooled_slab(k)
        cat_ref[:, D_RNN:D_RNN + D_POOL] = (_dot(pool_ref[...], pwd_ref[...]) * ps_ref[0]).astype(BF16)

    def channel_dft():
        xf = u_ref[:, 2 * D_RNN + D_POOL:D_IN].astype(BF16)
        y1_ref[...] = _dot(xf, cdft_ref[...]).astype(BF16)

    seq_group = min(N_MXU, n_seq)

    def fourier_rows(q):
        rows = [slice((q + k) * seq_len, (q + k + 1) * seq_len) for k in range(seq_group)]
        y_cos = jnp.concatenate([y1_ref[r, 0:D_FOURIER] for r in rows], axis=1)
        y_sin = jnp.concatenate([y1_ref[r, D_FOURIER:] for r in rows], axis=1)
        four = (_dot(dft_ref[0], y_cos) + _dot(dft_ref[1], y_sin)).astype(BF16)
        for k, r in enumerate(rows):
            cat_ref[r, D_RNN + D_POOL:] = _dot(four[:, k * D_FOURIER:(k + 1) * D_FOURIER], fw_ref[0]).astype(BF16)

    side_work = iter(
        [channel_dft, pooling_group, functools.partial(project, D_RNN, 2 * D_RNN)]
        + [functools.partial(fourier_rows, q) for q in range(0, n_seq, seq_group)])

    xcb = xc_ref[...].astype(BF16)
    for d in range(2):
        for blk in range(2):
            pre = _dot(xcb[:, blk * BLOCK:(blk + 1) * BLOCK], wg_ref[d * 2 + blk])
            for gate in range(2):
                for half in range(2):
                    lanes = slice(gate * BLOCK + half * LANES, gate * BLOCK + (half + 1) * LANES)
                    gate_ref[gate * N_SLAB + 2 * blk + half, 0:TILE, :] = pre[:, lanes]
        lam = lam_ref[0, d:d + 1, :]
        softplus_neg_lam = jnp.maximum(-lam, 0.0) + jnp.log1p(jnp.exp(-jnp.abs(lam)))
        half_rate = (-0.5 * LRU_C) * softplus_neg_lam
        hb_r = 0.5 * bg_ref[0, 2 * d:2 * d + 1, :]
        hb_i = 0.5 * bg_ref[0, 2 * d + 1:2 * d + 2, :]
        for c in range(SUBLANES):
            r0 = c * CHUNK
            for s in range(N_SLAB):
                lanes = slice(s * LANES, (s + 1) * LANES)
                t_r = jnp.tanh(gate_ref[s, pl.ds(r0, CHUNK), :] + hb_r[:, lanes])
                t_i = jnp.tanh(gate_ref[N_SLAB + s, pl.ds(r0, CHUNK), :] + hb_i[:, lanes])
                rate = half_rate[:, lanes]
                a = jnp.exp(rate + rate * t_r)
                m2 = 1.0 - a * a
                root = jnp.where(m2 > 0.0, m2 * lax.rsqrt(m2), 0.0)
                b = root * ((0.5 + 0.5 * t_i) * xc_ref[pl.ds(r0, CHUNK), lanes])
                scan_ref[2 * d, s, pl.ds(c, CHUNK, stride=SUBLANES), :] = a
                scan_ref[2 * d + 1, s, pl.ds(c, CHUNK, stride=SUBLANES), :] = b
            next(side_work, lambda: None)()
    for rest in side_work:
        rest()

    def load(arr, s, t):
        return scan_ref[arr, s, pl.ds(pl.multiple_of(t * SUBLANES, SUBLANES), SUBLANES), :]

    def compose(first, second):
        (a1, b1), (a2, b2) = first, second
        return a2 * a1, a2 * b1 + b2

    def group_steps(arr, s, t0, sign):
        return [(load(arr, s, t0 + sign * k), load(arr + 1, s, t0 + sign * k)) for k in range(SCAN_GROUP)]

    def pass1(i, carry):
        hf, pf, hb, pb = carry
        nhf, npf, nhb, npb = [], [], [], []
        for s in range(N_SLAB):
            f = group_steps(0, s, i * SCAN_GROUP, 1)
            af, bf = compose(compose(f[0], f[1]), compose(f[2], f[3]))
            nhf.append(af * hf[s] + bf)
            npf.append(af * pf[s])
            g = group_steps(2, s, CHUNK - 1 - i * SCAN_GROUP, -1)
            ab, bb = compose(compose(g[0], g[1]), compose(g[2], g[3]))
            nhb.append(ab * hb[s] + bb)
            npb.append(ab * pb[s])
        return tuple(nhf), tuple(npf), tuple(nhb), tuple(npb)

    zeros = tuple(jnp.zeros((SUBLANES, LANES), F32) for _ in range(N_SLAB))
    ones = tuple(jnp.ones((SUBLANES, LANES), F32) for _ in range(N_SLAB))
    ef, pf, eb, pb = lax.fori_loop(0, CHUNK // SCAN_GROUP, pass1, (zeros, ones, zeros, ones),
                                   unroll=SCAN_UNROLL)

    row = _row_iota((SUBLANES, LANES))
    seq_first = (row & (chunks_per_seq - 1)) == 0
    seq_last = (row & (chunks_per_seq - 1)) == chunks_per_seq - 1
    init_f, init_b = [], []
    for s in range(N_SLAB):
        h0f = jnp.broadcast_to(h0_ref[0, 0, 0:1, s * LANES:(s + 1) * LANES], (SUBLANES, LANES))
        h0b = jnp.broadcast_to(h0_ref[0, 0, 1:2, s * LANES:(s + 1) * LANES], (SUBLANES, LANES))
        cf, cb = h0f, h0b
        for _ in range(chunks_per_seq - 1):
            cf = jnp.where(seq_first, h0f, pltpu.roll(ef[s] + pf[s] * cf, 1, axis=0))
            cb = jnp.where(seq_last, h0b, pltpu.roll(eb[s] + pb[s] * cb, SUBLANES - 1, axis=0))
        init_f.append(cf)
        init_b.append(cb)

    def run_group(steps, h, out_slab, t0, sign):
        for k in range(0, SCAN_GROUP, 2):
            a0, b0 = steps[k]
            a01, b01 = compose(steps[k], steps[k + 1])
            h_even = a0 * h + b0
            h = a01 * h + b01
            for step, val in ((k, h_even), (k + 1, h)):
                t = t0 + sign * step
                gate_ref[out_slab, pl.ds(pl.multiple_of(t * SUBLANES, SUBLANES), SUBLANES), :] = val
        return h

    def pass2(i, carry):
        hf, hb = carry
        nhf, nhb = [], []
        for s in range(N_SLAB):
            tf0 = i * SCAN_GROUP
            tb0 = CHUNK - 1 - i * SCAN_GROUP
            nhf.append(run_group(group_steps(0, s, tf0, 1), hf[s], s, tf0, 1))
            nhb.append(run_group(group_steps(2, s, tb0, -1), hb[s], N_SLAB + s, tb0, -1))
        return tuple(nhf), tuple(nhb)

    lax.fori_loop(0, CHUNK // SCAN_GROUP, pass2, (tuple(init_f), tuple(init_b)), unroll=SCAN_UNROLL)

    if st_ref is not None:
        even = (_row_iota((SUBLANES, LANES)) & 1) == 0
        for s in range(N_SLAB):
            last_f = gate_ref[s, pl.ds((CHUNK - 1) * SUBLANES, SUBLANES), :]
            first_b = gate_ref[N_SLAB + s, pl.ds(0, SUBLANES), :]
            st_ref[0, :, s * LANES:(s + 1) * LANES] = jnp.where(
                even, pltpu.roll(last_f, SUBLANES - 1, axis=0), pltpu.roll(first_b, 1, axis=0))

    for c in range(SUBLANES):
        r0 = c * CHUNK
        for s in range(N_SLAB):
            hsum = (gate_ref[s, pl.ds(c, CHUNK, stride=SUBLANES), :]
                    + gate_ref[N_SLAB + s, pl.ds(c, CHUNK, stride=SUBLANES), :])
            xg = u_ref[pl.ds(r0, CHUNK), D_RNN + s * LANES:D_RNN + (s + 1) * LANES]
            cat_ref[pl.ds(r0, CHUNK), s * LANES:(s + 1) * LANES] = (hsum * jax.nn.gelu(xg)).astype(BF16)

    xo_ref[...] = x_ref[...] + mods[2:3] * _dot(cat_ref[...], wout_ref[0])


def _mixer_call(x, mods, layer, cond_row0, p, h0, h0_index, dft, seq_len, with_state):
    n_tiles = x.shape[0] // TILE
    n_seq = TILE // seq_len
    once = pl.Buffered(1)
    lyr = lambda *shape: pl.BlockSpec((1,) + shape, lambda g: (layer,) + (0,) * len(shape), pipeline_mode=once)
    in_specs = [
        pl.BlockSpec((TILE, D_MODEL), lambda g: (g, 0)),
        pl.BlockSpec((1, 1, N_MOD, D_MODEL), lambda g: (layer, cond_row0(g), 0, 0)),
        lyr(1, D_MODEL),
        lyr(D_MODEL, D_IN),
        lyr(CONV_WIDTH, D_RNN),
        lyr(1, D_RNN),
        lyr(2, N_LRU_HEADS, LRU_HEAD_DIM, LRU_HEAD_DIM),
        lyr(2, N_LRU_HEADS, LRU_HEAD_DIM, LRU_HEAD_DIM),
        lyr(4, D_RNN),
        lyr(2, D_RNN),
        pl.BlockSpec((1, 1, 2, D_RNN), h0_index),
        lyr(HEADS_PER_BLOCK, LRU_HEAD_DIM, LRU_HEAD_DIM),
        lyr(1, D_POOL),
        pl.BlockSpec((D_FOURIER, 2 * D_FOURIER), lambda g: (0, 0), pipeline_mode=once),
        lyr(D_FOURIER, D_FOURIER),
        pl.BlockSpec((2, seq_len, seq_len), lambda g: (0, 0, 0), pipeline_mode=once),
        lyr(D_MODEL, D_MODEL),
    ]
    out_specs = [pl.BlockSpec((TILE, D_MODEL), lambda g: (g, 0))]
    out_shape = [jax.ShapeDtypeStruct(x.shape, F32)]
    if with_state:
        out_specs.append(pl.BlockSpec((1, 2 * n_seq, D_RNN), lambda g: (g, 0, 0)))
        out_shape.append(jax.ShapeDtypeStruct((n_tiles, 2 * n_seq, D_RNN), F32))
        body = functools.partial(_mixer_kernel, seq_len=seq_len)
    else:
        def body(*refs):
            _mixer_kernel(*refs[:18], None, *refs[18:], seq_len=seq_len)
    return pl.pallas_call(
        body,
        grid=(n_tiles,),
        in_specs=in_specs,
        out_specs=out_specs,
        out_shape=out_shape,
        scratch_shapes=[
            pltpu.VMEM((4, BLOCK, 2 * BLOCK), BF16),
            pltpu.VMEM((D_POOL, D_POOL), BF16),
            pltpu.VMEM((TILE, D_IN), F32),
            pltpu.VMEM((TILE, D_RNN), F32),
            pltpu.VMEM((2 * N_SLAB, SLAB_ROWS, LANES), F32),
            pltpu.VMEM((4, N_SLAB, SLAB_ROWS, LANES), F32),
            pltpu.VMEM((TILE, D_MODEL), BF16),
            pltpu.VMEM((TILE, D_MODEL), BF16),
            pltpu.VMEM((TILE, 2 * D_FOURIER), BF16),
            pltpu.VMEM((TILE, D_POOL), BF16),
        ],
        compiler_params=pltpu.CompilerParams(dimension_semantics=("arbitrary",),
                                             vmem_limit_bytes=VMEM_LIMIT),
        name="mixer_ctx" if with_state else "mixer_lat",
    )(x, mods, p['norm1_g'], p['w_in'], p['conv_w'], p['conv_b'], p['lru_wr'], p['lru_wi'], p['b_gate'], p['lam'], h0,
      p['pool_w'], p['pool_scale'], p['cdft'], p['fourier_w'], dft, p['w_out'])


def _mlp_kernel(xa_ref, xb_ref, mods_ref, g2_ref, w1_ref, w2_ref, fg_ref, oa_ref, ob_ref, h2_ref, acc_ref, *,
                final, n_first):
    j = pl.program_id(1)
    last = pl.num_programs(1) - 1
    mods = mods_ref[0, 0]
    is_first = pl.program_id(0) < n_first

    def x_tile(rows, cols):
        return jnp.where(is_first, xa_ref[rows, cols], xb_ref[rows, cols])

    def hidden(h2):
        return jnp.square(jnp.maximum(_dot(h2, w1_ref[0].astype(BF16)), 0.0)).astype(BF16)

    def chunk_out(h2):
        return _dot(hidden(h2), w2_ref[0].astype(BF16))

    @pl.when(j == 0)
    def _():
        for r0 in range(0, TILE, TILE // 2):
            rows = slice(r0, r0 + TILE // 2)
            x = x_tile(rows, slice(None))
            ms = jnp.mean(x * x, axis=-1, keepdims=True)
            h = x * lax.rsqrt(ms + EPS) * g2_ref[0]
            h2 = (h * (1.0 + mods[4:5]) + mods[3:4]).astype(BF16)
            h2_ref[rows, :] = h2
            acc_ref[rows, :] = chunk_out(h2)

    @pl.when((j > 0) & (j < last))
    def _():
        acc_ref[...] += chunk_out(h2_ref[...])

    @pl.when(j == last)
    def _():
        hid = hidden(h2_ref[...])
        for c0 in range(0, D_MODEL, PROJ_COLS):
            cols = slice(c0, c0 + PROJ_COLS)
            ff = acc_ref[:, cols] + _dot(hid, w2_ref[0, :, cols].astype(BF16))
            acc_ref[:, cols] = x_tile(slice(None), cols) + mods[5:6, cols] * ff
        y = acc_ref[...]
        if final:
            ms = jnp.mean(y * y, axis=-1, keepdims=True)
            y = y * lax.rsqrt(ms + EPS) * fg_ref[...]

        @pl.when(is_first)
        def _():
            oa_ref[...] = y

        @pl.when(jnp.logical_not(is_first))
        def _():
            ob_ref[...] = y


def _mlp_call(xa, xb, mods, layer, p, final):
    n_first = xa.shape[0] // TILE
    n_tiles = n_first + xb.shape[0] // TILE
    a_block = lambda g, j: (jnp.minimum(g, n_first - 1), 0)
    b_block = lambda g, j: (jnp.maximum(g - n_first, 0), 0)
    cond_row0 = lambda g: jnp.maximum(g - (n_first - 1), 0)
    return pl.pallas_call(
        functools.partial(_mlp_kernel, final=final, n_first=n_first),
        grid=(n_tiles, D_FF // FF_CHUNK),
        in_specs=[
            pl.BlockSpec((TILE, D_MODEL), a_block),
            pl.BlockSpec((TILE, D_MODEL), b_block),
            pl.BlockSpec((1, 1, N_MOD, D_MODEL), lambda g, j: (layer, cond_row0(g), 0, 0)),
            pl.BlockSpec((1, 1, D_MODEL), lambda g, j: (layer, 0, 0)),
            pl.BlockSpec((1, D_MODEL, FF_CHUNK), lambda g, j: (layer, 0, j)),
            pl.BlockSpec((1, FF_CHUNK, D_MODEL), lambda g, j: (layer, j, 0)),
            pl.BlockSpec((1, D_MODEL), lambda g, j: (0, 0)),
        ],
        out_specs=[pl.BlockSpec((TILE, D_MODEL), a_block), pl.BlockSpec((TILE, D_MODEL), b_block)],
        out_shape=[jax.ShapeDtypeStruct(xa.shape, F32), jax.ShapeDtypeStruct(xb.shape, F32)],
        scratch_shapes=[pltpu.VMEM((TILE, D_MODEL), BF16), pltpu.VMEM((TILE, D_MODEL), F32)],
        compiler_params=pltpu.CompilerParams(dimension_semantics=("arbitrary", "arbitrary"),
                                             vmem_limit_bytes=VMEM_LIMIT),
        name="mlp",
    )(xa, xb, mods, p['norm2_g'], p['mlp_w1'], p['mlp_w2'], p['final_g'])


def _dft_tables(n, scale):
    k = np.arange(n)
    ang = 2.0 * np.pi * ((k[:, None] * k[None, :]) % n) / n
    return np.cos(ang) * scale, np.sin(ang) * scale


def _seq_dft(seq_len):
    c, s = _dft_tables(seq_len, 1.0 / math.sqrt(seq_len))
    return jnp.asarray(np.stack([c, -s]), F32).astype(BF16)


def _channel_dft():
    c, s = _dft_tables(FOURIER_HEAD_DIM, 1.0 / math.sqrt(FOURIER_HEAD_DIM))
    eye = np.eye(N_FOURIER_HEADS)
    return jnp.asarray(np.concatenate([np.kron(eye, c), np.kron(eye, s)], axis=1), F32).astype(BF16)


def kernel(x_prompt, x_sample, state_rglru, c, c_ctx, norm1_g, norm2_g, final_g, w_mod, b_mod, w_in, conv_w, conv_b, lru_wr, lru_br, lru_wi, lru_bi, lru_lambda, pool_w, pool_scale, fourier_w, w_out, mlp_w1, mlp_w2):
    batch, seq, _ = x_prompt.shape
    dec_batch, dec_seq, _ = x_sample.shape
    assert TILE % seq == 0 and (batch * seq) % TILE == 0 and dec_seq == TILE

    p = {
        'norm1_g': norm1_g.reshape(DEPTH, 1, D_MODEL),
        'norm2_g': norm2_g.reshape(DEPTH, 1, D_MODEL),
        'final_g': final_g.reshape(1, D_MODEL),
        'w_in': w_in.astype(BF16),
        'conv_w': conv_w,
        'conv_b': conv_b.reshape(DEPTH, 1, D_RNN),
        'lru_wr': lru_wr,
        'lru_wi': lru_wi,
        'b_gate': jnp.stack([lru_br, lru_bi], axis=2).reshape(DEPTH, 4, D_RNN),
        'lam': lru_lambda,
        'pool_w': pool_w,
        'pool_scale': pool_scale.reshape(DEPTH, 1, D_POOL),
        'cdft': _channel_dft(),
        'fourier_w': fourier_w.astype(BF16),
        'w_out': w_out.astype(BF16),
        'mlp_w1': mlp_w1,
        'mlp_w2': mlp_w2,
    }

    cond = jnp.concatenate([c_ctx[None], c, jnp.zeros((MODS_ROWS - 1 - dec_batch, D_MODEL), F32)], axis=0)
    mods = _mods_call(cond, w_mod, b_mod).reshape(DEPTH, MODS_ROWS, N_MOD, D_MODEL)

    ctx_row = lambda g: 0
    lat_row = lambda g: 1 + g
    h0_ctx = jnp.zeros((1, 1, 2, D_RNN), F32)
    dft_ctx = _seq_dft(seq)
    dft_lat = _seq_dft(dec_seq)

    xc = x_prompt.reshape(batch * seq, D_MODEL)
    xs = x_sample.reshape(dec_batch * dec_seq, D_MODEL)
    states = []
    for l in range(DEPTH):
        final = l == DEPTH - 1
        xc, st = _mixer_call(xc, mods, l, ctx_row, p, h0_ctx, lambda g: (0, 0, 0, 0), dft_ctx, seq, True)
        states.append(st.reshape(batch, 2, D_RNN))
        (xs,) = _mixer_call(xs, mods, l, lat_row, p, state_rglru, lambda g, l=l: (g, l, 0, 0), dft_lat,
                            dec_seq, False)
        xc, xs = _mlp_call(xc, xs, mods, l, p, final)

    y_prompt = xc.reshape(batch, seq, D_MODEL)
    y_sample = xs.reshape(dec_batch, dec_seq, D_MODEL)
    new_state = jnp.stack(states, axis=1)
    return (y_prompt, y_sample, new_state)
```

```python
import functools
import math

import numpy as np
import jax
import jax.numpy as jnp
from jax import lax
from jax.experimental import pallas as pl
from jax.experimental.pallas import tpu as pltpu

D_MODEL = 1024
DEPTH = 4
GRID_W = 64
D_RNN = 512
N_LRU_HEADS = 8
LRU_HEAD_DIM = 64
LRU_C = 8.0
CONV_WIDTH = 4
CONV_LEFT = 2
D_POOL = 256
POOL_WINDOWS = (2, 4, 8, 16)
POOL_GROUP_DIM = D_POOL // len(POOL_WINDOWS)
D_FOURIER = 256
N_FOURIER_HEADS = 4
FOURIER_HEAD_DIM = 64
D_IN = 2 * D_RNN + D_POOL + D_FOURIER
D_FF = 4 * D_MODEL
N_MOD = 6
EPS = 1e-6

LANES = 128
SUBLANES = 8
TILE = 1024
CHUNK = TILE // SUBLANES
N_SLAB = D_RNN // LANES
SLAB_ROWS = TILE + SUBLANES
SCAN_GROUP = 4
SCAN_UNROLL = 4
HEADS_PER_BLOCK = 4
BLOCK = HEADS_PER_BLOCK * LRU_HEAD_DIM
N_MXU = 2
MXU_TILE = 256
PROJ_COLS = N_MXU * MXU_TILE
FF_CHUNK = 1024
MODS_ROWS = 8
MODS_COLS = 2048
VMEM_LIMIT = 60 * 1024 * 1024

F32 = jnp.float32
BF16 = jnp.bfloat16


def _dot(a, b):
    return jnp.dot(a, b, preferred_element_type=F32)


def _row_iota(shape):
    return lax.broadcasted_iota(jnp.int32, shape, 0)


def _shift_rows(x, d):
    n = x.shape[0]
    return pltpu.roll(x, (-d) % n, axis=0)


def _mods_kernel(cond_ref, w_ref, b_ref, o_ref):
    s = cond_ref[...]
    s = s * jax.nn.sigmoid(s)
    o_ref[0] = _dot(s.astype(BF16), w_ref[0].astype(BF16)) + b_ref[0]


def _mods_call(cond, w_mod, b_mod):
    n_tiles = (N_MOD * D_MODEL) // MODS_COLS
    return pl.pallas_call(
        _mods_kernel,
        grid=(DEPTH, n_tiles),
        in_specs=[
            pl.BlockSpec((MODS_ROWS, D_MODEL), lambda l, j: (0, 0)),
            pl.BlockSpec((1, D_MODEL, MODS_COLS), lambda l, j: (l, 0, j)),
            pl.BlockSpec((1, 1, MODS_COLS), lambda l, j: (l, 0, j)),
        ],
        out_specs=pl.BlockSpec((1, MODS_ROWS, MODS_COLS), lambda l, j: (l, 0, j)),
        out_shape=jax.ShapeDtypeStruct((DEPTH, MODS_ROWS, N_MOD * D_MODEL), F32),
        compiler_params=pltpu.CompilerParams(dimension_semantics=("arbitrary", "arbitrary"),
                                             vmem_limit_bytes=VMEM_LIMIT),
        name="mods",
    )(cond, w_mod, b_mod.reshape(DEPTH, 1, N_MOD * D_MODEL))


def _window_sums(x, pos, length, unit, max_half):
    fwd = x
    bwd = jnp.where(pos >= 1, _shift_rows(x, -unit), 0.0)
    sums = {1: fwd + bwd}
    k = 1
    while k < max_half:
        fwd = fwd + jnp.where(pos + k < length, _shift_rows(fwd, k * unit), 0.0)
        bwd = bwd + jnp.where(pos - k >= 0, _shift_rows(bwd, -k * unit), 0.0)
        k *= 2
        sums[k] = fwd + bwd
    return sums


def _pool_axis(x, pos, length, unit, halves):
    sums = _window_sums(x, pos, length, unit, max(halves))
    low = lax.broadcasted_iota(jnp.int32, x.shape, 1) < POOL_GROUP_DIM
    sel = jnp.where(low, sums[halves[0]], sums[halves[1]])
    half = jnp.where(low, halves[0], halves[1])
    cnt = jnp.minimum(pos + half, length) - jnp.maximum(pos - half, 0)
    return sel, cnt.astype(F32)


def _mixer_kernel(x_ref, mods_ref, g1_ref, win_ref, cw_ref, cb_ref, wr_ref, wi_ref, bg_ref, lam_ref, h0_ref,
                  pw_ref, ps_ref, cdft_ref, fw_ref, dft_ref, wout_ref,
                  xo_ref, st_ref,
                  wg_ref, pwd_ref, u_ref, xc_ref, gate_ref, scan_ref, cat_ref, hn_ref, y1_ref, pool_ref, *, seq_len):
    n_seq = TILE // seq_len
    chunks_per_seq = seq_len // CHUNK
    mods = mods_ref[0, 0]

    @pl.when(pl.program_id(0) == 0)
    def _():
        wg_ref[...] = jnp.zeros_like(wg_ref)
        pwd_ref[...] = jnp.zeros_like(pwd_ref)
        for hd in range(HEADS_PER_BLOCK):
            rows = slice(hd * LRU_HEAD_DIM, (hd + 1) * LRU_HEAD_DIM)
            pwd_ref[rows, rows] = pw_ref[0, hd].astype(BF16)
            for d in range(2):
                for gate, w_ref in enumerate((wr_ref, wi_ref)):
                    for blk in range(2):
                        w = w_ref[0, d, blk * HEADS_PER_BLOCK + hd]
                        cols = slice(gate * BLOCK + hd * LRU_HEAD_DIM, gate * BLOCK + (hd + 1) * LRU_HEAD_DIM)
                        wg_ref[d * 2 + blk, rows, cols] = (0.5 * w).astype(BF16)

    def project(lo, hi, rows=slice(0, TILE)):
        for c0 in range(lo, hi, PROJ_COLS):
            u_ref[rows, c0:c0 + PROJ_COLS] = _dot(hn_ref[rows, :], win_ref[0, :, c0:c0 + PROJ_COLS])

    gain = g1_ref[0] * (1.0 + mods[1:2])
    for r0 in range(0, TILE, TILE // 2):
        rows = slice(r0, r0 + TILE // 2)
        x = x_ref[rows, :]
        ms = jnp.mean(x * x, axis=-1, keepdims=True)
        hn_ref[rows, :] = ((x * lax.rsqrt(ms + EPS)) * gain + mods[0:1]).astype(BF16)
        project(0, D_RNN, rows)
    project(2 * D_RNN, D_IN)

    pos = _row_iota((TILE, LANES)) & (seq_len - 1)
    taps = [(k, k - CONV_LEFT) for k in range(CONV_WIDTH) if k != CONV_LEFT]
    valid = {d: (pos + d >= 0) & (pos + d < seq_len) for _, d in taps}
    for s in range(N_SLAB):
        lanes = slice(s * LANES, (s + 1) * LANES)
        xr = u_ref[:, lanes]
        acc = cb_ref[0, :, lanes] + xr * cw_ref[0, CONV_LEFT:CONV_LEFT + 1, lanes]
        for k, d in taps:
            acc = acc + jnp.where(valid[d], _shift_rows(xr, d), 0.0) * cw_ref[0, k:k + 1, lanes]
        xc_ref[:, lanes] = acc


    def pooled_slab(k):
        halves = tuple(w // 2 for w in POOL_WINDOWS[2 * k:2 * k + 2])
        xp = u_ref[:, 2 * D_RNN + k * LANES:2 * D_RNN + (k + 1) * LANES]
        tok = _row_iota((TILE, LANES))
        if seq_len == TILE:
            n_rows = TILE // GRID_W
            col_sum, col_cnt = _pool_axis(xp, tok & (GRID_W - 1), GRID_W, 1, halves)
            win_sum, row_cnt = _pool_axis(col_sum, lax.shift_right_logical(tok, GRID_W.bit_length() - 1),
                                          n_rows, GRID_W, halves)
            cnt = row_cnt * col_cnt
        else:
            win_sum, cnt = _pool_axis(xp, tok & (seq_len - 1), seq_len, 1, halves)
        pool_ref[:, k * LANES:(k + 1) * LANES] = (win_sum / cnt - xp).astype(BF16)

    def pooling_group():
        for k in range(D_POOL // LANES):
            pooled_slab(k)
        cat_ref[:, D_RNN:D_RNN + D_POOL] = (_dot(pool_ref[...], pwd_ref[...]) * ps_ref[0]).astype(BF16)

    def channel_dft():
        xf = u_ref[:, 2 * D_RNN + D_POOL:D_IN].astype(BF16)
        y1_ref[...] = _dot(xf, cdft_ref[...]).astype(BF16)

    seq_group = min(N_MXU, n_seq)

    def fourier_rows(q):
        rows = [slice((q + k) * seq_len, (q + k + 1) * seq_len) for k in range(seq_group)]
        y_cos = jnp.concatenate([y1_ref[r, 0:D_FOURIER] for r in rows], axis=1)
        y_sin = jnp.concatenate([y1_ref[r, D_FOURIER:] for r in rows], axis=1)
        four = (_dot(dft_ref[0], y_cos) + _dot(dft_ref[1], y_sin)).astype(BF16)
        for k, r in enumerate(rows):
            cat_ref[r, D_RNN + D_POOL:] = _dot(four[:, k * D_FOURIER:(k + 1) * D_FOURIER], fw_ref[0]).astype(BF16)

    side_work = iter(
        [channel_dft, pooling_group, functools.partial(project, D_RNN, 2 * D_RNN)]
        + [functools.partial(fourier_rows, q) for q in range(0, n_seq, seq_group)])

    xcb = xc_ref[...].astype(BF16)
    for d in range(2):
        for blk in range(2):
            pre = _dot(xcb[:, blk * BLOCK:(blk + 1) * BLOCK], wg_ref[d * 2 + blk])
            for gate in range(2):
                for half in range(2):
                    lanes = slice(gate * BLOCK + half * LANES, gate * BLOCK + (half + 1) * LANES)
                    gate_ref[gate * N_SLAB + 2 * blk + half, 0:TILE, :] = pre[:, lanes]
        lam = lam_ref[0, d:d + 1, :]
        softplus_neg_lam = jnp.maximum(-lam, 0.0) + jnp.log1p(jnp.exp(-jnp.abs(lam)))
        half_rate = (-0.5 * LRU_C) * softplus_neg_lam
        hb_r = 0.5 * bg_ref[0, 2 * d:2 * d + 1, :]
        hb_i = 0.5 * bg_ref[0, 2 * d + 1:2 * d + 2, :]
        for c in range(SUBLANES):
            r0 = c * CHUNK
            for s in range(N_SLAB):
                lanes = slice(s * LANES, (s + 1) * LANES)
                t_r = jnp.tanh(gate_ref[s, pl.ds(r0, CHUNK), :] + hb_r[:, lanes])
                t_i = jnp.tanh(gate_ref[N_SLAB + s, pl.ds(r0, CHUNK), :] + hb_i[:, lanes])
                rate = half_rate[:, lanes]
                a = jnp.exp(rate + rate * t_r)
                m2 = 1.0 - a * a
                root = jnp.where(m2 > 0.0, m2 * lax.rsqrt(m2), 0.0)
                b = root * ((0.5 + 0.5 * t_i) * xc_ref[pl.ds(r0, CHUNK), lanes])
                scan_ref[2 * d, s, pl.ds(c, CHUNK, stride=SUBLANES), :] = a
                scan_ref[2 * d + 1, s, pl.ds(c, CHUNK, stride=SUBLANES), :] = b
            next(side_work, lambda: None)()
    for rest in side_work:
        rest()

    def load(arr, s, t):
        return scan_ref[arr, s, pl.ds(pl.multiple_of(t * SUBLANES, SUBLANES), SUBLANES), :]

    def compose(first, second):
        (a1, b1), (a2, b2) = first, second
        return a2 * a1, a2 * b1 + b2

    def group_steps(arr, s, t0, sign):
        return [(load(arr, s, t0 + sign * k), load(arr + 1, s, t0 + sign * k)) for k in range(SCAN_GROUP)]

    def pass1(i, carry):
        hf, pf, hb, pb = carry
        nhf, npf, nhb, npb = [], [], [], []
        for s in range(N_SLAB):
            f = group_steps(0, s, i * SCAN_GROUP, 1)
            af, bf = compose(compose(f[0], f[1]), compose(f[2], f[3]))
            nhf.append(af * hf[s] + bf)
            npf.append(af * pf[s])
            g = group_steps(2, s, CHUNK - 1 - i * SCAN_GROUP, -1)
            ab, bb = compose(compose(g[0], g[1]), compose(g[2], g[3]))
            nhb.append(ab * hb[s] + bb)
            npb.append(ab * pb[s])
        return tuple(nhf), tuple(npf), tuple(nhb), tuple(npb)

    zeros = tuple(jnp.zeros((SUBLANES, LANES), F32) for _ in range(N_SLAB))
    ones = tuple(jnp.ones((SUBLANES, LANES), F32) for _ in range(N_SLAB))
    ef, pf, eb, pb = lax.fori_loop(0, CHUNK // SCAN_GROUP, pass1, (zeros, ones, zeros, ones),
                                   unroll=SCAN_UNROLL)

    row = _row_iota((SUBLANES, LANES))
    seq_first = (row & (chunks_per_seq - 1)) == 0
    seq_last = (row & (chunks_per_seq - 1)) == chunks_per_seq - 1
    init_f, init_b = [], []
    for s in range(N_SLAB):
        h0f = jnp.broadcast_to(h0_ref[0, 0, 0:1, s * LANES:(s + 1) * LANES], (SUBLANES, LANES))
        h0b = jnp.broadcast_to(h0_ref[0, 0, 1:2, s * LANES:(s + 1) * LANES], (SUBLANES, LANES))
        cf, cb = h0f, h0b
        for _ in range(chunks_per_seq - 1):
            cf = jnp.where(seq_first, h0f, pltpu.roll(ef[s] + pf[s] * cf, 1, axis=0))
            cb = jnp.where(seq_last, h0b, pltpu.roll(eb[s] + pb[s] * cb, SUBLANES - 1, axis=0))
        init_f.append(cf)
        init_b.append(cb)

    def run_group(steps, h, out_slab, t0, sign):
        for k in range(0, SCAN_GROUP, 2):
            a0, b0 = steps[k]
            a01, b01 = compose(steps[k], steps[k + 1])
            h_even = a0 * h + b0
            h = a01 * h + b01
            for step, val in ((k, h_even), (k + 1, h)):
                t = t0 + sign * step
                gate_ref[out_slab, pl.ds(pl.multiple_of(t * SUBLANES, SUBLANES), SUBLANES), :] = val
        return h

    def pass2(i, carry):
        hf, hb = carry
        nhf, nhb = [], []
        for s in range(N_SLAB):
            tf0 = i * SCAN_GROUP
            tb0 = CHUNK - 1 - i * SCAN_GROUP
            nhf.append(run_group(group_steps(0, s, tf0, 1), hf[s], s, tf0, 1))
            nhb.append(run_group(group_steps(2, s, tb0, -1), hb[s], N_SLAB + s, tb0, -1))
        return tuple(nhf), tuple(nhb)

    lax.fori_loop(0, CHUNK // SCAN_GROUP, pass2, (tuple(init_f), tuple(init_b)), unroll=SCAN_UNROLL)

    if st_ref is not None:
        even = (_row_iota((SUBLANES, LANES)) & 1) == 0
        for s in range(N_SLAB):
            last_f = gate_ref[s, pl.ds((CHUNK - 1) * SUBLANES, SUBLANES), :]
            first_b = gate_ref[N_SLAB + s, pl.ds(0, SUBLANES), :]
            st_ref[0, :, s * LANES:(s + 1) * LANES] = jnp.where(
                even, pltpu.roll(last_f, SUBLANES - 1, axis=0), pltpu.roll(first_b, 1, axis=0))

    for c in range(SUBLANES):
        r0 = c * CHUNK
        for s in range(N_SLAB):
            hsum = (gate_ref[s, pl.ds(c, CHUNK, stride=SUBLANES), :]
                    + gate_ref[N_SLAB + s, pl.ds(c, CHUNK, stride=SUBLANES), :])
            xg = u_ref[pl.ds(r0, CHUNK), D_RNN + s * LANES:D_RNN + (s + 1) * LANES]
            cat_ref[pl.ds(r0, CHUNK), s * LANES:(s + 1) * LANES] = (hsum * jax.nn.gelu(xg)).astype(BF16)

    xo_ref[...] = x_ref[...] + mods[2:3] * _dot(cat_ref[...], wout_ref[0])


def _mixer_call(x, mods, layer, cond_row0, p, h0, h0_index, dft, seq_len, with_state):
    n_tiles = x.shape[0] // TILE
    n_seq = TILE // seq_len
    once = pl.Buffered(1)
    lyr = lambda *shape: pl.BlockSpec((1,) + shape, lambda g: (layer,) + (0,) * len(shape), pipeline_mode=once)
    in_specs = [
        pl.BlockSpec((TILE, D_MODEL), lambda g: (g, 0)),
        pl.BlockSpec((1, 1, N_MOD, D_MODEL), lambda g: (layer, cond_row0(g), 0, 0)),
        lyr(1, D_MODEL),
        lyr(D_MODEL, D_IN),
        lyr(CONV_WIDTH, D_RNN),
        lyr(1, D_RNN),
        lyr(2, N_LRU_HEADS, LRU_HEAD_DIM, LRU_HEAD_DIM),
        lyr(2, N_LRU_HEADS, LRU_HEAD_DIM, LRU_HEAD_DIM),
        lyr(4, D_RNN),
        lyr(2, D_RNN),
        pl.BlockSpec((1, 1, 2, D_RNN), h0_index),
        lyr(HEADS_PER_BLOCK, LRU_HEAD_DIM, LRU_HEAD_DIM),
        lyr(1, D_POOL),
        pl.BlockSpec((D_FOURIER, 2 * D_FOURIER), lambda g: (0, 0), pipeline_mode=once),
        lyr(D_FOURIER, D_FOURIER),
        pl.BlockSpec((2, seq_len, seq_len), lambda g: (0, 0, 0), pipeline_mode=once),
        lyr(D_MODEL, D_MODEL),
    ]
    out_specs = [pl.BlockSpec((TILE, D_MODEL), lambda g: (g, 0))]
    out_shape = [jax.ShapeDtypeStruct(x.shape, F32)]
    if with_state:
        out_specs.append(pl.BlockSpec((1, 2 * n_seq, D_RNN), lambda g: (g, 0, 0)))
        out_shape.append(jax.ShapeDtypeStruct((n_tiles, 2 * n_seq, D_RNN), F32))
        body = functools.partial(_mixer_kernel, seq_len=seq_len)
    else:
        def body(*refs):
            _mixer_kernel(*refs[:18], None, *refs[18:], seq_len=seq_len)
    return pl.pallas_call(
        body,
        grid=(n_tiles,),
        in_specs=in_specs,
        out_specs=out_specs,
        out_shape=out_shape,
        scratch_shapes=[
            pltpu.VMEM((4, BLOCK, 2 * BLOCK), BF16),
            pltpu.VMEM((D_POOL, D_POOL), BF16),
            pltpu.VMEM((TILE, D_IN), F32),
            pltpu.VMEM((TILE, D_RNN), F32),
            pltpu.VMEM((2 * N_SLAB, SLAB_ROWS, LANES), F32),
            pltpu.VMEM((4, N_SLAB, SLAB_ROWS, LANES), F32),
            pltpu.VMEM((TILE, D_MODEL), BF16),
            pltpu.VMEM((TILE, D_MODEL), BF16),
            pltpu.VMEM((TILE, 2 * D_FOURIER), BF16),
            pltpu.VMEM((TILE, D_POOL), BF16),
        ],
        compiler_params=pltpu.CompilerParams(dimension_semantics=("arbitrary",),
                                             vmem_limit_bytes=VMEM_LIMIT),
        name="mixer_ctx" if with_state else "mixer_lat",
    )(x, mods, p['norm1_g'], p['w_in'], p['conv_w'], p['conv_b'], p['lru_wr'], p['lru_wi'], p['b_gate'], p['lam'], h0,
      p['pool_w'], p['pool_scale'], p['cdft'], p['fourier_w'], dft, p['w_out'])


def _mlp_kernel(x_ref, mods_ref, g2_ref, w1_ref, w2_ref, fg_ref, o_ref, h2_ref, acc_ref, *, final):
    j = pl.program_id(1)
    last = pl.num_programs(1) - 1
    mods = mods_ref[0, 0]

    def hidden(h2):
        return jnp.square(jnp.maximum(_dot(h2, w1_ref[0].astype(BF16)), 0.0)).astype(BF16)

    def chunk_out(h2):
        return _dot(hidden(h2), w2_ref[0].astype(BF16))

    @pl.when(j == 0)
    def _():
        for r0 in range(0, TILE, TILE // 2):
            rows = slice(r0, r0 + TILE // 2)
            x = x_ref[rows, :]
            ms = jnp.mean(x * x, axis=-1, keepdims=True)
            gain = g2_ref[0] * (1.0 + mods[4:5])
            h2 = ((x * lax.rsqrt(ms + EPS)) * gain + mods[3:4]).astype(BF16)
            h2_ref[rows, :] = h2
            acc_ref[rows, :] = chunk_out(h2)

    @pl.when((j > 0) & (j < last))
    def _():
        acc_ref[...] += chunk_out(h2_ref[...])

    @pl.when(j == last)
    def _():
        hid = hidden(h2_ref[...])
        for c0 in range(0, D_MODEL, PROJ_COLS):
            cols = slice(c0, c0 + PROJ_COLS)
            ff = acc_ref[:, cols] + _dot(hid, w2_ref[0, :, cols].astype(BF16))
            o_ref[:, cols] = x_ref[:, cols] + mods[5:6, cols] * ff
        if final:
            y = o_ref[...]
            ms = jnp.mean(y * y, axis=-1, keepdims=True)
            o_ref[...] = y * lax.rsqrt(ms + EPS) * fg_ref[...]


def _mlp_call(x, mods, layer, cond_row0, p, final):
    n_tiles = x.shape[0] // TILE
    return pl.pallas_call(
        functools.partial(_mlp_kernel, final=final),
        grid=(n_tiles, D_FF // FF_CHUNK),
        in_specs=[
            pl.BlockSpec((TILE, D_MODEL), lambda g, j: (g, 0)),
            pl.BlockSpec((1, 1, N_MOD, D_MODEL), lambda g, j: (layer, cond_row0(g), 0, 0)),
            pl.BlockSpec((1, 1, D_MODEL), lambda g, j: (layer, 0, 0)),
            pl.BlockSpec((1, D_MODEL, FF_CHUNK), lambda g, j: (layer, 0, j)),
            pl.BlockSpec((1, FF_CHUNK, D_MODEL), lambda g, j: (layer, j, 0)),
            pl.BlockSpec((1, D_MODEL), lambda g, j: (0, 0)),
        ],
        out_specs=pl.BlockSpec((TILE, D_MODEL), lambda g, j: (g, 0)),
        out_shape=jax.ShapeDtypeStruct(x.shape, F32),
        scratch_shapes=[pltpu.VMEM((TILE, D_MODEL), BF16), pltpu.VMEM((TILE, D_MODEL), F32)],
        compiler_params=pltpu.CompilerParams(dimension_semantics=("arbitrary", "arbitrary"),
                                             vmem_limit_bytes=VMEM_LIMIT),
        name="mlp",
    )(x, mods, p['norm2_g'], p['mlp_w1'], p['mlp_w2'], p['final_g'])


def _dft_tables(n, scale):
    k = np.arange(n)
    ang = 2.0 * np.pi * ((k[:, None] * k[None, :]) % n) / n
    return np.cos(ang) * scale, np.sin(ang) * scale


def _seq_dft(seq_len):
    c, s = _dft_tables(seq_len, 1.0 / math.sqrt(seq_len))
    return jnp.asarray(np.stack([c, -s]), F32).astype(BF16)


def _channel_dft():
    c, s = _dft_tables(FOURIER_HEAD_DIM, 1.0 / math.sqrt(FOURIER_HEAD_DIM))
    eye = np.eye(N_FOURIER_HEADS)
    return jnp.asarray(np.concatenate([np.kron(eye, c), np.kron(eye, s)], axis=1), F32).astype(BF16)


def kernel(x_prompt, x_sample, state_rglru, c, c_ctx, norm1_g, norm2_g, final_g, w_mod, b_mod, w_in, conv_w, conv_b, lru_wr, lru_br, lru_wi, lru_bi, lru_lambda, pool_w, pool_scale, fourier_w, w_out, mlp_w1, mlp_w2):
    batch, seq, _ = x_prompt.shape
    dec_batch, dec_seq, _ = x_sample.shape
    assert TILE % seq == 0 and (batch * seq) % TILE == 0 and dec_seq == TILE

    p = {
        'norm1_g': norm1_g.reshape(DEPTH, 1, D_MODEL),
        'norm2_g': norm2_g.reshape(DEPTH, 1, D_MODEL),
        'final_g': final_g.reshape(1, D_MODEL),
        'w_in': w_in.astype(BF16),
        'conv_w': conv_w,
        'conv_b': conv_b.reshape(DEPTH, 1, D_RNN),
        'lru_wr': lru_wr,
        'lru_wi': lru_wi,
        'b_gate': jnp.stack([lru_br, lru_bi], axis=2).reshape(DEPTH, 4, D_RNN),
        'lam': lru_lambda,
        'pool_w': pool_w,
        'pool_scale': pool_scale.reshape(DEPTH, 1, D_POOL),
        'cdft': _channel_dft(),
        'fourier_w': fourier_w.astype(BF16),
        'w_out': w_out.astype(BF16),
        'mlp_w1': mlp_w1,
        'mlp_w2': mlp_w2,
    }

    cond = jnp.concatenate([c_ctx[None], c, jnp.zeros((MODS_ROWS - 1 - dec_batch, D_MODEL), F32)], axis=0)
    mods = _mods_call(cond, w_mod, b_mod).reshape(DEPTH, MODS_ROWS, N_MOD, D_MODEL)

    ctx_row = lambda g: 0
    lat_row = lambda g: 1 + g
    h0_ctx = jnp.zeros((1, 1, 2, D_RNN), F32)
    dft_ctx = _seq_dft(seq)
    dft_lat = _seq_dft(dec_seq)

    xc = x_prompt.reshape(batch * seq, D_MODEL)
    xs = x_sample.reshape(dec_batch * dec_seq, D_MODEL)
    states = []
    for l in range(DEPTH):
        final = l == DEPTH - 1
        xc, st = _mixer_call(xc, mods, l, ctx_row, p, h0_ctx, lambda g: (0, 0, 0, 0), dft_ctx, seq, True)
        states.append(st.reshape(batch, 2, D_RNN))
        xc = _mlp_call(xc, mods, l, ctx_row, p, final)
        (xs,) = _mixer_call(xs, mods, l, lat_row, p, state_rglru, lambda g, l=l: (g, l, 0, 0), dft_lat,
                            dec_seq, False)
        xs = _mlp_call(xs, mods, l, lat_row, p, final)

    y_prompt = xc.reshape(batch, seq, D_MODEL)
    y_sample = xs.reshape(dec_batch, dec_seq, D_MODEL)
    new_state = jnp.stack(states, axis=1)
    return (y_prompt, y_sample, new_state)
```
